```python
import jax
import jax.numpy as jnp
from jax import lax
import numpy as np

D_MODEL = 2048
BATCH = 4
SEQ = 4096
DEPTH = 2

EPS = 1e-6
NEG_INF = -1e30
CONV_CH = D_MODEL // 2
CONV_GROUPS = 8
CONV_WIDTH = 31
ATT_HEADS = 8
HEAD_DIM = 128
ATT_W = ATT_HEADS * HEAD_DIM
ATT_BRANCHES = ((128, 1), (512, 4), (2048, 16))
ATT_BLK = 64
ALIBI_MAX = 8.0
LRU_WIDTH = (4 * D_MODEL // 3) // 128 * 128
LRU_BLOCKS = 16
LRU_BW = LRU_WIDTH // LRU_BLOCKS
LRU_CONV = 4
LRU_C = 8.0
N_GROUPS = 4
EXPERTS_PER_GROUP = 8
N_EXPERTS = N_GROUPS * EXPERTS_PER_GROUP
TOP_K = 2
EXPERT_FF = D_MODEL // 2
MOE_BLK = 256

kernel_name = "hybrid_conv_dilattn_rglru_hmoe_encoder"


def rms_norm(x, g):
    xf = x.astype(jnp.float32)
    y = xf * lax.rsqrt(jnp.mean(xf * xf, axis=-1, keepdims=True) + EPS)
    return (y * g.astype(jnp.float32)).astype(x.dtype)


def group_layer_norm(x, g, b, groups):
    shp = x.shape
    xf = x.astype(jnp.float32).reshape(*shp[:-1], groups, shp[-1] // groups)
    mu = jnp.mean(xf, axis=-1, keepdims=True)
    var = jnp.mean(jnp.square(xf - mu), axis=-1, keepdims=True)
    y = ((xf - mu) * lax.rsqrt(var + EPS)).reshape(shp)
    return (y * g.astype(jnp.float32) + b.astype(jnp.float32)).astype(x.dtype)


def depthwise_conv(x, w, b, pad_left, pad_right):
    y = lax.conv_general_dilated(
        x, w[:, None, :].astype(x.dtype), window_strides=(1,),
        padding=[(pad_left, pad_right)], dimension_numbers=("NWC", "WIO", "NWC"),
        feature_group_count=x.shape[-1])
    return y + b


def ada_modulation(c, w, b):
    mod = jax.nn.silu(c) @ w + b
    shift, scale, gate = jnp.split(mod, 3, axis=-1)
    return shift[:, None, :], scale[:, None, :], gate[:, None, :]


def modulated_residual(x, c, norm_g, ada_w, ada_b, fn):
    shift, scale, gate = ada_modulation(c, ada_w, ada_b)
    h = rms_norm(x, norm_g) * (1.0 + scale) + shift
    return x + gate * fn(h)


def dilated_window_attention(q, k, v, dilation, radius, slopes):
    bsz, seq, heads, dh = q.shape
    sub = seq // dilation
    nblk = -(-sub // ATT_BLK)
    padded = nblk * ATT_BLK

    def by_residue(t):
        t = t.reshape(bsz, sub, dilation, heads, dh).transpose(0, 2, 1, 3, 4)
        return jnp.pad(t, ((0, 0), (0, 0), (0, padded - sub), (0, 0), (0, 0)))

    def neighbour_blocks(t):
        tp = jnp.pad(t, ((0, 0), (0, 0), (ATT_BLK, ATT_BLK), (0, 0), (0, 0)))
        tp = tp.reshape(bsz, dilation, nblk + 2, ATT_BLK, heads, dh)
        return jnp.concatenate([tp[:, :, :-2], tp[:, :, 1:-1], tp[:, :, 2:]], axis=3)

    qb = by_residue(q).reshape(bsz, dilation, nblk, ATT_BLK, heads, dh)
    kb = neighbour_blocks(by_residue(k))
    vb = neighbour_blocks(by_residue(v))

    m_idx = jnp.arange(padded).reshape(nblk, ATT_BLK, 1)
    n_idx = ((jnp.arange(nblk) - 1) * ATT_BLK)[:, None, None] + jnp.arange(3 * ATT_BLK)[None, None, :]
    rel = jnp.abs(m_idx - n_idx)
    valid = (rel <= radius) & (n_idx >= 0) & (n_idx < sub)
    alibi = slopes[None, :, None, None] * (rel * dilation).astype(jnp.float32)[:, None]

    s = jnp.einsum("brnihe,brnjhe->brnhij", qb, kb).astype(jnp.float32) * (dh ** -0.5) - alibi
    s = jnp.where(valid[:, None], s, NEG_INF)
    mx = jnp.max(s, axis=-1, keepdims=True)
    e = jnp.exp(s - mx)
    den = jnp.sum(e, axis=-1, keepdims=True)
    o = jnp.einsum("brnhij,brnjhe->brnihe", (e / den).astype(v.dtype), vb)
    lse = (mx + jnp.log(den))[..., 0].transpose(0, 1, 2, 4, 3)

    def to_sequence(t):
        t = t.reshape(bsz, dilation, padded, *t.shape[4:])[:, :, :sub]
        t = jnp.swapaxes(t, 1, 2)
        return t.reshape(bsz, seq, *t.shape[3:])

    return to_sequence(o), to_sequence(lse)


def conv_attention_mixer(h, w_in, conv_w, conv_b, conv_norm_g, conv_norm_b, w_out):
    bsz, seq, _ = h.shape
    split_at = [CONV_CH, 2 * CONV_CH, 2 * CONV_CH + ATT_W, 2 * CONV_CH + 2 * ATT_W]
    glu_v, glu_g, q, k, v = jnp.split(h @ w_in, split_at, axis=-1)
    a = glu_v * jax.nn.sigmoid(glu_g)
    a = depthwise_conv(a, conv_w, conv_b, CONV_WIDTH // 2, CONV_WIDTH // 2)
    a = jax.nn.silu(group_layer_norm(a, conv_norm_g, conv_norm_b, CONV_GROUPS))
    q, k, v = (t.reshape(bsz, seq, ATT_HEADS, HEAD_DIM) for t in (q, k, v))
    slopes = 2.0 ** (-ALIBI_MAX * jnp.arange(1, ATT_HEADS + 1, dtype=jnp.float32) / ATT_HEADS)
    outs, lses = [], []
    for window, dilation in ATT_BRANCHES:
        o_br, lse_br = dilated_window_attention(q, k, v, dilation, window // (2 * dilation), slopes)
        outs.append(o_br)
        lses.append(lse_br)
    wts = jax.nn.softmax(jnp.stack(lses), axis=0)[..., None]
    o = jnp.sum(wts * jnp.stack(outs).astype(jnp.float32), axis=0).astype(h.dtype)
    return jnp.concatenate([a, o.reshape(bsz, seq, ATT_W)], axis=-1) @ w_out


def block_diagonal(x, w, b):
    xb = x.reshape(*x.shape[:-1], LRU_BLOCKS, LRU_BW)
    return jnp.einsum("bsnc,ncd->bsnd", xb, w).reshape(x.shape) + b


def linear_recurrence_combine(left, right):
    a_l, b_l = left
    a_r, b_r = right
    return a_l * a_r, a_r * b_l + b_r


def rglru(x, w_a, b_a, w_x, b_x, lam, reverse):
    r = jax.nn.sigmoid(block_diagonal(x, w_a, b_a).astype(jnp.float32))
    i = jax.nn.sigmoid(block_diagonal(x, w_x, b_x).astype(jnp.float32))
    log_a = -LRU_C * r * jax.nn.softplus(-lam.astype(jnp.float32))
    a = jnp.exp(log_a)
    u = jnp.sqrt(-jnp.expm1(2.0 * log_a)) * (i * x.astype(jnp.float32))
    _, hs = lax.associative_scan(linear_recurrence_combine, (a, u), axis=1, reverse=reverse)
    return hs.astype(x.dtype)


def rglru_mixer(h, w_in, conv_w, conv_b, w_a, b_a, w_x, b_x, lam, w_out):
    gate, xr = jnp.split(h @ w_in, 2, axis=-1)
    xr = depthwise_conv(xr, conv_w, conv_b, LRU_CONV // 2, LRU_CONV - 1 - LRU_CONV // 2)
    y = (rglru(xr, w_a[0], b_a[0], w_x[0], b_x[0], lam[0], reverse=False)
         + rglru(xr, w_a[1], b_a[1], w_x[1], b_x[1], lam[1], reverse=True))
    return (jax.nn.gelu(gate) * y) @ w_out


def hierarchical_moe(h, w_group, w_expert, w_gate_up, w_down):
    bsz, seq, d = h.shape
    xt = h.reshape(-1, d)
    n_tok = xt.shape[0]
    g_logits = (xt @ w_group).astype(jnp.float32)
    g_sel = jnp.argmax(g_logits, axis=-1)
    g_prob = jnp.take_along_axis(jax.nn.softmax(g_logits, axis=-1), g_sel[:, None], axis=-1)
    e_logits = (xt @ w_expert).astype(jnp.float32).reshape(n_tok, N_GROUPS, EXPERTS_PER_GROUP)
    e_logits = jnp.take_along_axis(e_logits, g_sel[:, None, None], axis=1)[:, 0]
    top_v, top_i = lax.top_k(e_logits, TOP_K)
    gate = (g_prob * jax.nn.softmax(top_v, axis=-1)).reshape(-1)
    eid = (g_sel[:, None] * EXPERTS_PER_GROUP + top_i).reshape(-1)
    tok = jnp.repeat(jnp.arange(n_tok, dtype=jnp.int32), TOP_K)
    n_asg = n_tok * TOP_K
    order = jnp.argsort(eid)
    eid_s, tok_s, gate_s = eid[order], tok[order], gate[order]
    counts = jnp.bincount(eid, length=N_EXPERTS)
    start = jnp.cumsum(counts) - counts
    pcounts = (counts + MOE_BLK - 1) // MOE_BLK * MOE_BLK
    pend = jnp.cumsum(pcounts)
    dest = (pend - pcounts)[eid_s] + jnp.arange(n_asg) - start[eid_s]
    rows = n_asg + N_EXPERTS * MOE_BLK
    buf_tok = jnp.full((rows,), n_tok, jnp.int32).at[dest].set(tok_s)
    buf_gate = jnp.zeros((rows,), h.dtype).at[dest].set(gate_s.astype(h.dtype))
    n_blk = rows // MOE_BLK
    blk_expert = jnp.minimum(
        jnp.searchsorted(pend, jnp.arange(n_blk) * MOE_BLK, side="right"), N_EXPERTS - 1)
    xpad = jnp.concatenate([xt, jnp.zeros((1, d), xt.dtype)], axis=0)
    xb = xpad[buf_tok].reshape(n_blk, MOE_BLK, d)

    def expert_block(args):
        xblk, e = args
        g, u = jnp.split(xblk @ w_gate_up[e], 2, axis=-1)
        return (jax.nn.silu(g) * u) @ w_down[e]

    yb = lax.map(expert_block, (xb, blk_expert)).reshape(rows, d)
    y = jnp.zeros((n_tok + 1, d), yb.dtype).at[buf_tok].add(yb * buf_gate[:, None])
    return y[:n_tok].reshape(bsz, seq, d)


def setup_inputs(seed: int = 0) -> dict:
    key = jax.random.key(seed)
    keys = iter(jax.random.split(key, 64))
    d = D_MODEL

    def normal(shape, scale):
        return jax.random.normal(next(keys), shape, jnp.float32) * scale

    def gain(n):
        return 1.0 + normal((n,), 0.05)

    def small(shape):
        return normal(shape, 0.01)

    def lru_lambda():
        a0 = jax.random.uniform(next(keys), (2, LRU_WIDTH), jnp.float32, 0.9, 0.999)
        s = a0 ** (1.0 / LRU_C)
        return jnp.log(s) - jnp.log1p(-s)

    ada_scale = 0.5 * d ** -0.5
    inp = {}
    inp["x"] = normal((BATCH, SEQ, d), 1.0)
    inp["c"] = normal((BATCH, d), 1.0)
    inp["norm0_mix"] = gain(d)
    inp["ada0_mix_w"] = normal((d, 3 * d), ada_scale)
    inp["ada0_mix_b"] = small((3 * d,))
    inp["w_in0"] = normal((d, 2 * CONV_CH + 3 * ATT_W), d ** -0.5)
    inp["conv_w"] = normal((CONV_WIDTH, CONV_CH), CONV_WIDTH ** -0.5)
    inp["conv_b"] = small((CONV_CH,))
    inp["conv_norm_g"] = gain(CONV_CH)
    inp["conv_norm_b"] = small((CONV_CH,))
    inp["w_out0"] = normal((CONV_CH + ATT_W, d), (CONV_CH + ATT_W) ** -0.5)
    inp["norm0_ffn"] = gain(d)
    inp["ada0_ffn_w"] = normal((d, 3 * d), ada_scale)
    inp["ada0_ffn_b"] = small((3 * d,))
    inp["moe0_w_group"] = normal((d, N_GROUPS), d ** -0.5)
    inp["moe0_w_expert"] = normal((d, N_EXPERTS), d ** -0.5)
    inp["moe0_w_gate_up"] = normal((N_EXPERTS, d, 2 * EXPERT_FF), d ** -0.5)
    inp["moe0_w_down"] = normal((N_EXPERTS, EXPERT_FF, d), EXPERT_FF ** -0.5)
    inp["norm1_mix"] = gain(d)
    inp["ada1_mix_w"] = normal((d, 3 * d), ada_scale)
    inp["ada1_mix_b"] = small((3 * d,))
    inp["w_in1"] = normal((d, 2 * LRU_WIDTH), d ** -0.5)
    inp["lru_conv_w"] = normal((LRU_CONV, LRU_WIDTH), LRU_CONV ** -0.5)
    inp["lru_conv_b"] = small((LRU_WIDTH,))
    inp["lru_w_a"] = normal((2, LRU_BLOCKS, LRU_BW, LRU_BW), LRU_BW ** -0.5)
    inp["lru_b_a"] = small((2, LRU_WIDTH))
    inp["lru_w_x"] = normal((2, LRU_BLOCKS, LRU_BW, LRU_BW), LRU_BW ** -0.5)
    inp["lru_b_x"] = small((2, LRU_WIDTH))
    inp["lru_lambda"] = lru_lambda()
    inp["w_out1"] = normal((LRU_WIDTH, d), LRU_WIDTH ** -0.5)
    inp["norm1_ffn"] = gain(d)
    inp["ada1_ffn_w"] = normal((d, 3 * d), ada_scale)
    inp["ada1_ffn_b"] = small((3 * d,))
    inp["moe1_w_group"] = normal((d, N_GROUPS), d ** -0.5)
    inp["moe1_w_expert"] = normal((d, N_EXPERTS), d ** -0.5)
    inp["moe1_w_gate_up"] = normal((N_EXPERTS, d, 2 * EXPERT_FF), d ** -0.5)
    inp["moe1_w_down"] = normal((N_EXPERTS, EXPERT_FF, d), EXPERT_FF ** -0.5)
    inp["norm_final"] = gain(d)
    return inp


def reference(x, c,
              norm0_mix, ada0_mix_w, ada0_mix_b, w_in0, conv_w, conv_b, conv_norm_g, conv_norm_b, w_out0,
              norm0_ffn, ada0_ffn_w, ada0_ffn_b, moe0_w_group, moe0_w_expert, moe0_w_gate_up, moe0_w_down,
              norm1_mix, ada1_mix_w, ada1_mix_b, w_in1, lru_conv_w, lru_conv_b, lru_w_a, lru_b_a,
              lru_w_x, lru_b_x, lru_lambda, w_out1,
              norm1_ffn, ada1_ffn_w, ada1_ffn_b, moe1_w_group, moe1_w_expert, moe1_w_gate_up, moe1_w_down,
              norm_final):
    mixer_fns = (
        lambda h: conv_attention_mixer(h, w_in0, conv_w, conv_b, conv_norm_g, conv_norm_b, w_out0),
        lambda h: rglru_mixer(h, w_in1, lru_conv_w, lru_conv_b, lru_w_a, lru_b_a, lru_w_x, lru_b_x,
                              lru_lambda, w_out1),
    )
    mix_mod = ((norm0_mix, ada0_mix_w, ada0_mix_b), (norm1_mix, ada1_mix_w, ada1_mix_b))
    ffn_mod = ((norm0_ffn, ada0_ffn_w, ada0_ffn_b), (norm1_ffn, ada1_ffn_w, ada1_ffn_b))
    moe_params = ((moe0_w_group, moe0_w_expert, moe0_w_gate_up, moe0_w_down),
                  (moe1_w_group, moe1_w_expert, moe1_w_gate_up, moe1_w_down))
    for layer in range(DEPTH):
        x = modulated_residual(x, c, *mix_mod[layer], mixer_fns[layer])
        x = modulated_residual(x, c, *ffn_mod[layer], lambda h: hierarchical_moe(h, *moe_params[layer]))
    return rms_norm(x, norm_final)
```

```python
import functools

import jax
import jax.numpy as jnp
from jax import lax
from jax.experimental import pallas as pl
from jax.experimental.pallas import tpu as pltpu

F32 = jnp.float32
BF16 = jnp.bfloat16

D_MODEL = 2048
BATCH = 4
SEQ = 4096
N_TOK = BATCH * SEQ
EPS = 1e-6
NEG_INF = -1e30

CONV_CH = 1024
CONV_GROUPS = 8
CONV_WIDTH = 31
CONV_HALO = 16

ATT_HEADS = 8
HEAD_DIM = 128
ATT_W = ATT_HEADS * HEAD_DIM
ATT_BRANCHES = ((128, 1), (512, 4), (2048, 16))
ATT_RADIUS = 64
ATT_QB = 128
ATT_KB = ATT_QB + 2 * ATT_RADIUS
ALIBI_MAX = 8.0

LRU_WIDTH = 2688
LRU_BLOCKS = 16
LRU_BW = LRU_WIDTH // LRU_BLOCKS
LRU_PW = 256
LRU_WP = LRU_BLOCKS * LRU_PW
LRU_CONV = 4
LRU_C = 8.0
LRU_CHUNKS = 8
LRU_CL = SEQ // LRU_CHUNKS
LRU_PITCH = LRU_CL + 8
LRU_XPAD = 8

N_GROUPS = 4
EXPERTS_PER_GROUP = 8
N_EXPERTS = 32
TOP_K = 2
EXPERT_FF = 1024
MOE_TB = 256
MOE_ROWS = N_TOK * TOP_K + N_EXPERTS * MOE_TB
MOE_NBLK = MOE_ROWS // MOE_TB
ROUTER_LANES = 128

VMEM_LIMIT = 48 * 1024 * 1024


def _cparams(sem):
    return pltpu.CompilerParams(dimension_semantics=sem, vmem_limit_bytes=VMEM_LIMIT)


def _sigmoid(x):
    return 0.5 * jnp.tanh(0.5 * x) + 0.5


def _ada_kernel(ct_ref, w_ref, b_ref, o_ref):
    ct = ct_ref[...]
    st = ct * _sigmoid(ct)
    w = w_ref[...]
    rows = [jnp.sum(w * st[:, b:b + 1], axis=0, keepdims=True) for b in range(BATCH)]
    o_ref[...] = jnp.concatenate(rows, axis=0) + b_ref[...]


def ada_modulation(c, w, b):
    tn = 512
    n = w.shape[1]
    mod = pl.pallas_call(
        _ada_kernel,
        grid=(n // tn,),
        in_specs=[
            pl.BlockSpec((D_MODEL, BATCH), lambda j: (0, 0)),
            pl.BlockSpec((D_MODEL, tn), lambda j: (0, j)),
            pl.BlockSpec((1, tn), lambda j: (0, j)),
        ],
        out_specs=pl.BlockSpec((BATCH, tn), lambda j: (0, j)),
        out_shape=jax.ShapeDtypeStruct((BATCH, n), F32),
        compiler_params=_cparams(("parallel",)),
        name="ada_modulation",
    )(c.T, w, b.reshape(1, n))
    shift, scale, gate = jnp.split(mod, 3, axis=-1)
    return shift[:, None, :], scale[:, None, :], gate[:, None, :]


def _modulated_norm(x, g, scale, shift):
    y = x * lax.rsqrt(jnp.mean(x * x, axis=-1, keepdims=True) + EPS)
    return (y * g) * (1.0 + scale) + shift


def _norm_mm_kernel(x_ref, g_ref, sc_ref, sh_ref, w_ref, o_ref, h_scr, *, head_major):
    @pl.when(pl.program_id(2) == 0)
    def _():
        h_scr[...] = _modulated_norm(x_ref[0], g_ref[...], sc_ref[0], sh_ref[0]).astype(BF16)

    res = jnp.dot(h_scr[...], w_ref[...], preferred_element_type=F32)
    if head_major:
        for hh in range(ATT_HEADS):
            o_ref[0, 0, hh] = res[:, hh * HEAD_DIM:(hh + 1) * HEAD_DIM].astype(o_ref.dtype)
    else:
        o_ref[0] = res.astype(o_ref.dtype)


def norm_project(x, g, scale, shift, w, out_dtype, head_major=False, tm=512, tn=1024):
    n = w.shape[1]
    grid = (BATCH, SEQ // tm, n // tn)
    if head_major:
        assert tn == ATT_W
        out_shape = jax.ShapeDtypeStruct((n // tn, BATCH, ATT_HEADS, SEQ, HEAD_DIM), out_dtype)
        out_spec = pl.BlockSpec((1, 1, ATT_HEADS, tm, HEAD_DIM), lambda b, i, j: (j, b, 0, i, 0))
    else:
        out_shape = jax.ShapeDtypeStruct((BATCH, SEQ, n), out_dtype)
        out_spec = pl.BlockSpec((1, tm, tn), lambda b, i, j: (b, i, j))
    return pl.pallas_call(
        functools.partial(_norm_mm_kernel, head_major=head_major),
        grid=grid,
        in_specs=[
            pl.BlockSpec((1, tm, D_MODEL), lambda b, i, j: (b, i, 0)),
            pl.BlockSpec((1, D_MODEL), lambda b, i, j: (0, 0)),
            pl.BlockSpec((1, 1, D_MODEL), lambda b, i, j: (b, 0, 0)),
            pl.BlockSpec((1, 1, D_MODEL), lambda b, i, j: (b, 0, 0)),
            pl.BlockSpec((D_MODEL, tn), lambda b, i, j: (0, j)),
        ],
        out_specs=out_spec,
        out_shape=out_shape,
        scratch_shapes=[pltpu.VMEM((tm, D_MODEL), BF16)],
        compiler_params=_cparams(("parallel", "parallel", "arbitrary")),
        name="norm_project",
    )(x, g.reshape(1, D_MODEL), scale, shift, w)


def _mm_residual_kernel(*refs, n_in):
    a_refs, w_refs = refs[:n_in], refs[n_in:2 * n_in]
    x_ref, gate_ref, o_ref = refs[2 * n_in:]
    acc = jnp.dot(a_refs[0][0], w_refs[0][...], preferred_element_type=F32)
    for a_ref, w_ref in zip(a_refs[1:], w_refs[1:]):
        acc = acc + jnp.dot(a_ref[0], w_ref[...], preferred_element_type=F32)
    o_ref[0] = x_ref[0] + gate_ref[0] * acc


def project_residual(a_list, w_list, x, gate, tm=512, tn=1024):
    n_in = len(a_list)
    grid = (BATCH, SEQ // tm, D_MODEL // tn)
    in_specs = [pl.BlockSpec((1, tm, a.shape[-1]), lambda b, i, j: (b, i, 0)) for a in a_list]
    in_specs += [pl.BlockSpec((w.shape[0], tn), lambda b, i, j: (0, j)) for w in w_list]
    in_specs += [
        pl.BlockSpec((1, tm, tn), lambda b, i, j: (b, i, j)),
        pl.BlockSpec((1, 1, tn), lambda b, i, j: (b, 0, j)),
    ]
    return pl.pallas_call(
        functools.partial(_mm_residual_kernel, n_in=n_in),
        grid=grid,
        in_specs=in_specs,
        out_specs=pl.BlockSpec((1, tm, tn), lambda b, i, j: (b, i, j)),
        out_shape=jax.ShapeDtypeStruct((BATCH, SEQ, D_MODEL), F32),
        compiler_params=_cparams(("parallel", "parallel", "parallel")),
        name="project_residual",
    )(*a_list, *w_list, x, gate)


def _conv_kernel(v_ref, g_ref, cw_ref, cb_ref, ng_ref, nb_ref, o_ref, pad_scr, *, ts):
    zeros = jnp.zeros((CONV_HALO, 128), F32)
    pad_scr[0:CONV_HALO, :] = zeros
    pad_scr[SEQ + CONV_HALO:SEQ + 2 * CONV_HALO, :] = zeros
    for t0 in range(0, SEQ, ts):
        pad_scr[CONV_HALO + t0:CONV_HALO + t0 + ts, :] = v_ref[0, t0:t0 + ts, :] * _sigmoid(g_ref[0, t0:t0 + ts, :])
    first = CONV_HALO - CONV_WIDTH // 2
    for t0 in range(0, SEQ, ts):
        acc = cw_ref[0:1, :] * pad_scr[first + t0:first + t0 + ts, :] + cb_ref[...]
        for k in range(1, CONV_WIDTH):
            acc = acc + cw_ref[k:k + 1, :] * pad_scr[first + t0 + k:first + t0 + k + ts, :]
        mu = jnp.mean(acc, axis=-1, keepdims=True)
        cen = acc - mu
        var = jnp.mean(cen * cen, axis=-1, keepdims=True)
        y = cen * lax.rsqrt(var + EPS) * ng_ref[...] + nb_ref[...]
        o_ref[0, t0:t0 + ts, :] = (y * _sigmoid(y)).astype(o_ref.dtype)


def conv_module(p, conv_w, conv_b, norm_g, norm_b, ts=128):
    ng = CONV_GROUPS
    vec = lambda a: a.reshape(1, CONV_CH)
    vspec = pl.BlockSpec((1, 128), lambda b, c: (0, c))
    return pl.pallas_call(
        functools.partial(_conv_kernel, ts=ts),
        grid=(BATCH, ng),
        in_specs=[
            pl.BlockSpec((1, SEQ, 128), lambda b, c: (b, 0, c)),
            pl.BlockSpec((1, SEQ, 128), lambda b, c: (b, 0, c + ng)),
            pl.BlockSpec((CONV_WIDTH, 128), lambda b, c: (0, c)),
            vspec, vspec, vspec,
        ],
        out_specs=pl.BlockSpec((1, SEQ, 128), lambda b, c: (b, 0, c)),
        out_shape=jax.ShapeDtypeStruct((BATCH, SEQ, CONV_CH), BF16),
        scratch_shapes=[pltpu.VMEM((SEQ + 2 * CONV_HALO, 128), F32)],
        compiler_params=_cparams(("parallel", "parallel")),
        name="conv_module",
    )(p, p, conv_w, vec(conv_b), vec(norm_g), vec(norm_b))


def _attn_kernel(q_ref, k_ref, v_ref, bias_ref, o_ref, st_ref, kpad, vpad, *, sub, nres):
    head = pl.program_id(2)
    lane = lax.broadcasted_iota(jnp.int32, (ATT_QB, 128), 1)
    col = lax.broadcasted_iota(jnp.int32, (ATT_QB, ATT_KB), 1)
    zeros = jnp.zeros((ATT_RADIUS, HEAD_DIM), BF16)
    scale = HEAD_DIM ** -0.5

    @pl.when(head == 0)
    def _():
        st_ref[...] = jnp.zeros(st_ref.shape, F32)

    for r in range(nres):
        lanes = slice(r * HEAD_DIM, (r + 1) * HEAD_DIM)
        for pad, src in ((kpad, k_ref), (vpad, v_ref)):
            pad[0:ATT_RADIUS, :] = zeros
            pad[sub + ATT_RADIUS:sub + 2 * ATT_RADIUS, :] = zeros
            pad[ATT_RADIUS:sub + ATT_RADIUS, :] = src[0, 0, :, lanes]

        def block(i, carry):
            q0 = pl.multiple_of(i * ATT_QB, ATT_QB)
            qb = q_ref[0, 0, pl.ds(q0, ATT_QB), lanes]
            kb = kpad[pl.ds(q0, ATT_KB), :]
            vb = vpad[pl.ds(q0, ATT_KB), :]
            s = lax.dot_general(qb, kb, (((1,), (1,)), ((), ())), preferred_element_type=F32)
            s = s * scale + bias_ref[0]
            key = q0 - ATT_RADIUS + col
            s = jnp.where((key >= 0) & (key < sub), s, NEG_INF)
            m = jnp.max(s, axis=-1, keepdims=True)
            e = jnp.exp(s - m)
            den = jnp.sum(e, axis=-1, keepdims=True)
            o = jnp.dot(e.astype(BF16), vb, preferred_element_type=F32) / den
            o_ref[0, 0, pl.ds(q0, ATT_QB), lanes] = o.astype(o_ref.dtype)
            lse = m + jnp.log(den)
            st_ref[0, pl.ds(q0, ATT_QB), lanes] = jnp.where(lane == head, lse, st_ref[0, pl.ds(q0, ATT_QB), lanes])
            return carry

        lax.fori_loop(0, sub // ATT_QB, block, 0)


def _alibi_bias(dilation):
    row = jnp.arange(ATT_QB)[:, None]
    colm = jnp.arange(ATT_KB)[None, :]
    rel = jnp.abs(colm - ATT_RADIUS - row)
    slopes = 2.0 ** (-ALIBI_MAX * jnp.arange(1, ATT_HEADS + 1, dtype=F32) / ATT_HEADS)
    bias = -slopes[:, None, None] * (rel * dilation).astype(F32)[None]
    return jnp.where((rel <= ATT_RADIUS)[None], bias, NEG_INF)


def dilated_attention_branch(qkv, dilation):
    sub = SEQ // dilation
    nres = min(dilation, 4)
    qkv_v = qkv.reshape(3, BATCH, ATT_HEADS, sub, dilation * HEAD_DIM)
    wid = nres * HEAD_DIM
    qspec = lambda which: pl.BlockSpec((None, 1, 1, sub, wid), lambda b, r, h, which=which: (which, b, h, 0, r))
    o, st = pl.pallas_call(
        functools.partial(_attn_kernel, sub=sub, nres=nres),
        grid=(BATCH, dilation // nres, ATT_HEADS),
        in_specs=[qspec(0), qspec(1), qspec(2),
                  pl.BlockSpec((1, ATT_QB, ATT_KB), lambda b, r, h: (h, 0, 0))],
        out_specs=[
            pl.BlockSpec((1, 1, sub, wid), lambda b, r, h: (b, h, 0, r)),
            pl.BlockSpec((1, sub, wid), lambda b, r, h: (b, 0, r)),
        ],
        out_shape=[
            jax.ShapeDtypeStruct((BATCH, ATT_HEADS, sub, dilation * HEAD_DIM), F32),
            jax.ShapeDtypeStruct((BATCH, sub, dilation * 128), F32),
        ],
        scratch_shapes=[pltpu.VMEM((sub + 2 * ATT_RADIUS, HEAD_DIM), BF16),
                        pltpu.VMEM((sub + 2 * ATT_RADIUS, HEAD_DIM), BF16)],
        compiler_params=_cparams(("parallel", "parallel", "arbitrary")),
        name=f"dilated_attention_d{dilation}",
    )(qkv_v, qkv_v, qkv_v, _alibi_bias(dilation))
    return o.reshape(BATCH, ATT_HEADS, SEQ, HEAD_DIM), st.reshape(BATCH, SEQ, 128)


def _merge_kernel(o1_ref, o2_ref, o3_ref, s1_ref, s2_ref, s3_ref, out_ref):
    l1, l2, l3 = s1_ref[0], s2_ref[0], s3_ref[0]
    m = jnp.maximum(jnp.maximum(l1, l2), l3)
    e1, e2, e3 = jnp.exp(l1 - m), jnp.exp(l2 - m), jnp.exp(l3 - m)
    inv = 1.0 / (e1 + e2 + e3)
    w1, w2, w3 = e1 * inv, e2 * inv, e3 * inv
    for h in range(ATT_HEADS):
        acc = (w1[:, h:h + 1] * o1_ref[0, h] + w2[:, h:h + 1] * o2_ref[0, h]
               + w3[:, h:h + 1] * o3_ref[0, h])
        out_ref[0, :, h * HEAD_DIM:(h + 1) * HEAD_DIM] = acc.astype(out_ref.dtype)


def merge_branches(outs, stats, tm=512):
    ospec = pl.BlockSpec((1, ATT_HEADS, tm, HEAD_DIM), lambda b, i: (b, 0, i, 0))
    sspec = pl.BlockSpec((1, tm, 128), lambda b, i: (b, i, 0))
    return pl.pallas_call(
        _merge_kernel,
        grid=(BATCH, SEQ // tm),
        in_specs=[ospec] * 3 + [sspec] * 3,
        out_specs=pl.BlockSpec((1, tm, ATT_W), lambda b, i: (b, i, 0)),
        out_shape=jax.ShapeDtypeStruct((BATCH, SEQ, ATT_W), BF16),
        compiler_params=_cparams(("parallel", "parallel")),
        name="merge_branches",
    )(*outs, *stats)


def _softplus(x):
    return jnp.maximum(x, 0.0) + jnp.log1p(jnp.exp(-jnp.abs(x)))


def _gelu_tanh(x):
    return 0.5 * x * (1.0 + jnp.tanh(0.7978845608028654 * (x + 0.044715 * (x * x * x))))


def _lru_kernel(x_ref, gate_ref, cw_ref, cb_ref, w_ref, b_ref, lam_ref, o_ref, xpad, xc, a_scr, u_scr, yacc):
    ntile = LRU_PW // 128
    zeros = jnp.zeros((LRU_XPAD, LRU_PW), F32)
    xpad[0:LRU_XPAD, :] = zeros
    xpad[SEQ + LRU_XPAD:SEQ + 2 * LRU_XPAD, :] = zeros
    for j in range(LRU_CHUNKS):
        xpad[LRU_XPAD + j * LRU_CL:LRU_XPAD + (j + 1) * LRU_CL, :] = x_ref[0, j * LRU_CL:(j + 1) * LRU_CL, :]
    first = LRU_XPAD - LRU_CONV // 2
    for j in range(LRU_CHUNKS):
        base = first + j * LRU_CL
        acc = cw_ref[0:1, :] * xpad[base:base + LRU_CL, :] + cb_ref[...]
        for k in range(1, LRU_CONV):
            acc = acc + cw_ref[k:k + 1, :] * xpad[base + k:base + k + LRU_CL, :]
        xc[j * LRU_CL:(j + 1) * LRU_CL, :] = acc

    for d in range(2):
        reverse = d == 1
        decay = -LRU_C * _softplus(-lam_ref[d:d + 1, :])
        for j in range(LRU_CHUNKS):
            xj = xc[j * LRU_CL:(j + 1) * LRU_CL, :]
            pre = jnp.dot(xj.astype(BF16), w_ref[d, 0], preferred_element_type=F32) + b_ref[d, 0]
            r = _sigmoid(pre[:, :LRU_PW])
            i = _sigmoid(pre[:, LRU_PW:])
            log_a = r * decay
            a = jnp.exp(log_a)
            one_minus_a2 = -jnp.tanh(log_a) * (a * a + 1.0)
            u = jnp.sqrt(one_minus_a2) * (i * xj)
            for lt in range(ntile):
                a_scr[lt, j * LRU_PITCH:j * LRU_PITCH + LRU_CL, :] = a[:, lt * 128:(lt + 1) * 128]
                u_scr[lt, j * LRU_PITCH:j * LRU_PITCH + LRU_CL, :] = u[:, lt * 128:(lt + 1) * 128]

        def step(ii, carry):
            row = (LRU_CL - 1 - ii) if reverse else ii
            out = []
            for lt in range(ntile):
                h, prod = carry[lt]
                a = a_scr[lt, pl.ds(row, LRU_CHUNKS, stride=LRU_PITCH), :]
                u = u_scr[lt, pl.ds(row, LRU_CHUNKS, stride=LRU_PITCH), :]
                h = a * h + u
                prod = a * prod
                u_scr[lt, pl.ds(row, LRU_CHUNKS, stride=LRU_PITCH), :] = h
                a_scr[lt, pl.ds(row, LRU_CHUNKS, stride=LRU_PITCH), :] = prod
                out.append((h, prod))
            return tuple(out)

        init = tuple((jnp.zeros((LRU_CHUNKS, 128), F32), jnp.ones((LRU_CHUNKS, 128), F32)) for _ in range(ntile))
        lax.fori_loop(0, LRU_CL, step, init)

        last = 0 if reverse else LRU_CL - 1
        order = range(LRU_CHUNKS - 1, -1, -1) if reverse else range(LRU_CHUNKS)
        for lt in range(ntile):
            h_end = u_scr[lt, pl.ds(last, LRU_CHUNKS, stride=LRU_PITCH), :]
            p_end = a_scr[lt, pl.ds(last, LRU_CHUNKS, stride=LRU_PITCH), :]
            carry = jnp.zeros((1, 128), F32)
            for j in order:
                rows = slice(j * LRU_PITCH, j * LRU_PITCH + LRU_CL)
                y = u_scr[lt, rows, :] + a_scr[lt, rows, :] * carry
                dst = (slice(j * LRU_CL, (j + 1) * LRU_CL), slice(lt * 128, (lt + 1) * 128))
                if reverse:
                    yacc[dst] = yacc[dst] + y
                else:
                    yacc[dst] = y
                carry = h_end[j:j + 1, :] + p_end[j:j + 1, :] * carry

    for j in range(LRU_CHUNKS):
        rows = slice(j * LRU_CL, (j + 1) * LRU_CL)
        o_ref[0, rows, :] = (_gelu_tanh(gate_ref[0, rows, :]) * yacc[rows, :]).astype(o_ref.dtype)


def rglru_block(p, conv_w, conv_b, w_gates, b_gates, lam):
    nb = LRU_BLOCKS
    pw = LRU_PW
    return pl.pallas_call(
        _lru_kernel,
        grid=(BATCH, nb),
        in_specs=[
            pl.BlockSpec((1, SEQ, pw), lambda b, n: (b, 0, n + nb)),
            pl.BlockSpec((1, SEQ, pw), lambda b, n: (b, 0, n)),
            pl.BlockSpec((LRU_CONV, pw), lambda b, n: (0, n)),
            pl.BlockSpec((1, pw), lambda b, n: (0, n)),
            pl.BlockSpec((2, 1, pw, 2 * pw), lambda b, n: (0, n, 0, 0)),
            pl.BlockSpec((2, 1, 1, 2 * pw), lambda b, n: (0, n, 0, 0)),
            pl.BlockSpec((2, pw), lambda b, n: (0, n)),
        ],
        out_specs=pl.BlockSpec((1, SEQ, pw), lambda b, n: (b, 0, n)),
        out_shape=jax.ShapeDtypeStruct((BATCH, SEQ, LRU_WP), BF16),
        scratch_shapes=[
            pltpu.VMEM((SEQ + 2 * LRU_XPAD, pw), F32),
            pltpu.VMEM((SEQ, pw), F32),
            pltpu.VMEM((pw // 128, LRU_CHUNKS * LRU_PITCH, 128), F32),
            pltpu.VMEM((pw // 128, LRU_CHUNKS * LRU_PITCH, 128), F32),
            pltpu.VMEM((SEQ, pw), F32),
        ],
        compiler_params=_cparams(("parallel", "parallel")),
        name="rglru_block",
    )(p, p, conv_w, conv_b, w_gates, b_gates, lam)


def _pad_blocks(a):
    lead = a.shape[:-1]
    a = a.reshape(*lead, LRU_BLOCKS, LRU_BW)
    a = jnp.pad(a, [(0, 0)] * len(lead) + [(0, 0), (0, LRU_PW - LRU_BW)])
    return a.reshape(*lead, LRU_WP)


def _router_kernel(x_ref, g_ref, sc_ref, sh_ref, wr_ref, h_ref, eid_ref, gt_ref):
    h = _modulated_norm(x_ref[0], g_ref[...], sc_ref[0], sh_ref[0])
    hb = h.astype(BF16)
    h_ref[0] = hb
    h_lo = (h - hb.astype(F32)).astype(BF16)
    w = wr_ref[...]
    w_hi = w.astype(BF16)
    w_lo = (w - w_hi.astype(F32)).astype(BF16)
    lg = (jnp.dot(hb, w_hi, preferred_element_type=F32) + jnp.dot(h_lo, w_hi, preferred_element_type=F32)
          + jnp.dot(hb, w_lo, preferred_element_type=F32))
    lt = lg.T
    tm = lt.shape[1]
    row = lax.broadcasted_iota(jnp.int32, (8, tm), 0)
    big = jnp.int32(99)
    gl = jnp.where(row < N_GROUPS, lt[0:8], -jnp.inf)
    g_max = jnp.max(gl, axis=0, keepdims=True)
    g_sel = jnp.min(jnp.where(gl == g_max, row, big), axis=0, keepdims=True)
    g_prob = 1.0 / jnp.sum(jnp.exp(gl - g_max), axis=0, keepdims=True)
    el = jnp.zeros((8, tm), F32)
    for g in range(N_GROUPS):
        el = jnp.where(g_sel == g, lt[8 + 8 * g:16 + 8 * g], el)
    v1 = jnp.max(el, axis=0, keepdims=True)
    i1 = jnp.min(jnp.where(el == v1, row, big), axis=0, keepdims=True)
    el2 = jnp.where(row == i1, -jnp.inf, el)
    v2 = jnp.max(el2, axis=0, keepdims=True)
    i2 = jnp.min(jnp.where(el2 == v2, row, big), axis=0, keepdims=True)
    e2 = jnp.exp(v2 - v1)
    p1 = 1.0 / (1.0 + e2)
    p2 = e2 * p1
    eid_ref[0] = jnp.where(row == 0, g_sel * EXPERTS_PER_GROUP + i1,
                           jnp.where(row == 1, g_sel * EXPERTS_PER_GROUP + i2, 0))
    gt_ref[0] = jnp.where(row == 0, g_prob * p1, jnp.where(row == 1, g_prob * p2, 0.0))


def route(x, g, scale, shift, w_router, tm=512):
    return pl.pallas_call(
        _router_kernel,
        grid=(BATCH, SEQ // tm),
        in_specs=[
            pl.BlockSpec((1, tm, D_MODEL), lambda b, i: (b, i, 0)),
            pl.BlockSpec((1, D_MODEL), lambda b, i: (0, 0)),
            pl.BlockSpec((1, 1, D_MODEL), lambda b, i: (b, 0, 0)),
            pl.BlockSpec((1, 1, D_MODEL), lambda b, i: (b, 0, 0)),
            pl.BlockSpec((D_MODEL, ROUTER_LANES), lambda b, i: (0, 0)),
        ],
        out_specs=[
            pl.BlockSpec((1, tm, D_MODEL), lambda b, i: (b, i, 0)),
            pl.BlockSpec((1, 8, tm), lambda b, i: (b, 0, i)),
            pl.BlockSpec((1, 8, tm), lambda b, i: (b, 0, i)),
        ],
        out_shape=[
            jax.ShapeDtypeStruct((BATCH, SEQ, D_MODEL), BF16),
            jax.ShapeDtypeStruct((BATCH, 8, SEQ), jnp.int32),
            jax.ShapeDtypeStruct((BATCH, 8, SEQ), F32),
        ],
        compiler_params=_cparams(("parallel", "parallel")),
        name="route",
    )(x, g.reshape(1, D_MODEL), scale, shift, w_router)


def _expert_kernel(be_ref, nu_ref, x_ref, wgu_ref, wd_ref, o_ref):
    @pl.when(pl.program_id(0) < nu_ref[0])
    def _():
        gu = jnp.dot(x_ref[...], wgu_ref[0], preferred_element_type=F32)
        g = gu[:, :EXPERT_FF]
        u = gu[:, EXPERT_FF:]
        act = (g * _sigmoid(g) * u).astype(BF16)
        o_ref[...] = jnp.dot(act, wd_ref[0], preferred_element_type=F32)


def expert_blocks(xs, blk_expert, n_used, w_gate_up, w_down):
    row_map = lambda i, be, nu: (jnp.minimum(i, nu[0] - 1), 0)
    return pl.pallas_call(
        _expert_kernel,
        grid_spec=pltpu.PrefetchScalarGridSpec(
            num_scalar_prefetch=2,
            grid=(MOE_NBLK,),
            in_specs=[
                pl.BlockSpec((MOE_TB, D_MODEL), row_map),
                pl.BlockSpec((1, D_MODEL, 2 * EXPERT_FF), lambda i, be, nu: (be[i], 0, 0)),
                pl.BlockSpec((1, EXPERT_FF, D_MODEL), lambda i, be, nu: (be[i], 0, 0)),
            ],
            out_specs=pl.BlockSpec((MOE_TB, D_MODEL), row_map),
        ),
        out_shape=jax.ShapeDtypeStruct((MOE_ROWS, D_MODEL), F32),
        compiler_params=_cparams(("arbitrary",)),
        name="expert_blocks",
    )(blk_expert, n_used, xs, w_gate_up, w_down)


def _router_weights(w_group, w_expert):
    w = jnp.zeros((D_MODEL, ROUTER_LANES), F32)
    w = w.at[:, 0:N_GROUPS].set(w_group)
    return w.at[:, 8:8 + N_EXPERTS].set(w_expert)


def hierarchical_moe(x, c, norm_g, ada_w, ada_b, w_group, w_expert, w_gate_up, w_down):
    shift, scale, gate_ada = ada_modulation(c, ada_w, ada_b)
    h, eid, gts = route(x, norm_g, scale, shift, _router_weights(w_group, w_expert))
    h = h.reshape(N_TOK, D_MODEL)
    e_flat = jnp.concatenate([eid[:, 0, :].reshape(N_TOK), eid[:, 1, :].reshape(N_TOK)])
    g_flat = jnp.concatenate([gts[:, 0, :].reshape(N_TOK), gts[:, 1, :].reshape(N_TOK)])
    tok = jnp.tile(jnp.arange(N_TOK, dtype=jnp.int32), 2)
    onehot = (e_flat[:, None] == jnp.arange(N_EXPERTS, dtype=jnp.int32)[None, :]).astype(jnp.int32)
    csum = jnp.cumsum(onehot, axis=0)
    rank = jnp.sum(csum * onehot, axis=1) - 1
    counts = csum[-1]
    pcounts = (counts + MOE_TB - 1) // MOE_TB * MOE_TB
    pend = jnp.cumsum(pcounts)
    dest = (pend - pcounts)[e_flat] + rank
    n_used = (pend[-1] // MOE_TB).astype(jnp.int32)
    blk = jnp.minimum(jnp.arange(MOE_NBLK, dtype=jnp.int32), n_used - 1) * MOE_TB
    blk_expert = jnp.minimum(jnp.searchsorted(pend, blk, side="right"), N_EXPERTS - 1).astype(jnp.int32)
    buf_tok = jnp.zeros((MOE_ROWS,), jnp.int32).at[dest].set(tok)
    xs = h[buf_tok]
    yb = expert_blocks(xs, blk_expert, n_used.reshape(1), w_gate_up.astype(BF16), w_down.astype(BF16))
    y = g_flat[:N_TOK, None] * yb[dest[:N_TOK]] + g_flat[N_TOK:, None] * yb[dest[N_TOK:]]
    return x + gate_ada * y.reshape(BATCH, SEQ, D_MODEL)


def _final_norm_kernel(x_ref, g_ref, o_ref):
    x = x_ref[0]
    o_ref[0] = x * lax.rsqrt(jnp.mean(x * x, axis=-1, keepdims=True) + EPS) * g_ref[...]


def final_norm(x, g, tm=512):
    return pl.pallas_call(
        _final_norm_kernel,
        grid=(BATCH, SEQ // tm),
        in_specs=[pl.BlockSpec((1, tm, D_MODEL), lambda b, i: (b, i, 0)),
                  pl.BlockSpec((1, D_MODEL), lambda b, i: (0, 0))],
        out_specs=pl.BlockSpec((1, tm, D_MODEL), lambda b, i: (b, i, 0)),
        out_shape=jax.ShapeDtypeStruct((BATCH, SEQ, D_MODEL), F32),
        compiler_params=_cparams(("parallel", "parallel")),
        name="final_norm",
    )(x, g.reshape(1, D_MODEL))


def kernel(x, c, norm0_mix, ada0_mix_w, ada0_mix_b, w_in0, conv_w, conv_b, conv_norm_g, conv_norm_b, w_out0, norm0_ffn, ada0_ffn_w, ada0_ffn_b, moe0_w_group, moe0_w_expert, moe0_w_gate_up, moe0_w_down, norm1_mix, ada1_mix_w, ada1_mix_b, w_in1, lru_conv_w, lru_conv_b, lru_w_a, lru_b_a, lru_w_x, lru_b_x, lru_lambda, w_out1, norm1_ffn, ada1_ffn_w, ada1_ffn_b, moe1_w_group, moe1_w_expert, moe1_w_gate_up, moe1_w_down, norm_final):
    shift, scale, gate = ada_modulation(c, ada0_mix_w, ada0_mix_b)
    w_in0 = w_in0.astype(BF16)
    p_conv = norm_project(x, norm0_mix, scale, shift, w_in0[:, :2 * CONV_CH], F32)
    qkv = norm_project(x, norm0_mix, scale, shift, w_in0[:, 2 * CONV_CH:], BF16, head_major=True)
    a = conv_module(p_conv, conv_w, conv_b, conv_norm_g, conv_norm_b)
    outs, stats = zip(*(dilated_attention_branch(qkv, dilation) for _, dilation in ATT_BRANCHES))
    o = merge_branches(outs, stats)
    w_out0 = w_out0.astype(BF16)
    x = project_residual([a, o], [w_out0[:CONV_CH], w_out0[CONV_CH:]], x, gate)
    x = hierarchical_moe(x, c, norm0_ffn, ada0_ffn_w, ada0_ffn_b, moe0_w_group, moe0_w_expert,
                         moe0_w_gate_up, moe0_w_down)

    shift, scale, gate = ada_modulation(c, ada1_mix_w, ada1_mix_b)
    w_in1p = jnp.concatenate([_pad_blocks(w_in1[:, :LRU_WIDTH]), _pad_blocks(w_in1[:, LRU_WIDTH:])], axis=1)
    p_lru = norm_project(x, norm1_mix, scale, shift, w_in1p.astype(BF16), F32)
    pad_sq = ((0, 0), (0, 0), (0, LRU_PW - LRU_BW), (0, LRU_PW - LRU_BW))
    w_gates = jnp.concatenate([jnp.pad(lru_w_a, pad_sq), jnp.pad(lru_w_x, pad_sq)], axis=-1).astype(BF16)
    b_gates = jnp.concatenate([_pad_blocks(lru_b_a).reshape(2, LRU_BLOCKS, 1, LRU_PW),
                               _pad_blocks(lru_b_x).reshape(2, LRU_BLOCKS, 1, LRU_PW)], axis=-1)
    y = rglru_block(p_lru, _pad_blocks(lru_conv_w), _pad_blocks(lru_conv_b).reshape(1, LRU_WP), w_gates, b_gates,
                    _pad_blocks(lru_lambda))
    w_out1p = jnp.pad(w_out1.reshape(LRU_BLOCKS, LRU_BW, D_MODEL), ((0, 0), (0, LRU_PW - LRU_BW), (0, 0)))
    x = project_residual([y], [w_out1p.reshape(LRU_WP, D_MODEL).astype(BF16)], x, gate)
    x = hierarchical_moe(x, c, norm1_ffn, ada1_ffn_w, ada1_ffn_b, moe1_w_group, moe1_w_expert,
                         moe1_w_gate_up, moe1_w_down)
    return final_norm(x, norm_final)
```

```python
import functools

import jax
import jax.numpy as jnp
from jax import lax
from jax.experimental import pallas as pl
from jax.experimental.pallas import tpu as pltpu

F32 = jnp.float32
BF16 = jnp.bfloat16

D_MODEL = 2048
BATCH = 4
SEQ = 4096
N_TOK = BATCH * SEQ
EPS = 1e-6
NEG_INF = -1e30

CONV_CH = 1024
CONV_GROUPS = 8
CONV_WIDTH = 31
CONV_HALO = 16

ATT_HEADS = 8
HEAD_DIM = 128
ATT_W = ATT_HEADS * HEAD_DIM
ATT_BRANCHES = ((128, 1), (512, 4), (2048, 16))
ATT_RADIUS = 64
ATT_QB = 128
ATT_KB = ATT_QB + 2 * ATT_RADIUS
ALIBI_MAX = 8.0

LRU_WIDTH = 2688
LRU_BLOCKS = 16
LRU_BW = LRU_WIDTH // LRU_BLOCKS
LRU_PW = 192
LRU_WP = LRU_BLOCKS * LRU_PW
LRU_GW = 2 * LRU_PW
LRU_CONV = 4
LRU_C = 8.0
LRU_CHUNKS = 8
LRU_CL = SEQ // LRU_CHUNKS
LRU_TILE = 512
LOG2E = 1.4426950408889634

N_GROUPS = 4
EXPERTS_PER_GROUP = 8
N_EXPERTS = 32
TOP_K = 2
EXPERT_FF = 1024
MOE_TB = 256
MOE_ROWS = N_TOK * TOP_K + N_EXPERTS * MOE_TB
MOE_NBLK = MOE_ROWS // MOE_TB
MOE_CAST_ROWS = 256
ROUTER_LANES = 128

VMEM_LIMIT = 48 * 1024 * 1024
BIG_VMEM_LIMIT = 56 * 1024 * 1024


def _cparams(sem):
    return pltpu.CompilerParams(dimension_semantics=sem, vmem_limit_bytes=VMEM_LIMIT)


def _sigmoid(x):
    return 0.5 * jnp.tanh(0.5 * x) + 0.5


def _ada_kernel(ct_ref, w_ref, b_ref, o_ref):
    ct = ct_ref[...]
    st = ct * _sigmoid(ct)
    w = w_ref[...]
    rows = [jnp.sum(w * st[:, b:b + 1], axis=0, keepdims=True) for b in range(BATCH)]
    o_ref[...] = jnp.concatenate(rows, axis=0) + b_ref[...]


def ada_modulation(c, w, b):
    tn = 512
    n = w.shape[1]
    mod = pl.pallas_call(
        _ada_kernel,
        grid=(n // tn,),
        in_specs=[
            pl.BlockSpec((D_MODEL, BATCH), lambda j: (0, 0)),
            pl.BlockSpec((D_MODEL, tn), lambda j: (0, j)),
            pl.BlockSpec((1, tn), lambda j: (0, j)),
        ],
        out_specs=pl.BlockSpec((BATCH, tn), lambda j: (0, j)),
        out_shape=jax.ShapeDtypeStruct((BATCH, n), F32),
        compiler_params=_cparams(("parallel",)),
        name="ada_modulation",
    )(c.T, w, b.reshape(1, n))
    shift, scale, gate = jnp.split(mod, 3, axis=-1)
    return shift[:, None, :], scale[:, None, :], gate[:, None, :]


def _modulated_norm(x, g, scale, shift):
    y = x * lax.rsqrt(jnp.mean(x * x, axis=-1, keepdims=True) + EPS)
    return (y * g) * (1.0 + scale) + shift


def _norm_mm_kernel(x_ref, g_ref, sc_ref, sh_ref, w_ref, o_ref, h_scr, *, head_major):
    @pl.when(pl.program_id(2) == 0)
    def _():
        h_scr[...] = _modulated_norm(x_ref[0], g_ref[...], sc_ref[0], sh_ref[0]).astype(BF16)

    res = jnp.dot(h_scr[...], w_ref[...], preferred_element_type=F32)
    if head_major:
        for hh in range(ATT_HEADS):
            o_ref[0, 0, hh] = res[:, hh * HEAD_DIM:(hh + 1) * HEAD_DIM].astype(o_ref.dtype)
    else:
        o_ref[0] = res.astype(o_ref.dtype)


def norm_project(x, g, scale, shift, w, out_dtype, head_major=False, chunk_interleaved=False, tm=512, tn=1024):
    n = w.shape[1]
    grid = (BATCH, SEQ // tm, n // tn)
    if head_major:
        assert tn == ATT_W
        out_shape = jax.ShapeDtypeStruct((n // tn, BATCH, ATT_HEADS, SEQ, HEAD_DIM), out_dtype)
        out_spec = pl.BlockSpec((1, 1, ATT_HEADS, tm, HEAD_DIM), lambda b, i, j: (j, b, 0, i, 0))
    elif chunk_interleaved:
        assert tm == LRU_CL
        ncol = n // tn
        out_shape = jax.ShapeDtypeStruct((BATCH, LRU_CL, LRU_CHUNKS * n), out_dtype)
        out_spec = pl.BlockSpec((1, tm, tn), lambda b, i, j: (b, 0, i * ncol + j))
    else:
        out_shape = jax.ShapeDtypeStruct((BATCH, SEQ, n), out_dtype)
        out_spec = pl.BlockSpec((1, tm, tn), lambda b, i, j: (b, i, j))
    return pl.pallas_call(
        functools.partial(_norm_mm_kernel, head_major=head_major),
        grid=grid,
        in_specs=[
            pl.BlockSpec((1, tm, D_MODEL), lambda b, i, j: (b, i, 0)),
            pl.BlockSpec((1, D_MODEL), lambda b, i, j: (0, 0)),
            pl.BlockSpec((1, 1, D_MODEL), lambda b, i, j: (b, 0, 0)),
            pl.BlockSpec((1, 1, D_MODEL), lambda b, i, j: (b, 0, 0)),
            pl.BlockSpec((D_MODEL, tn), lambda b, i, j: (0, j)),
        ],
        out_specs=out_spec,
        out_shape=out_shape,
        scratch_shapes=[pltpu.VMEM((tm, D_MODEL), BF16)],
        compiler_params=_cparams(("parallel", "parallel", "arbitrary")),
        name="norm_project",
    )(x, g.reshape(1, D_MODEL), scale, shift, w)


def _mm_residual_kernel(*refs, n_in):
    a_refs, w_refs = refs[:n_in], refs[n_in:2 * n_in]
    x_ref, gate_ref, o_ref = refs[2 * n_in:]
    acc = jnp.dot(a_refs[0][0], w_refs[0][...], preferred_element_type=F32)
    for a_ref, w_ref in zip(a_refs[1:], w_refs[1:]):
        acc = acc + jnp.dot(a_ref[0], w_ref[...], preferred_element_type=F32)
    o_ref[0] = x_ref[0] + gate_ref[0] * acc


def project_residual(a_list, w_list, x, gate, tm=512, tn=1024):
    n_in = len(a_list)
    grid = (BATCH, SEQ // tm, D_MODEL // tn)
    in_specs = [pl.BlockSpec((1, tm, a.shape[-1]), lambda b, i, j: (b, i, 0)) for a in a_list]
    in_specs += [pl.BlockSpec((w.shape[0], tn), lambda b, i, j: (0, j)) for w in w_list]
    in_specs += [
        pl.BlockSpec((1, tm, tn), lambda b, i, j: (b, i, j)),
        pl.BlockSpec((1, 1, tn), lambda b, i, j: (b, 0, j)),
    ]
    return pl.pallas_call(
        functools.partial(_mm_residual_kernel, n_in=n_in),
        grid=grid,
        in_specs=in_specs,
        out_specs=pl.BlockSpec((1, tm, tn), lambda b, i, j: (b, i, j)),
        out_shape=jax.ShapeDtypeStruct((BATCH, SEQ, D_MODEL), F32),
        compiler_params=_cparams(("parallel", "parallel", "parallel")),
        name="project_residual",
    )(*a_list, *w_list, x, gate)


def _conv_kernel(v_ref, g_ref, cw_ref, cb_ref, ng_ref, nb_ref, o_ref, pad_scr, *, ts):
    zeros = jnp.zeros((CONV_HALO, 128), F32)
    pad_scr[0:CONV_HALO, :] = zeros
    pad_scr[SEQ + CONV_HALO:SEQ + 2 * CONV_HALO, :] = zeros
    for t0 in range(0, SEQ, ts):
        pad_scr[CONV_HALO + t0:CONV_HALO + t0 + ts, :] = v_ref[0, t0:t0 + ts, :] * _sigmoid(g_ref[0, t0:t0 + ts, :])
    first = CONV_HALO - CONV_WIDTH // 2
    for t0 in range(0, SEQ, ts):
        acc = cw_ref[0:1, :] * pad_scr[first + t0:first + t0 + ts, :] + cb_ref[...]
        for k in range(1, CONV_WIDTH):
            acc = acc + cw_ref[k:k + 1, :] * pad_scr[first + t0 + k:first + t0 + k + ts, :]
        mu = jnp.mean(acc, axis=-1, keepdims=True)
        cen = acc - mu
        var = jnp.mean(cen * cen, axis=-1, keepdims=True)
        y = cen * lax.rsqrt(var + EPS) * ng_ref[...] + nb_ref[...]
        o_ref[0, t0:t0 + ts, :] = (y * _sigmoid(y)).astype(o_ref.dtype)


def conv_module(p, conv_w, conv_b, norm_g, norm_b, ts=128):
    ng = CONV_GROUPS
    vec = lambda a: a.reshape(1, CONV_CH)
    vspec = pl.BlockSpec((1, 128), lambda b, c: (0, c))
    return pl.pallas_call(
        functools.partial(_conv_kernel, ts=ts),
        grid=(BATCH, ng),
        in_specs=[
            pl.BlockSpec((1, SEQ, 128), lambda b, c: (b, 0, c)),
            pl.BlockSpec((1, SEQ, 128), lambda b, c: (b, 0, c + ng)),
            pl.BlockSpec((CONV_WIDTH, 128), lambda b, c: (0, c)),
            vspec, vspec, vspec,
        ],
        out_specs=pl.BlockSpec((1, SEQ, 128), lambda b, c: (b, 0, c)),
        out_shape=jax.ShapeDtypeStruct((BATCH, SEQ, CONV_CH), BF16),
        scratch_shapes=[pltpu.VMEM((SEQ + 2 * CONV_HALO, 128), F32)],
        compiler_params=_cparams(("parallel", "parallel")),
        name="conv_module",
    )(p, p, conv_w, vec(conv_b), vec(norm_g), vec(norm_b))


def _attn_kernel(q_ref, k_ref, v_ref, bias_ref, o_ref, st_ref, kpad, vpad, *, sub, nres):
    head = pl.program_id(2)
    lane = lax.broadcasted_iota(jnp.int32, (ATT_QB, 128), 1)
    zeros = jnp.zeros((ATT_RADIUS, HEAD_DIM), BF16)
    scale = HEAD_DIM ** -0.5
    nblk = sub // ATT_QB

    @pl.when(head == 0)
    def _():
        st_ref[...] = jnp.zeros(st_ref.shape, F32)

    for r in range(nres):
        lanes = slice(r * HEAD_DIM, (r + 1) * HEAD_DIM)
        for pad, src in ((kpad, k_ref), (vpad, v_ref)):
            pad[0:ATT_RADIUS, :] = zeros
            pad[sub + ATT_RADIUS:sub + 2 * ATT_RADIUS, :] = zeros
            pad[ATT_RADIUS:sub + ATT_RADIUS, :] = src[0, 0, :, lanes]

        def block(i, carry):
            q0 = pl.multiple_of(i * ATT_QB, ATT_QB)
            qb = q_ref[0, 0, pl.ds(q0, ATT_QB), lanes]
            kb = kpad[pl.ds(q0, ATT_KB), :]
            vb = vpad[pl.ds(q0, ATT_KB), :]
            s = lax.dot_general(qb, kb, (((1,), (1,)), ((), ())), preferred_element_type=F32)
            variant = jnp.where(i == 0, 0, jnp.where(i == nblk - 1, 2, 1))
            s = s * scale + bias_ref[0, variant]
            m = jnp.max(s, axis=-1, keepdims=True)
            e = jnp.exp(s - m)
            den = jnp.sum(e, axis=-1, keepdims=True)
            o = jnp.dot(e.astype(BF16), vb, preferred_element_type=F32) / den
            o_ref[0, 0, pl.ds(q0, ATT_QB), lanes] = o.astype(o_ref.dtype)
            lse = m + jnp.log(den)
            st_ref[0, pl.ds(q0, ATT_QB), lanes] = jnp.where(lane == head, lse, st_ref[0, pl.ds(q0, ATT_QB), lanes])
            return carry

        lax.fori_loop(0, nblk, block, 0, unroll=min(nblk, 4))


def _alibi_bias(dilation):
    row = jnp.arange(ATT_QB)[:, None]
    colm = jnp.arange(ATT_KB)[None, :]
    rel = jnp.abs(colm - ATT_RADIUS - row)
    slopes = 2.0 ** (-ALIBI_MAX * jnp.arange(1, ATT_HEADS + 1, dtype=F32) / ATT_HEADS)
    bias = -slopes[:, None, None] * (rel * dilation).astype(F32)[None]
    band = rel <= ATT_RADIUS
    masks = jnp.stack([band & (colm >= ATT_RADIUS), band, band & (colm < ATT_KB - ATT_RADIUS)])
    return jnp.where(masks[None], bias[:, None], NEG_INF)


def dilated_attention_branch(qkv, dilation):
    sub = SEQ // dilation
    assert sub >= 2 * ATT_QB
    nres = min(dilation, 4)
    qkv_v = qkv.reshape(3, BATCH, ATT_HEADS, sub, dilation * HEAD_DIM)
    wid = nres * HEAD_DIM
    qspec = lambda which: pl.BlockSpec((None, 1, 1, sub, wid), lambda b, r, h, which=which: (which, b, h, 0, r))
    o, st = pl.pallas_call(
        functools.partial(_attn_kernel, sub=sub, nres=nres),
        grid=(BATCH, dilation // nres, ATT_HEADS),
        in_specs=[qspec(0), qspec(1), qspec(2),
                  pl.BlockSpec((1, 3, ATT_QB, ATT_KB), lambda b, r, h: (h, 0, 0, 0))],
        out_specs=[
            pl.BlockSpec((1, 1, sub, wid), lambda b, r, h: (b, h, 0, r)),
            pl.BlockSpec((1, sub, wid), lambda b, r, h: (b, 0, r)),
        ],
        out_shape=[
            jax.ShapeDtypeStruct((BATCH, ATT_HEADS, sub, dilation * HEAD_DIM), F32),
            jax.ShapeDtypeStruct((BATCH, sub, dilation * 128), F32),
        ],
        scratch_shapes=[pltpu.VMEM((sub + 2 * ATT_RADIUS, HEAD_DIM), BF16),
                        pltpu.VMEM((sub + 2 * ATT_RADIUS, HEAD_DIM), BF16)],
        compiler_params=_cparams(("parallel", "parallel", "arbitrary")),
        name=f"dilated_attention_d{dilation}",
    )(qkv_v, qkv_v, qkv_v, _alibi_bias(dilation))
    return o.reshape(BATCH, ATT_HEADS, SEQ, HEAD_DIM), st.reshape(BATCH, SEQ, 128)


def _merge_kernel(o1_ref, o2_ref, o3_ref, s1_ref, s2_ref, s3_ref, out_ref):
    l1, l2, l3 = s1_ref[0], s2_ref[0], s3_ref[0]
    m = jnp.maximum(jnp.maximum(l1, l2), l3)
    e1, e2, e3 = jnp.exp(l1 - m), jnp.exp(l2 - m), jnp.exp(l3 - m)
    inv = 1.0 / (e1 + e2 + e3)
    w1, w2, w3 = e1 * inv, e2 * inv, e3 * inv
    for h in range(ATT_HEADS):
        acc = (w1[:, h:h + 1] * o1_ref[0, h] + w2[:, h:h + 1] * o2_ref[0, h]
               + w3[:, h:h + 1] * o3_ref[0, h])
        out_ref[0, :, h * HEAD_DIM:(h + 1) * HEAD_DIM] = acc.astype(out_ref.dtype)


def merge_branches(outs, stats, tm=512):
    ospec = pl.BlockSpec((1, ATT_HEADS, tm, HEAD_DIM), lambda b, i: (b, 0, i, 0))
    sspec = pl.BlockSpec((1, tm, 128), lambda b, i: (b, i, 0))
    return pl.pallas_call(
        _merge_kernel,
        grid=(BATCH, SEQ // tm),
        in_specs=[ospec] * 3 + [sspec] * 3,
        out_specs=pl.BlockSpec((1, tm, ATT_W), lambda b, i: (b, i, 0)),
        out_shape=jax.ShapeDtypeStruct((BATCH, SEQ, ATT_W), BF16),
        compiler_params=_cparams(("parallel", "parallel")),
        name="merge_branches",
    )(*outs, *stats)


def _softplus(x):
    return jnp.maximum(x, 0.0) + jnp.log1p(jnp.exp(-jnp.abs(x)))


def _gelu_tanh(x):
    return 0.5 * x * (1.0 + jnp.tanh(0.7978845608028654 * (x + 0.044715 * (x * x * x))))


def _lru_kernel(x_ref, cw_ref, cb_ref, w_ref, b_ref, lam_ref, o_ref, xc, a_scr, u_scr, yacc):
    ng = LRU_CHUNKS
    gw = LRU_GW
    ntile = gw // 128
    lead = LRU_CONV // 2
    zrow = jnp.zeros((1, gw), F32)

    def x_group(i):
        if i < 0:
            g = x_ref[0, (LRU_CL + i) * ng:(LRU_CL + i + 1) * ng, :]
            return jnp.concatenate([zrow, g[:ng - 1]], axis=0)
        if i >= LRU_CL:
            g = x_ref[0, (i - LRU_CL) * ng:(i - LRU_CL + 1) * ng, :]
            return jnp.concatenate([g[1:], zrow], axis=0)
        return x_ref[0, i * ng:(i + 1) * ng, :]

    edge_steps = list(range(lead)) + list(range(LRU_CL - (LRU_CONV - 1 - lead), LRU_CL))
    for i in edge_steps:
        acc = cw_ref[0:1, :] * x_group(i - lead) + cb_ref[...]
        for k in range(1, LRU_CONV):
            acc = acc + cw_ref[k:k + 1, :] * x_group(i - lead + k)
        xc[i * ng:(i + 1) * ng, :] = acc
    lo, hi = lead * ng, SEQ - (LRU_CONV - 1 - lead) * ng
    for r0 in range(lo, hi, LRU_TILE):
        r1 = min(r0 + LRU_TILE, hi)
        acc = cw_ref[0:1, :] * x_ref[0, r0 - lead * ng:r1 - lead * ng, :] + cb_ref[...]
        for k in range(1, LRU_CONV):
            off = (k - lead) * ng
            acc = acc + cw_ref[k:k + 1, :] * x_ref[0, r0 + off:r1 + off, :]
        xc[r0:r1, :] = acc

    for d in range(2):
        reverse = d == 1
        half_decay = (0.5 * LRU_C) * _softplus(-lam_ref[d:d + 1, :])
        for r0 in range(0, SEQ, LRU_TILE):
            rows = slice(r0, r0 + LRU_TILE)
            xj = xc[rows, :]
            pre = jnp.dot(xj.astype(BF16), w_ref[d, 0], preferred_element_type=F32) + b_ref[d, 0]
            t_r = jnp.tanh(pre[:, :gw])
            t_i = jnp.tanh(pre[:, gw:])
            neg_log_a = t_r * half_decay + half_decay
            a = jnp.exp2(neg_log_a * (-LOG2E))
            one_minus_a2 = jnp.tanh(neg_log_a) * (a * a + 1.0)
            root = one_minus_a2 * lax.rsqrt(jnp.maximum(one_minus_a2, 1e-30))
            a_scr[rows, :] = a
            u_scr[rows, :] = root * ((0.5 * t_i + 0.5) * xj)

        def step(ii, carry):
            r0 = pl.multiple_of(((LRU_CL - 1 - ii) if reverse else ii) * ng, ng)
            out = []
            for lt in range(ntile):
                lanes = slice(lt * 128, (lt + 1) * 128)
                h, prod = carry[lt]
                a = a_scr[pl.ds(r0, ng), lanes]
                h = a * h + u_scr[pl.ds(r0, ng), lanes]
                prod = a * prod
                u_scr[pl.ds(r0, ng), lanes] = h
                a_scr[pl.ds(r0, ng), lanes] = prod
                out.append((h, prod))
            return tuple(out)

        init = tuple((jnp.zeros((ng, 128), F32), jnp.ones((ng, 128), F32)) for _ in range(ntile))
        lax.fori_loop(0, LRU_CL, step, init, unroll=4)

        last = 0 if reverse else LRU_CL - 1
        h_end = u_scr[last * ng:(last + 1) * ng, :]
        p_end = a_scr[last * ng:(last + 1) * ng, :]
        entering = [None] * ng
        carry = zrow
        for j in (range(ng - 1, -1, -1) if reverse else range(ng)):
            entering[j] = carry
            carry = h_end[j:j + 1, :] + p_end[j:j + 1, :] * carry
        entering = jnp.concatenate(entering, axis=0)[None]
        for r0 in range(0, SEQ, LRU_TILE):
            rows = slice(r0, r0 + LRU_TILE)
            local = u_scr[rows, :].reshape(LRU_TILE // ng, ng, gw)
            prod = a_scr[rows, :].reshape(LRU_TILE // ng, ng, gw)
            y = (local + prod * entering).reshape(LRU_TILE, gw)
            if reverse:
                o_ref[0, rows, :] = (yacc[rows, :] + y).astype(o_ref.dtype)
            else:
                yacc[rows, :] = y


def rglru_block(p, conv_w, conv_b, w_gates, b_gates, lam):
    npair = LRU_WP // LRU_GW
    gw = LRU_GW
    big = lambda: pltpu.VMEM((SEQ, gw), F32)
    return pl.pallas_call(
        _lru_kernel,
        grid=(BATCH, npair),
        in_specs=[
            pl.BlockSpec((1, SEQ, gw), lambda b, n: (b, 0, n + npair)),
            pl.BlockSpec((LRU_CONV, gw), lambda b, n: (0, n)),
            pl.BlockSpec((1, gw), lambda b, n: (0, n)),
            pl.BlockSpec((2, 1, gw, 2 * gw), lambda b, n: (0, n, 0, 0)),
            pl.BlockSpec((2, 1, 1, 2 * gw), lambda b, n: (0, n, 0, 0)),
            pl.BlockSpec((2, gw), lambda b, n: (0, n)),
        ],
        out_specs=pl.BlockSpec((1, SEQ, gw), lambda b, n: (b, 0, n)),
        out_shape=jax.ShapeDtypeStruct((BATCH, SEQ, LRU_WP), BF16),
        scratch_shapes=[big(), big(), big(), big()],
        compiler_params=pltpu.CompilerParams(dimension_semantics=("parallel", "parallel"),
                                             vmem_limit_bytes=BIG_VMEM_LIMIT),
        name="rglru_block",
    )(p, conv_w, conv_b, w_gates, b_gates, lam)


def _gated_mm_residual_kernel(y_ref, gate_ref, w_ref, x_ref, g_ref, o_ref, a_scr):
    @pl.when(pl.program_id(2) == 0)
    def _():
        a_scr[...] = (_gelu_tanh(gate_ref[0]) * y_ref[0].astype(F32)).astype(BF16)

    o_ref[0] = x_ref[0] + g_ref[0] * jnp.dot(a_scr[...], w_ref[...], preferred_element_type=F32)


def lru_out_project(y, p, w, x, gate, tn=1024):
    yv = y.reshape(BATCH, LRU_CL, LRU_CHUNKS * LRU_WP)
    pv = p.reshape(BATCH, LRU_CL, LRU_CHUNKS * 2 * LRU_WP)
    return pl.pallas_call(
        _gated_mm_residual_kernel,
        grid=(BATCH, LRU_CHUNKS, D_MODEL // tn),
        in_specs=[
            pl.BlockSpec((1, LRU_CL, LRU_WP), lambda b, j, n: (b, 0, j)),
            pl.BlockSpec((1, LRU_CL, LRU_WP), lambda b, j, n: (b, 0, 2 * j)),
            pl.BlockSpec((LRU_WP, tn), lambda b, j, n: (0, n)),
            pl.BlockSpec((1, LRU_CL, tn), lambda b, j, n: (b, j, n)),
            pl.BlockSpec((1, 1, tn), lambda b, j, n: (b, 0, n)),
        ],
        out_specs=pl.BlockSpec((1, LRU_CL, tn), lambda b, j, n: (b, j, n)),
        out_shape=jax.ShapeDtypeStruct((BATCH, SEQ, D_MODEL), F32),
        scratch_shapes=[pltpu.VMEM((LRU_CL, LRU_WP), BF16)],
        compiler_params=_cparams(("parallel", "parallel", "arbitrary")),
        name="lru_out_project",
    )(yv, pv, w, x, gate)


def _pad_blocks(a):
    lead = a.shape[:-1]
    a = a.reshape(*lead, LRU_BLOCKS, LRU_BW)
    a = jnp.pad(a, [(0, 0)] * len(lead) + [(0, 0), (0, LRU_PW - LRU_BW)])
    return a.reshape(*lead, LRU_WP)


def _pair_block_diagonal(w):
    pad = LRU_PW - LRU_BW
    w = jnp.pad(w, ((0, 0), (0, 0), (0, pad), (0, pad))).reshape(2, LRU_BLOCKS // 2, 2, LRU_PW, LRU_PW)
    zero = jnp.zeros_like(w[:, :, 0])
    top = jnp.concatenate([w[:, :, 0], zero], axis=-1)
    bottom = jnp.concatenate([zero, w[:, :, 1]], axis=-1)
    return jnp.concatenate([top, bottom], axis=-2)


def _router_kernel(x_ref, g_ref, sc_ref, sh_ref, wr_ref, h_ref, eid_ref, gt_ref):
    h = _modulated_norm(x_ref[0], g_ref[...], sc_ref[0], sh_ref[0])
    hb = h.astype(BF16)
    h_ref[0] = hb
    h_lo = (h - hb.astype(F32)).astype(BF16)
    w = wr_ref[...]
    w_hi = w.astype(BF16)
    w_lo = (w - w_hi.astype(F32)).astype(BF16)
    lg = (jnp.dot(hb, w_hi, preferred_element_type=F32) + jnp.dot(h_lo, w_hi, preferred_element_type=F32)
          + jnp.dot(hb, w_lo, preferred_element_type=F32))
    lt = lg.T
    tm = lt.shape[1]
    row = lax.broadcasted_iota(jnp.int32, (8, tm), 0)
    big = jnp.int32(99)
    gl = jnp.where(row < N_GROUPS, lt[0:8], -jnp.inf)
    g_max = jnp.max(gl, axis=0, keepdims=True)
    g_sel = jnp.min(jnp.where(gl == g_max, row, big), axis=0, keepdims=True)
    g_prob = 1.0 / jnp.sum(jnp.exp(gl - g_max), axis=0, keepdims=True)
    el = jnp.zeros((8, tm), F32)
    for g in range(N_GROUPS):
        el = jnp.where(g_sel == g, lt[8 + 8 * g:16 + 8 * g], el)
    v1 = jnp.max(el, axis=0, keepdims=True)
    i1 = jnp.min(jnp.where(el == v1, row, big), axis=0, keepdims=True)
    el2 = jnp.where(row == i1, -jnp.inf, el)
    v2 = jnp.max(el2, axis=0, keepdims=True)
    i2 = jnp.min(jnp.where(el2 == v2, row, big), axis=0, keepdims=True)
    e2 = jnp.exp(v2 - v1)
    p1 = 1.0 / (1.0 + e2)
    p2 = e2 * p1
    eid_ref[0] = jnp.where(row == 0, g_sel * EXPERTS_PER_GROUP + i1,
                           jnp.where(row == 1, g_sel * EXPERTS_PER_GROUP + i2, 0))
    gt_ref[0] = jnp.where(row == 0, g_prob * p1, jnp.where(row == 1, g_prob * p2, 0.0))


def route(x, g, scale, shift, w_router, tm=512):
    return pl.pallas_call(
        _router_kernel,
        grid=(BATCH, SEQ // tm),
        in_specs=[
            pl.BlockSpec((1, tm, D_MODEL), lambda b, i: (b, i, 0)),
            pl.BlockSpec((1, D_MODEL), lambda b, i: (0, 0)),
            pl.BlockSpec((1, 1, D_MODEL), lambda b, i: (b, 0, 0)),
            pl.BlockSpec((1, 1, D_MODEL), lambda b, i: (b, 0, 0)),
            pl.BlockSpec((D_MODEL, ROUTER_LANES), lambda b, i: (0, 0)),
        ],
        out_specs=[
            pl.BlockSpec((1, tm, D_MODEL), lambda b, i: (b, i, 0)),
            pl.BlockSpec((1, 8, tm), lambda b, i: (b, 0, i)),
            pl.BlockSpec((1, 8, tm), lambda b, i: (b, 0, i)),
        ],
        out_shape=[
            jax.ShapeDtypeStruct((BATCH, SEQ, D_MODEL), BF16),
            jax.ShapeDtypeStruct((BATCH, 8, SEQ), jnp.int32),
            jax.ShapeDtypeStruct((BATCH, 8, SEQ), F32),
        ],
        compiler_params=_cparams(("parallel", "parallel")),
        name="route",
    )(x, g.reshape(1, D_MODEL), scale, shift, w_router)


def _expert_kernel(be_ref, nu_ref, first_ref, nxt_ref, x_ref, wgu_hbm, wd_hbm, o_ref,
                   gu_stage, d_stage, wgu, wd, sem):
    i = pl.program_id(0)
    active = i < nu_ref[0]

    def weight_copies(e):
        return (pltpu.make_async_copy(wgu_hbm.at[e], gu_stage, sem.at[0]),
                pltpu.make_async_copy(wd_hbm.at[e], d_stage, sem.at[1]))

    @pl.when(i == 0)
    def _():
        for cp in weight_copies(be_ref[0]):
            cp.start()

    @pl.when(active & (first_ref[i] == 1))
    def _():
        for cp in weight_copies(be_ref[i]):
            cp.wait()

        def cast_rows(stage, dst):
            def body(c, carry):
                r0 = pl.multiple_of(c * MOE_CAST_ROWS, MOE_CAST_ROWS)
                dst[pl.ds(r0, MOE_CAST_ROWS), :] = stage[pl.ds(r0, MOE_CAST_ROWS), :].astype(BF16)
                return carry
            lax.fori_loop(0, stage.shape[0] // MOE_CAST_ROWS, body, 0)

        cast_rows(gu_stage, wgu)
        cast_rows(d_stage, wd)

        @pl.when(nxt_ref[i] >= 0)
        def _():
            for cp in weight_copies(nxt_ref[i]):
                cp.start()

    @pl.when(active)
    def _():
        gu = jnp.dot(x_ref[...], wgu[...], preferred_element_type=F32)
        g = gu[:, :EXPERT_FF]
        u = gu[:, EXPERT_FF:]
        act = (g * _sigmoid(g) * u).astype(BF16)
        o_ref[...] = jnp.dot(act, wd[...], preferred_element_type=F32)

    @pl.when(jnp.logical_not(active))
    def _():
        o_ref[...] = jnp.zeros(o_ref.shape, F32)


def expert_blocks(xs, blk_expert, n_used, first, nxt, w_gate_up, w_down):
    row_map = lambda i, be, nu, first, nxt: (jnp.minimum(i, nu[0] - 1), 0)
    return pl.pallas_call(
        _expert_kernel,
        grid_spec=pltpu.PrefetchScalarGridSpec(
            num_scalar_prefetch=4,
            grid=(MOE_NBLK,),
            in_specs=[
                pl.BlockSpec((MOE_TB, D_MODEL), row_map),
                pl.BlockSpec(memory_space=pl.ANY),
                pl.BlockSpec(memory_space=pl.ANY),
            ],
            out_specs=pl.BlockSpec((MOE_TB, D_MODEL), lambda i, be, nu, first, nxt: (i, 0)),
            scratch_shapes=[
                pltpu.VMEM((D_MODEL, 2 * EXPERT_FF), F32),
                pltpu.VMEM((EXPERT_FF, D_MODEL), F32),
                pltpu.VMEM((D_MODEL, 2 * EXPERT_FF), BF16),
                pltpu.VMEM((EXPERT_FF, D_MODEL), BF16),
                pltpu.SemaphoreType.DMA((2,)),
            ],
        ),
        out_shape=jax.ShapeDtypeStruct((MOE_ROWS, D_MODEL), F32),
        compiler_params=pltpu.CompilerParams(dimension_semantics=("arbitrary",), vmem_limit_bytes=BIG_VMEM_LIMIT),
        name="expert_blocks",
    )(blk_expert, n_used, first, nxt, xs, w_gate_up, w_down)


def _router_weights(w_group, w_expert):
    w = jnp.zeros((D_MODEL, ROUTER_LANES), F32)
    w = w.at[:, 0:N_GROUPS].set(w_group)
    return w.at[:, 8:8 + N_EXPERTS].set(w_expert)


def hierarchical_moe(x, c, norm_g, ada_w, ada_b, w_group, w_expert, w_gate_up, w_down):
    shift, scale, gate_ada = ada_modulation(c, ada_w, ada_b)
    h, eid, gts = route(x, norm_g, scale, shift, _router_weights(w_group, w_expert))
    h = h.reshape(N_TOK, D_MODEL)
    e_flat = jnp.concatenate([eid[:, 0, :].reshape(N_TOK), eid[:, 1, :].reshape(N_TOK)])
    g_flat = jnp.concatenate([gts[:, 0, :].reshape(N_TOK), gts[:, 1, :].reshape(N_TOK)])
    tok = jnp.tile(jnp.arange(N_TOK, dtype=jnp.int32), 2)
    onehot = (e_flat[:, None] == jnp.arange(N_EXPERTS, dtype=jnp.int32)[None, :]).astype(jnp.int32)
    csum = jnp.cumsum(onehot, axis=0)
    rank = jnp.sum(csum * onehot, axis=1) - 1
    counts = csum[-1]
    pcounts = (counts + MOE_TB - 1) // MOE_TB * MOE_TB
    pend = jnp.cumsum(pcounts)
    dest = (pend - pcounts)[e_flat] + rank
    n_used = (pend[-1] // MOE_TB).astype(jnp.int32)
    blk_idx = jnp.arange(MOE_NBLK, dtype=jnp.int32)
    blk_start = jnp.minimum(blk_idx, n_used - 1) * MOE_TB
    blk_expert = jnp.minimum(jnp.sum(pend[None, :] <= blk_start[:, None], axis=1), N_EXPERTS - 1).astype(jnp.int32)
    first = (((blk_idx == 0) | (blk_expert != jnp.roll(blk_expert, 1))) & (blk_idx < n_used)).astype(jnp.int32)
    experts = jnp.arange(N_EXPERTS, dtype=jnp.int32)
    later = (counts > 0)[None, :] & (experts[None, :] > experts[:, None])
    nxt_of = jnp.min(jnp.where(later, experts[None, :], N_EXPERTS), axis=1)
    nxt = jnp.where(nxt_of < N_EXPERTS, nxt_of, -1)[blk_expert].astype(jnp.int32)
    buf_tok = jnp.zeros((MOE_ROWS,), jnp.int32).at[dest].set(tok)
    xs = h[buf_tok]
    yb = expert_blocks(xs, blk_expert, n_used.reshape(1), first, nxt, w_gate_up, w_down)
    y = g_flat[:N_TOK, None] * yb[dest[:N_TOK]] + g_flat[N_TOK:, None] * yb[dest[N_TOK:]]
    return x + gate_ada * y.reshape(BATCH, SEQ, D_MODEL)


def _final_norm_kernel(x_ref, g_ref, o_ref):
    x = x_ref[0]
    o_ref[0] = x * lax.rsqrt(jnp.mean(x * x, axis=-1, keepdims=True) + EPS) * g_ref[...]


def final_norm(x, g, tm=512):
    return pl.pallas_call(
        _final_norm_kernel,
        grid=(BATCH, SEQ // tm),
        in_specs=[pl.BlockSpec((1, tm, D_MODEL), lambda b, i: (b, i, 0)),
                  pl.BlockSpec((1, D_MODEL), lambda b, i: (0, 0))],
        out_specs=pl.BlockSpec((1, tm, D_MODEL), lambda b, i: (b, i, 0)),
        out_shape=jax.ShapeDtypeStruct((BATCH, SEQ, D_MODEL), F32),
        compiler_params=_cparams(("parallel", "parallel")),
        name="final_norm",
    )(x, g.reshape(1, D_MODEL))


def kernel(x, c, norm0_mix, ada0_mix_w, ada0_mix_b, w_in0, conv_w, conv_b, conv_norm_g, conv_norm_b, w_out0, norm0_ffn, ada0_ffn_w, ada0_ffn_b, moe0_w_group, moe0_w_expert, moe0_w_gate_up, moe0_w_down, norm1_mix, ada1_mix_w, ada1_mix_b, w_in1, lru_conv_w, lru_conv_b, lru_w_a, lru_b_a, lru_w_x, lru_b_x, lru_lambda, w_out1, norm1_ffn, ada1_ffn_w, ada1_ffn_b, moe1_w_group, moe1_w_expert, moe1_w_gate_up, moe1_w_down, norm_final):
    shift, scale, gate = ada_modulation(c, ada0_mix_w, ada0_mix_b)
    w_in0 = w_in0.astype(BF16)
    p_conv = norm_project(x, norm0_mix, scale, shift, w_in0[:, :2 * CONV_CH], F32)
    qkv = norm_project(x, norm0_mix, scale, shift, w_in0[:, 2 * CONV_CH:], BF16, head_major=True)
    a = conv_module(p_conv, conv_w, conv_b, conv_norm_g, conv_norm_b)
    outs, stats = zip(*(dilated_attention_branch(qkv, dilation) for _, dilation in ATT_BRANCHES))
    o = merge_branches(outs, stats)
    w_out0 = w_out0.astype(BF16)
    x = project_residual([a, o], [w_out0[:CONV_CH], w_out0[CONV_CH:]], x, gate)
    x = hierarchical_moe(x, c, norm0_ffn, ada0_ffn_w, ada0_ffn_b, moe0_w_group, moe0_w_expert,
                         moe0_w_gate_up, moe0_w_down)

    shift, scale, gate = ada_modulation(c, ada1_mix_w, ada1_mix_b)
    w_in1p = jnp.concatenate([_pad_blocks(w_in1[:, :LRU_WIDTH]), _pad_blocks(w_in1[:, LRU_WIDTH:])], axis=1)
    p_lru = norm_project(x, norm1_mix, scale, shift, w_in1p.astype(BF16), F32, chunk_interleaved=True)
    p_lru = p_lru.reshape(BATCH, SEQ, 2 * LRU_WP)
    npair = LRU_WP // LRU_GW
    w_gates = jnp.concatenate([_pair_block_diagonal(lru_w_a), _pair_block_diagonal(lru_w_x)], axis=-1)
    b_gates = jnp.concatenate([_pad_blocks(lru_b_a).reshape(2, npair, 1, LRU_GW),
                               _pad_blocks(lru_b_x).reshape(2, npair, 1, LRU_GW)], axis=-1)
    y = rglru_block(p_lru, _pad_blocks(lru_conv_w), _pad_blocks(lru_conv_b).reshape(1, LRU_WP),
                    (0.5 * w_gates).astype(BF16), 0.5 * b_gates, _pad_blocks(lru_lambda))
    w_out1p = jnp.pad(w_out1.reshape(LRU_BLOCKS, LRU_BW, D_MODEL), ((0, 0), (0, LRU_PW - LRU_BW), (0, 0)))
    x = lru_out_project(y, p_lru, w_out1p.reshape(LRU_WP, D_MODEL).astype(BF16), x, gate)
    x = hierarchical_moe(x, c, norm1_ffn, ada1_ffn_w, ada1_ffn_b, moe1_w_group, moe1_w_expert,
                         moe1_w_gate_up, moe1_w_down)
    return final_norm(x, norm_final)
```

```python
import functools

import jax
import jax.numpy as jnp
from jax import lax
from jax.experimental import pallas as pl
from jax.experimental.pallas import tpu as pltpu
from jax.experimental.pallas import tpu_sc as plsc

F32 = jnp.float32
BF16 = jnp.bfloat16

D_MODEL = 2048
BATCH = 4
SEQ = 4096
N_TOK = BATCH * SEQ
EPS = 1e-6
NEG_INF = -1e30

CONV_CH = 1024
CONV_GROUPS = 8
CONV_WIDTH = 31
CONV_HALO = 16

ATT_HEADS = 8
HEAD_DIM = 128
ATT_W = ATT_HEADS * HEAD_DIM
ATT_BRANCHES = ((128, 1), (512, 4), (2048, 16))
ATT_RADIUS = 64
ATT_QB = 128
ATT_KB = ATT_QB + 2 * ATT_RADIUS
ALIBI_MAX = 8.0

LRU_WIDTH = 2688
LRU_BLOCKS = 16
LRU_BW = LRU_WIDTH // LRU_BLOCKS
LRU_PW = 192
LRU_WP = LRU_BLOCKS * LRU_PW
LRU_GW = 2 * LRU_PW
LRU_CONV = 4
LRU_C = 8.0
LRU_CHUNKS = 8
LRU_CL = SEQ // LRU_CHUNKS
LRU_PITCH = LRU_CL + 8
LOG2E = 1.4426950408889634

N_GROUPS = 4
EXPERTS_PER_GROUP = 8
N_EXPERTS = 32
TOP_K = 2
EXPERT_FF = 1024
MOE_TB = 256
MOE_ROWS = N_TOK * TOP_K + N_EXPERTS * MOE_TB
MOE_NBLK = MOE_ROWS // MOE_TB
MOE_CAST_ROWS = 256
META_LANES = 256
assert MOE_NBLK <= META_LANES
SC_CORES = 2
SC_SUBCORES = 16
SC_CHUNK = 16
ROUTER_LANES = 128

VMEM_LIMIT = 48 * 1024 * 1024
BIG_VMEM_LIMIT = 56 * 1024 * 1024


def _cparams(sem):
    return pltpu.CompilerParams(dimension_semantics=sem, vmem_limit_bytes=VMEM_LIMIT)


def _sigmoid(x):
    return 0.5 * jnp.tanh(0.5 * x) + 0.5


def _ada_kernel(ct_ref, w_ref, b_ref, o_ref):
    ct = ct_ref[...]
    st = ct * _sigmoid(ct)
    w = w_ref[...]
    rows = [jnp.sum(w * st[:, b:b + 1], axis=0, keepdims=True) for b in range(BATCH)]
    o_ref[...] = jnp.concatenate(rows, axis=0) + b_ref[...]


def ada_modulation(c, w, b):
    tn = 512
    n = w.shape[1]
    mod = pl.pallas_call(
        _ada_kernel,
        grid=(n // tn,),
        in_specs=[
            pl.BlockSpec((D_MODEL, BATCH), lambda j: (0, 0)),
            pl.BlockSpec((D_MODEL, tn), lambda j: (0, j)),
            pl.BlockSpec((1, tn), lambda j: (0, j)),
        ],
        out_specs=pl.BlockSpec((BATCH, tn), lambda j: (0, j)),
        out_shape=jax.ShapeDtypeStruct((BATCH, n), F32),
        compiler_params=_cparams(("parallel",)),
        name="ada_modulation",
    )(c.T, w, b.reshape(1, n))
    shift, scale, gate = jnp.split(mod, 3, axis=-1)
    return shift[:, None, :], scale[:, None, :], gate[:, None, :]


def _modulated_norm(x, g, scale, shift):
    y = x * lax.rsqrt(jnp.mean(x * x, axis=-1, keepdims=True) + EPS)
    return (y * g) * (1.0 + scale) + shift


def _norm_mm_kernel(x_ref, g_ref, sc_ref, sh_ref, w_ref, o_ref, h_scr, *, head_major):
    @pl.when(pl.program_id(2) == 0)
    def _():
        h_scr[...] = _modulated_norm(x_ref[0], g_ref[...], sc_ref[0], sh_ref[0]).astype(BF16)

    res = jnp.dot(h_scr[...], w_ref[...], preferred_element_type=F32)
    if head_major:
        for hh in range(ATT_HEADS):
            o_ref[0, 0, hh] = res[:, hh * HEAD_DIM:(hh + 1) * HEAD_DIM].astype(o_ref.dtype)
    else:
        o_ref[0] = res.astype(o_ref.dtype)


def norm_project(x, g, scale, shift, w, out_dtype, head_major=False, tm=512, tn=1024):
    n = w.shape[1]
    grid = (BATCH, SEQ // tm, n // tn)
    if head_major:
        assert tn == ATT_W
        out_shape = jax.ShapeDtypeStruct((n // tn, BATCH, ATT_HEADS, SEQ, HEAD_DIM), out_dtype)
        out_spec = pl.BlockSpec((1, 1, ATT_HEADS, tm, HEAD_DIM), lambda b, i, j: (j, b, 0, i, 0))
    else:
        out_shape = jax.ShapeDtypeStruct((BATCH, SEQ, n), out_dtype)
        out_spec = pl.BlockSpec((1, tm, tn), lambda b, i, j: (b, i, j))
    return pl.pallas_call(
        functools.partial(_norm_mm_kernel, head_major=head_major),
        grid=grid,
        in_specs=[
            pl.BlockSpec((1, tm, D_MODEL), lambda b, i, j: (b, i, 0)),
            pl.BlockSpec((1, D_MODEL), lambda b, i, j: (0, 0)),
            pl.BlockSpec((1, 1, D_MODEL), lambda b, i, j: (b, 0, 0)),
            pl.BlockSpec((1, 1, D_MODEL), lambda b, i, j: (b, 0, 0)),
            pl.BlockSpec((D_MODEL, tn), lambda b, i, j: (0, j)),
        ],
        out_specs=out_spec,
        out_shape=out_shape,
        scratch_shapes=[pltpu.VMEM((tm, D_MODEL), BF16)],
        compiler_params=_cparams(("parallel", "parallel", "arbitrary")),
        name="norm_project",
    )(x, g.reshape(1, D_MODEL), scale, shift, w)


def _mm_residual_kernel(*refs, n_in):
    a_refs, w_refs = refs[:n_in], refs[n_in:2 * n_in]
    x_ref, gate_ref, o_ref = refs[2 * n_in:]
    acc = jnp.dot(a_refs[0][0], w_refs[0][...], preferred_element_type=F32)
    for a_ref, w_ref in zip(a_refs[1:], w_refs[1:]):
        acc = acc + jnp.dot(a_ref[0], w_ref[...], preferred_element_type=F32)
    o_ref[0] = x_ref[0] + gate_ref[0] * acc


def project_residual(a_list, w_list, x, gate, tm=512, tn=1024):
    n_in = len(a_list)
    grid = (BATCH, SEQ // tm, D_MODEL // tn)
    in_specs = [pl.BlockSpec((1, tm, a.shape[-1]), lambda b, i, j: (b, i, 0)) for a in a_list]
    in_specs += [pl.BlockSpec((w.shape[0], tn), lambda b, i, j: (0, j)) for w in w_list]
    in_specs += [
        pl.BlockSpec((1, tm, tn), lambda b, i, j: (b, i, j)),
        pl.BlockSpec((1, 1, tn), lambda b, i, j: (b, 0, j)),
    ]
    return pl.pallas_call(
        functools.partial(_mm_residual_kernel, n_in=n_in),
        grid=grid,
        in_specs=in_specs,
        out_specs=pl.BlockSpec((1, tm, tn), lambda b, i, j: (b, i, j)),
        out_shape=jax.ShapeDtypeStruct((BATCH, SEQ, D_MODEL), F32),
        compiler_params=_cparams(("parallel", "parallel", "parallel")),
        name="project_residual",
    )(*a_list, *w_list, x, gate)


def _conv_kernel(v_ref, g_ref, cw_ref, cb_ref, ng_ref, nb_ref, o_ref, pad_scr, *, ts):
    zeros = jnp.zeros((CONV_HALO, 128), F32)
    pad_scr[0:CONV_HALO, :] = zeros
    pad_scr[SEQ + CONV_HALO:SEQ + 2 * CONV_HALO, :] = zeros
    for t0 in range(0, SEQ, ts):
        pad_scr[CONV_HALO + t0:CONV_HALO + t0 + ts, :] = v_ref[0, t0:t0 + ts, :] * _sigmoid(g_ref[0, t0:t0 + ts, :])
    first = CONV_HALO - CONV_WIDTH // 2
    for t0 in range(0, SEQ, ts):
        acc = cw_ref[0:1, :] * pad_scr[first + t0:first + t0 + ts, :] + cb_ref[...]
        for k in range(1, CONV_WIDTH):
            acc = acc + cw_ref[k:k + 1, :] * pad_scr[first + t0 + k:first + t0 + k + ts, :]
        mu = jnp.mean(acc, axis=-1, keepdims=True)
        cen = acc - mu
        var = jnp.mean(cen * cen, axis=-1, keepdims=True)
        y = cen * lax.rsqrt(var + EPS) * ng_ref[...] + nb_ref[...]
        o_ref[0, t0:t0 + ts, :] = (y * _sigmoid(y)).astype(o_ref.dtype)


def conv_module(p, conv_w, conv_b, norm_g, norm_b, ts=128):
    ng = CONV_GROUPS
    vec = lambda a: a.reshape(1, CONV_CH)
    vspec = pl.BlockSpec((1, 128), lambda b, c: (0, c))
    return pl.pallas_call(
        functools.partial(_conv_kernel, ts=ts),
        grid=(BATCH, ng),
        in_specs=[
            pl.BlockSpec((1, SEQ, 128), lambda b, c: (b, 0, c)),
            pl.BlockSpec((1, SEQ, 128), lambda b, c: (b, 0, c + ng)),
            pl.BlockSpec((CONV_WIDTH, 128), lambda b, c: (0, c)),
            vspec, vspec, vspec,
        ],
        out_specs=pl.BlockSpec((1, SEQ, 128), lambda b, c: (b, 0, c)),
        out_shape=jax.ShapeDtypeStruct((BATCH, SEQ, CONV_CH), BF16),
        scratch_shapes=[pltpu.VMEM((SEQ + 2 * CONV_HALO, 128), F32)],
        compiler_params=_cparams(("parallel", "parallel")),
        name="conv_module",
    )(p, p, conv_w, vec(conv_b), vec(norm_g), vec(norm_b))


def _attn_kernel(q_ref, k_ref, v_ref, bias_ref, o_ref, st_ref, kpad, vpad, *, sub, nres):
    head = pl.program_id(2)
    lane = lax.broadcasted_iota(jnp.int32, (ATT_QB, 128), 1)
    zeros = jnp.zeros((ATT_RADIUS, HEAD_DIM), BF16)
    scale = HEAD_DIM ** -0.5
    nblk = sub // ATT_QB

    @pl.when(head == 0)
    def _():
        st_ref[...] = jnp.zeros(st_ref.shape, F32)

    for r in range(nres):
        lanes = slice(r * HEAD_DIM, (r + 1) * HEAD_DIM)
        for pad, src in ((kpad, k_ref), (vpad, v_ref)):
            pad[0:ATT_RADIUS, :] = zeros
            pad[sub + ATT_RADIUS:sub + 2 * ATT_RADIUS, :] = zeros
            pad[ATT_RADIUS:sub + ATT_RADIUS, :] = src[0, 0, :, lanes]

        def block(i, carry):
            q0 = pl.multiple_of(i * ATT_QB, ATT_QB)
            qb = q_ref[0, 0, pl.ds(q0, ATT_QB), lanes]
            kb = kpad[pl.ds(q0, ATT_KB), :]
            vb = vpad[pl.ds(q0, ATT_KB), :]
            s = lax.dot_general(qb, kb, (((1,), (1,)), ((), ())), preferred_element_type=F32)
            variant = jnp.where(i == 0, 0, jnp.where(i == nblk - 1, 2, 1))
            s = s * scale + bias_ref[0, variant]
            m = jnp.max(s, axis=-1, keepdims=True)
            e = jnp.exp(s - m)
            den = jnp.sum(e, axis=-1, keepdims=True)
            o = jnp.dot(e.astype(BF16), vb, preferred_element_type=F32) / den
            o_ref[0, 0, pl.ds(q0, ATT_QB), lanes] = o.astype(o_ref.dtype)
            lse = m + jnp.log(den)
            st_ref[0, pl.ds(q0, ATT_QB), lanes] = jnp.where(lane == head, lse, st_ref[0, pl.ds(q0, ATT_QB), lanes])
            return carry

        lax.fori_loop(0, nblk, block, 0, unroll=min(nblk, 4))


def _alibi_bias(dilation):
    row = jnp.arange(ATT_QB)[:, None]
    colm = jnp.arange(ATT_KB)[None, :]
    rel = jnp.abs(colm - ATT_RADIUS - row)
    slopes = 2.0 ** (-ALIBI_MAX * jnp.arange(1, ATT_HEADS + 1, dtype=F32) / ATT_HEADS)
    bias = -slopes[:, None, None] * (rel * dilation).astype(F32)[None]
    band = rel <= ATT_RADIUS
    masks = jnp.stack([band & (colm >= ATT_RADIUS), band, band & (colm < ATT_KB - ATT_RADIUS)])
    return jnp.where(masks[None], bias[:, None], NEG_INF)


def dilated_attention_branch(qkv, dilation):
    sub = SEQ // dilation
    assert sub >= 2 * ATT_QB
    nres = min(dilation, 4)
    qkv_v = qkv.reshape(3, BATCH, ATT_HEADS, sub, dilation * HEAD_DIM)
    wid = nres * HEAD_DIM
    qspec = lambda which: pl.BlockSpec((None, 1, 1, sub, wid), lambda b, r, h, which=which: (which, b, h, 0, r))
    o, st = pl.pallas_call(
        functools.partial(_attn_kernel, sub=sub, nres=nres),
        grid=(BATCH, dilation // nres, ATT_HEADS),
        in_specs=[qspec(0), qspec(1), qspec(2),
                  pl.BlockSpec((1, 3, ATT_QB, ATT_KB), lambda b, r, h: (h, 0, 0, 0))],
        out_specs=[
            pl.BlockSpec((1, 1, sub, wid), lambda b, r, h: (b, h, 0, r)),
            pl.BlockSpec((1, sub, wid), lambda b, r, h: (b, 0, r)),
        ],
        out_shape=[
            jax.ShapeDtypeStruct((BATCH, ATT_HEADS, sub, dilation * HEAD_DIM), F32),
            jax.ShapeDtypeStruct((BATCH, sub, dilation * 128), F32),
        ],
        scratch_shapes=[pltpu.VMEM((sub + 2 * ATT_RADIUS, HEAD_DIM), BF16),
                        pltpu.VMEM((sub + 2 * ATT_RADIUS, HEAD_DIM), BF16)],
        compiler_params=_cparams(("parallel", "parallel", "arbitrary")),
        name=f"dilated_attention_d{dilation}",
    )(qkv_v, qkv_v, qkv_v, _alibi_bias(dilation))
    return o.reshape(BATCH, ATT_HEADS, SEQ, HEAD_DIM), st.reshape(BATCH, SEQ, 128)


def _merge_kernel(o1_ref, o2_ref, o3_ref, s1_ref, s2_ref, s3_ref, out_ref):
    l1, l2, l3 = s1_ref[0], s2_ref[0], s3_ref[0]
    m = jnp.maximum(jnp.maximum(l1, l2), l3)
    e1, e2, e3 = jnp.exp(l1 - m), jnp.exp(l2 - m), jnp.exp(l3 - m)
    inv = 1.0 / (e1 + e2 + e3)
    w1, w2, w3 = e1 * inv, e2 * inv, e3 * inv
    for h in range(ATT_HEADS):
        acc = (w1[:, h:h + 1] * o1_ref[0, h] + w2[:, h:h + 1] * o2_ref[0, h]
               + w3[:, h:h + 1] * o3_ref[0, h])
        out_ref[0, :, h * HEAD_DIM:(h + 1) * HEAD_DIM] = acc.astype(out_ref.dtype)


def merge_branches(outs, stats, tm=512):
    ospec = pl.BlockSpec((1, ATT_HEADS, tm, HEAD_DIM), lambda b, i: (b, 0, i, 0))
    sspec = pl.BlockSpec((1, tm, 128), lambda b, i: (b, i, 0))
    return pl.pallas_call(
        _merge_kernel,
        grid=(BATCH, SEQ // tm),
        in_specs=[ospec] * 3 + [sspec] * 3,
        out_specs=pl.BlockSpec((1, tm, ATT_W), lambda b, i: (b, i, 0)),
        out_shape=jax.ShapeDtypeStruct((BATCH, SEQ, ATT_W), BF16),
        compiler_params=_cparams(("parallel", "parallel")),
        name="merge_branches",
    )(*outs, *stats)


def _softplus(x):
    return jnp.maximum(x, 0.0) + jnp.log1p(jnp.exp(-jnp.abs(x)))


def _gelu_tanh(x):
    return 0.5 * x * (1.0 + jnp.tanh(0.7978845608028654 * (x + 0.044715 * (x * x * x))))


def _lru_kernel(x_ref, cw_ref, cb_ref, w_ref, b_ref, lam_ref, o_ref, xc, a_scr, u_scr, yacc):
    gw = LRU_GW
    ntile = gw // 128
    lead = LRU_CONV // 2

    def shifted(r0, off):
        lo, hi = r0 + off, r0 + off + LRU_CL
        parts = []
        if lo < 0:
            parts.append(jnp.zeros((-lo, gw), F32))
        parts.append(x_ref[0, max(lo, 0):min(hi, SEQ), :])
        if hi > SEQ:
            parts.append(jnp.zeros((hi - SEQ, gw), F32))
        return parts[0] if len(parts) == 1 else jnp.concatenate(parts, axis=0)

    for j in range(LRU_CHUNKS):
        r0 = j * LRU_CL
        acc = cw_ref[0:1, :] * shifted(r0, -lead) + cb_ref[...]
        for k in range(1, LRU_CONV):
            acc = acc + cw_ref[k:k + 1, :] * shifted(r0, k - lead)
        xc[r0:r0 + LRU_CL, :] = acc

    for d in range(2):
        reverse = d == 1
        half_decay = (0.5 * LRU_C) * _softplus(-lam_ref[d:d + 1, :])
        for j in range(LRU_CHUNKS):
            xj = xc[j * LRU_CL:(j + 1) * LRU_CL, :]
            pre = jnp.dot(xj.astype(BF16), w_ref[d, 0], preferred_element_type=F32) + b_ref[d, 0]
            t_r = jnp.tanh(pre[:, :gw])
            t_i = jnp.tanh(pre[:, gw:])
            neg_log_a = t_r * half_decay + half_decay
            a = jnp.exp2(neg_log_a * (-LOG2E))
            one_minus_a2 = jnp.tanh(neg_log_a) * (a * a + 1.0)
            root = one_minus_a2 * lax.rsqrt(jnp.maximum(one_minus_a2, 1e-30))
            u = root * ((0.5 * t_i + 0.5) * xj)
            for lt in range(ntile):
                rows = slice(j * LRU_PITCH, j * LRU_PITCH + LRU_CL)
                a_scr[lt, rows, :] = a[:, lt * 128:(lt + 1) * 128]
                u_scr[lt, rows, :] = u[:, lt * 128:(lt + 1) * 128]

        def step(ii, carry):
            row = (LRU_CL - 1 - ii) if reverse else ii
            out = []
            for lt in range(ntile):
                h, prod = carry[lt]
                a = a_scr[lt, pl.ds(row, LRU_CHUNKS, stride=LRU_PITCH), :]
                h = a * h + u_scr[lt, pl.ds(row, LRU_CHUNKS, stride=LRU_PITCH), :]
                prod = a * prod
                u_scr[lt, pl.ds(row, LRU_CHUNKS, stride=LRU_PITCH), :] = h
                a_scr[lt, pl.ds(row, LRU_CHUNKS, stride=LRU_PITCH), :] = prod
                out.append((h, prod))
            return tuple(out)

        init = tuple((jnp.zeros((LRU_CHUNKS, 128), F32), jnp.ones((LRU_CHUNKS, 128), F32)) for _ in range(ntile))
        lax.fori_loop(0, LRU_CL, step, init, unroll=4)

        last = 0 if reverse else LRU_CL - 1
        for lt in range(ntile):
            lanes = slice(lt * 128, (lt + 1) * 128)
            h_end = u_scr[lt, pl.ds(last, LRU_CHUNKS, stride=LRU_PITCH), :]
            p_end = a_scr[lt, pl.ds(last, LRU_CHUNKS, stride=LRU_PITCH), :]
            carry = jnp.zeros((1, 128), F32)
            for j in (range(LRU_CHUNKS - 1, -1, -1) if reverse else range(LRU_CHUNKS)):
                src = slice(j * LRU_PITCH, j * LRU_PITCH + LRU_CL)
                dst = slice(j * LRU_CL, (j + 1) * LRU_CL)
                y = u_scr[lt, src, :] + a_scr[lt, src, :] * carry
                if reverse:
                    o_ref[0, dst, lanes] = (yacc[dst, lanes] + y).astype(o_ref.dtype)
                else:
                    yacc[dst, lanes] = y
                carry = h_end[j:j + 1, :] + p_end[j:j + 1, :] * carry


def rglru_block(p, conv_w, conv_b, w_gates, b_gates, lam):
    npair = LRU_WP // LRU_GW
    gw = LRU_GW
    big = lambda: pltpu.VMEM((SEQ, gw), F32)
    slabs = lambda: pltpu.VMEM((gw // 128, LRU_CHUNKS * LRU_PITCH, 128), F32)
    return pl.pallas_call(
        _lru_kernel,
        grid=(BATCH, npair),
        in_specs=[
            pl.BlockSpec((1, SEQ, gw), lambda b, n: (b, 0, n + npair)),
            pl.BlockSpec((LRU_CONV, gw), lambda b, n: (0, n)),
            pl.BlockSpec((1, gw), lambda b, n: (0, n)),
            pl.BlockSpec((2, 1, gw, 2 * gw), lambda b, n: (0, n, 0, 0)),
            pl.BlockSpec((2, 1, 1, 2 * gw), lambda b, n: (0, n, 0, 0)),
            pl.BlockSpec((2, gw), lambda b, n: (0, n)),
        ],
        out_specs=pl.BlockSpec((1, SEQ, gw), lambda b, n: (b, 0, n)),
        out_shape=jax.ShapeDtypeStruct((BATCH, SEQ, LRU_WP), BF16),
        scratch_shapes=[big(), slabs(), slabs(), big()],
        compiler_params=pltpu.CompilerParams(dimension_semantics=("parallel", "parallel"),
                                             vmem_limit_bytes=BIG_VMEM_LIMIT),
        name="rglru_block",
    )(p, conv_w, conv_b, w_gates, b_gates, lam)


def _gated_mm_residual_kernel(y_ref, gate_ref, w_ref, x_ref, g_ref, o_ref, a_scr):
    @pl.when(pl.program_id(2) == 0)
    def _():
        a_scr[...] = (_gelu_tanh(gate_ref[0]) * y_ref[0].astype(F32)).astype(BF16)

    o_ref[0] = x_ref[0] + g_ref[0] * jnp.dot(a_scr[...], w_ref[...], preferred_element_type=F32)


def lru_out_project(y, p, w, x, gate, tm=512, tn=1024):
    return pl.pallas_call(
        _gated_mm_residual_kernel,
        grid=(BATCH, SEQ // tm, D_MODEL // tn),
        in_specs=[
            pl.BlockSpec((1, tm, LRU_WP), lambda b, i, n: (b, i, 0)),
            pl.BlockSpec((1, tm, LRU_WP), lambda b, i, n: (b, i, 0)),
            pl.BlockSpec((LRU_WP, tn), lambda b, i, n: (0, n)),
            pl.BlockSpec((1, tm, tn), lambda b, i, n: (b, i, n)),
            pl.BlockSpec((1, 1, tn), lambda b, i, n: (b, 0, n)),
        ],
        out_specs=pl.BlockSpec((1, tm, tn), lambda b, i, n: (b, i, n)),
        out_shape=jax.ShapeDtypeStruct((BATCH, SEQ, D_MODEL), F32),
        scratch_shapes=[pltpu.VMEM((tm, LRU_WP), BF16)],
        compiler_params=_cparams(("parallel", "parallel", "arbitrary")),
        name="lru_out_project",
    )(y, p, w, x, gate)


def _pad_blocks(a):
    lead = a.shape[:-1]
    a = a.reshape(*lead, LRU_BLOCKS, LRU_BW)
    a = jnp.pad(a, [(0, 0)] * len(lead) + [(0, 0), (0, LRU_PW - LRU_BW)])
    return a.reshape(*lead, LRU_WP)


def _pair_block_diagonal(w):
    pad = LRU_PW - LRU_BW
    w = jnp.pad(w, ((0, 0), (0, 0), (0, pad), (0, pad))).reshape(2, LRU_BLOCKS // 2, 2, LRU_PW, LRU_PW)
    zero = jnp.zeros_like(w[:, :, 0])
    top = jnp.concatenate([w[:, :, 0], zero], axis=-1)
    bottom = jnp.concatenate([zero, w[:, :, 1]], axis=-1)
    return jnp.concatenate([top, bottom], axis=-2)


def lru_mixer_layer(x, c, norm_g, ada_w, ada_b, w_in, conv_w, conv_b, w_a, b_a, w_x, b_x, lam, w_out):
    shift, scale, gate = ada_modulation(c, ada_w, ada_b)
    w_in_p = jnp.concatenate([_pad_blocks(w_in[:, :LRU_WIDTH]), _pad_blocks(w_in[:, LRU_WIDTH:])], axis=1)
    p = norm_project(x, norm_g, scale, shift, w_in_p.astype(BF16), F32)
    npair = LRU_WP // LRU_GW
    w_gates = jnp.concatenate([_pair_block_diagonal(w_a), _pair_block_diagonal(w_x)], axis=-1)
    b_gates = jnp.concatenate([_pad_blocks(b_a).reshape(2, npair, 1, LRU_GW),
                               _pad_blocks(b_x).reshape(2, npair, 1, LRU_GW)], axis=-1)
    y = rglru_block(p, _pad_blocks(conv_w), _pad_blocks(conv_b).reshape(1, LRU_WP),
                    (0.5 * w_gates).astype(BF16), 0.5 * b_gates, _pad_blocks(lam))
    w_out_p = jnp.pad(w_out.reshape(LRU_BLOCKS, LRU_BW, D_MODEL), ((0, 0), (0, LRU_PW - LRU_BW), (0, 0)))
    return lru_out_project(y, p, w_out_p.reshape(LRU_WP, D_MODEL).astype(BF16), x, gate)


def _router_kernel(x_ref, g_ref, sc_ref, sh_ref, wr_ref, h_ref, eid_ref, gt_ref):
    h = _modulated_norm(x_ref[0], g_ref[...], sc_ref[0], sh_ref[0])
    hb = h.astype(BF16)
    h_ref[0] = h
    h_lo = (h - hb.astype(F32)).astype(BF16)
    w = wr_ref[...]
    w_hi = w.astype(BF16)
    w_lo = (w - w_hi.astype(F32)).astype(BF16)
    lg = (jnp.dot(hb, w_hi, preferred_element_type=F32) + jnp.dot(h_lo, w_hi, preferred_element_type=F32)
          + jnp.dot(hb, w_lo, preferred_element_type=F32))
    lt = lg.T
    tm = lt.shape[1]
    row = lax.broadcasted_iota(jnp.int32, (8, tm), 0)
    big = jnp.int32(99)
    gl = jnp.where(row < N_GROUPS, lt[0:8], -jnp.inf)
    g_max = jnp.max(gl, axis=0, keepdims=True)
    g_sel = jnp.min(jnp.where(gl == g_max, row, big), axis=0, keepdims=True)
    g_prob = 1.0 / jnp.sum(jnp.exp(gl - g_max), axis=0, keepdims=True)
    el = jnp.zeros((8, tm), F32)
    for g in range(N_GROUPS):
        el = jnp.where(g_sel == g, lt[8 + 8 * g:16 + 8 * g], el)
    v1 = jnp.max(el, axis=0, keepdims=True)
    i1 = jnp.min(jnp.where(el == v1, row, big), axis=0, keepdims=True)
    el2 = jnp.where(row == i1, -jnp.inf, el)
    v2 = jnp.max(el2, axis=0, keepdims=True)
    i2 = jnp.min(jnp.where(el2 == v2, row, big), axis=0, keepdims=True)
    e2 = jnp.exp(v2 - v1)
    p1 = 1.0 / (1.0 + e2)
    p2 = e2 * p1
    eid_ref[0] = jnp.where(row == 0, g_sel * EXPERTS_PER_GROUP + i1,
                           jnp.where(row == 1, g_sel * EXPERTS_PER_GROUP + i2, 0))
    rows = lax.broadcasted_iota(jnp.int32, (ROUTER_LANES, tm), 0)
    gates = jnp.where(rows == 0, g_prob * p1, jnp.where(rows == 1, g_prob * p2, 0.0))
    gt_ref[0] = gates.T


def route(x, g, scale, shift, w_router, tm=512):
    return pl.pallas_call(
        _router_kernel,
        grid=(BATCH, SEQ // tm),
        in_specs=[
            pl.BlockSpec((1, tm, D_MODEL), lambda b, i: (b, i, 0)),
            pl.BlockSpec((1, D_MODEL), lambda b, i: (0, 0)),
            pl.BlockSpec((1, 1, D_MODEL), lambda b, i: (b, 0, 0)),
            pl.BlockSpec((1, 1, D_MODEL), lambda b, i: (b, 0, 0)),
            pl.BlockSpec((D_MODEL, ROUTER_LANES), lambda b, i: (0, 0)),
        ],
        out_specs=[
            pl.BlockSpec((1, tm, D_MODEL), lambda b, i: (b, i, 0)),
            pl.BlockSpec((1, 8, tm), lambda b, i: (b, 0, i)),
            pl.BlockSpec((1, tm, ROUTER_LANES), lambda b, i: (b, i, 0)),
        ],
        out_shape=[
            jax.ShapeDtypeStruct((BATCH, SEQ, D_MODEL), F32),
            jax.ShapeDtypeStruct((BATCH, 8, SEQ), jnp.int32),
            jax.ShapeDtypeStruct((BATCH, SEQ, ROUTER_LANES), F32),
        ],
        compiler_params=_cparams(("parallel", "parallel")),
        name="route",
    )(x, g.reshape(1, D_MODEL), scale, shift, w_router)


def _expert_kernel(be_ref, nu_ref, first_ref, nxt_ref, x_ref, wgu_hbm, wd_hbm, o_ref,
                   gu_stage, d_stage, wgu, wd, sem):
    i = pl.program_id(0)
    active = i < nu_ref[0]

    def weight_copies(e):
        return (pltpu.make_async_copy(wgu_hbm.at[e], gu_stage, sem.at[0]),
                pltpu.make_async_copy(wd_hbm.at[e], d_stage, sem.at[1]))

    @pl.when(i == 0)
    def _():
        for cp in weight_copies(be_ref[0]):
            cp.start()

    @pl.when(active & (first_ref[i] == 1))
    def _():
        for cp in weight_copies(be_ref[i]):
            cp.wait()

        def cast_rows(stage, dst):
            def body(c, carry):
                r0 = pl.multiple_of(c * MOE_CAST_ROWS, MOE_CAST_ROWS)
                dst[pl.ds(r0, MOE_CAST_ROWS), :] = stage[pl.ds(r0, MOE_CAST_ROWS), :].astype(BF16)
                return carry
            lax.fori_loop(0, stage.shape[0] // MOE_CAST_ROWS, body, 0)

        cast_rows(gu_stage, wgu)
        cast_rows(d_stage, wd)

        @pl.when(nxt_ref[i] >= 0)
        def _():
            for cp in weight_copies(nxt_ref[i]):
                cp.start()

    @pl.when(active)
    def _():
        gu = jnp.dot(x_ref[...].astype(BF16), wgu[...], preferred_element_type=F32)
        g = gu[:, :EXPERT_FF]
        u = gu[:, EXPERT_FF:]
        act = (g * _sigmoid(g) * u).astype(BF16)
        o_ref[...] = jnp.dot(act, wd[...], preferred_element_type=F32)

    @pl.when(jnp.logical_not(active))
    def _():
        o_ref[...] = jnp.zeros(o_ref.shape, F32)


def expert_blocks(xs, blk_expert, n_used, first, nxt, w_gate_up, w_down):
    row_map = lambda i, be, nu, first, nxt: (jnp.minimum(i, nu[0] - 1), 0)
    return pl.pallas_call(
        _expert_kernel,
        grid_spec=pltpu.PrefetchScalarGridSpec(
            num_scalar_prefetch=4,
            grid=(MOE_NBLK,),
            in_specs=[
                pl.BlockSpec((MOE_TB, D_MODEL), row_map),
                pl.BlockSpec(memory_space=pl.ANY),
                pl.BlockSpec(memory_space=pl.ANY),
            ],
            out_specs=pl.BlockSpec((MOE_TB, D_MODEL), lambda i, be, nu, first, nxt: (i, 0)),
            scratch_shapes=[
                pltpu.VMEM((D_MODEL, 2 * EXPERT_FF), F32),
                pltpu.VMEM((EXPERT_FF, D_MODEL), F32),
                pltpu.VMEM((D_MODEL, 2 * EXPERT_FF), BF16),
                pltpu.VMEM((EXPERT_FF, D_MODEL), BF16),
                pltpu.SemaphoreType.DMA((2,)),
            ],
        ),
        out_shape=jax.ShapeDtypeStruct((MOE_ROWS, D_MODEL), F32),
        compiler_params=pltpu.CompilerParams(dimension_semantics=("arbitrary",), vmem_limit_bytes=BIG_VMEM_LIMIT),
        name="expert_blocks",
    )(blk_expert, n_used, first, nxt, xs, w_gate_up, w_down)


def _router_weights(w_group, w_expert):
    w = jnp.zeros((D_MODEL, ROUTER_LANES), F32)
    w = w.at[:, 0:N_GROUPS].set(w_group)
    return w.at[:, 8:8 + N_EXPERTS].set(w_expert)


def _route_meta_kernel(e_ref, dest_ref, be_ref, first_ref, nxt_ref, nu_ref):
    nrow = e_ref.shape[0]
    expert = lax.broadcasted_iota(jnp.int32, (N_EXPERTS, META_LANES), 0)
    upto = (lax.broadcasted_iota(jnp.int32, (META_LANES, META_LANES), 0)
            <= lax.broadcasted_iota(jnp.int32, (META_LANES, META_LANES), 1)).astype(BF16)

    def count_row(c, acc):
        return acc + jnp.where(expert == e_ref[c], 1.0, 0.0)

    acc = lax.fori_loop(0, nrow, count_row, jnp.zeros((N_EXPERTS, META_LANES), F32))
    counts = jnp.sum(acc, axis=1, keepdims=True)
    pcounts = jnp.floor((counts + (MOE_TB - 1.0)) * (1.0 / MOE_TB)) * MOE_TB
    ends = []
    run = jnp.zeros((1, 1), F32)
    for e in range(N_EXPERTS):
        run = run + pcounts[e:e + 1, :]
        ends.append(run)
    pend = jnp.concatenate(ends, axis=0)
    pstart = pend - pcounts

    def dest_row(c, running):
        hit = expert == e_ref[c]
        seen = jnp.dot(jnp.where(hit, 1.0, 0.0).astype(BF16), upto, preferred_element_type=F32)
        slot = seen - 1.0 + (running + pstart)
        dest_ref[c] = jnp.sum(jnp.where(hit, slot, 0.0), axis=0, keepdims=True).astype(jnp.int32)
        return running + seen[:, META_LANES - 1:META_LANES]

    lax.fori_loop(0, nrow, dest_row, jnp.zeros((N_EXPERTS, 1), F32))

    blk = lax.broadcasted_iota(jnp.int32, (1, META_LANES), 1).astype(F32)
    n_used = pend[N_EXPERTS - 1:N_EXPERTS, :] * (1.0 / MOE_TB)

    def expert_at(b):
        start = jnp.minimum(b, n_used - 1.0) * MOE_TB
        return jnp.minimum(jnp.sum(jnp.where(pend <= start, 1.0, 0.0), axis=0, keepdims=True), N_EXPERTS - 1.0)

    be = expert_at(blk)
    is_first = jnp.logical_and(jnp.logical_or(blk == 0.0, be != expert_at(blk - 1.0)), blk < n_used)
    later = jnp.logical_and(expert.astype(F32) > be, counts > 0.0)
    nxt = jnp.min(jnp.where(later, expert.astype(F32), 2.0 * N_EXPERTS), axis=0, keepdims=True)
    be_ref[...] = be.astype(jnp.int32)
    first_ref[...] = jnp.where(is_first, 1, 0)
    nxt_ref[...] = jnp.where(nxt < N_EXPERTS, nxt, -1.0).astype(jnp.int32)
    nu_ref[...] = jnp.broadcast_to(n_used, (1, META_LANES)).astype(jnp.int32)


def route_metadata(e_flat):
    nrow = e_flat.shape[0] // META_LANES
    lane_row = jax.ShapeDtypeStruct((1, META_LANES), jnp.int32)
    dest, be, first, nxt, nu = pl.pallas_call(
        _route_meta_kernel,
        out_shape=[jax.ShapeDtypeStruct((nrow, 1, META_LANES), jnp.int32), lane_row, lane_row, lane_row, lane_row],
        compiler_params=pltpu.CompilerParams(vmem_limit_bytes=VMEM_LIMIT),
        name="route_metadata",
    )(e_flat.reshape(nrow, 1, META_LANES))
    return dest.reshape(-1), be[0, :MOE_NBLK], first[0, :MOE_NBLK], nxt[0, :MOE_NBLK], nu[0, :1]


def sc_move_rows(src, idx, n_out, scatter):
    n_idx = idx.shape[0]
    n_src, width = src.shape
    n_workers = SC_CORES * SC_SUBCORES
    per_w = n_idx // n_workers
    nchunk = per_w // SC_CHUNK
    assert per_w * n_workers == n_idx and nchunk * SC_CHUNK == per_w and nchunk % 2 == 0
    assert per_w % n_src == 0 or n_src % per_w == 0
    mesh = plsc.VectorSubcoreMesh(core_axis_name="c", subcore_axis_name="s")

    @functools.partial(
        pl.kernel, mesh=mesh,
        out_type=jax.ShapeDtypeStruct((n_out, width), src.dtype),
        scratch_types=[
            pltpu.VMEM((nchunk, SC_CHUNK), jnp.int32),
            pltpu.VMEM((2, SC_CHUNK, width), src.dtype),
            pltpu.SemaphoreType.DMA((2,)),
            pltpu.SemaphoreType.DMA((2,)),
        ],
    )
    def move(src_hbm, idx_hbm, out_hbm, idx_v, rows_v, in_sem, out_sem):
        wid = lax.axis_index("s") * SC_CORES + lax.axis_index("c")
        base = wid * per_w
        pltpu.sync_copy(idx_hbm.at[wid], idx_v)

        def load(c, b):
            if scatter:
                rows = src_hbm.at[pl.ds(lax.rem(base, n_src) + c * SC_CHUNK, SC_CHUNK)]
            else:
                rows = src_hbm.at[idx_v.at[c]]
            return pltpu.make_async_copy(rows, rows_v.at[b], in_sem.at[b])

        def store(c, b):
            if scatter:
                rows = out_hbm.at[idx_v.at[c]]
            else:
                rows = out_hbm.at[pl.ds(base + c * SC_CHUNK, SC_CHUNK)]
            return pltpu.make_async_copy(rows_v.at[b], rows, out_sem.at[b])

        load(0, 0).start()

        @pl.loop(0, nchunk, step=2)
        def _(c0):
            for b in (0, 1):
                c = c0 + b
                load(c, b).wait()

                @pl.when(c + 1 < nchunk)
                def _():
                    @pl.when(c >= 1)
                    def _():
                        store(c - 1, 1 - b).wait()

                    load(c + 1, 1 - b).start()

                store(c, b).start()

        store(nchunk - 2, 0).wait()
        store(nchunk - 1, 1).wait()

    return move(src, idx.reshape(n_workers, nchunk, SC_CHUNK))


def _combine_kernel(x_ref, z0_ref, z1_ref, gates_ref, ada_ref, ng_ref, o_ref, *, final):
    gates = gates_ref[0]
    y = gates[:, 0:1] * z0_ref[...] + gates[:, 1:2] * z1_ref[...]
    x = x_ref[0] + ada_ref[0] * y
    if final:
        x = x * lax.rsqrt(jnp.mean(x * x, axis=-1, keepdims=True) + EPS) * ng_ref[...]
    o_ref[0] = x


def moe_combine(x, z, gates, gate_ada, norm_g, tm=512):
    final = norm_g is not None
    if not final:
        norm_g = jnp.ones((D_MODEL,), F32)
    nt = SEQ // tm
    return pl.pallas_call(
        functools.partial(_combine_kernel, final=final),
        grid=(BATCH, nt),
        in_specs=[
            pl.BlockSpec((1, tm, D_MODEL), lambda b, i: (b, i, 0)),
            pl.BlockSpec((tm, D_MODEL), lambda b, i: (b * nt + i, 0)),
            pl.BlockSpec((tm, D_MODEL), lambda b, i: (BATCH * nt + b * nt + i, 0)),
            pl.BlockSpec((1, tm, ROUTER_LANES), lambda b, i: (b, i, 0)),
            pl.BlockSpec((1, 1, D_MODEL), lambda b, i: (b, 0, 0)),
            pl.BlockSpec((1, D_MODEL), lambda b, i: (0, 0)),
        ],
        out_specs=pl.BlockSpec((1, tm, D_MODEL), lambda b, i: (b, i, 0)),
        out_shape=jax.ShapeDtypeStruct((BATCH, SEQ, D_MODEL), F32),
        compiler_params=_cparams(("parallel", "parallel")),
        name="moe_combine",
    )(x, z, z, gates, gate_ada, norm_g.reshape(1, D_MODEL))


def hierarchical_moe(x, c, norm_g, ada_w, ada_b, w_group, w_expert, w_gate_up, w_down, final_norm_g=None):
    shift, scale, gate_ada = ada_modulation(c, ada_w, ada_b)
    h, eid, gates = route(x, norm_g, scale, shift, _router_weights(w_group, w_expert))
    e_flat = jnp.concatenate([eid[:, 0, :].reshape(N_TOK), eid[:, 1, :].reshape(N_TOK)])
    dest, blk_expert, first, nxt, n_used = route_metadata(e_flat)
    xs = sc_move_rows(h.reshape(N_TOK, D_MODEL), dest, MOE_ROWS, scatter=True)
    yb = expert_blocks(xs, blk_expert, n_used, first, nxt, w_gate_up, w_down)
    z = sc_move_rows(yb, dest, TOP_K * N_TOK, scatter=False)
    return moe_combine(x, z, gates, gate_ada, final_norm_g)


def kernel(x, c, norm0_mix, ada0_mix_w, ada0_mix_b, w_in0, conv_w, conv_b, conv_norm_g, conv_norm_b, w_out0, norm0_ffn, ada0_ffn_w, ada0_ffn_b, moe0_w_group, moe0_w_expert, moe0_w_gate_up, moe0_w_down, norm1_mix, ada1_mix_w, ada1_mix_b, w_in1, lru_conv_w, lru_conv_b, lru_w_a, lru_b_a, lru_w_x, lru_b_x, lru_lambda, w_out1, norm1_ffn, ada1_ffn_w, ada1_ffn_b, moe1_w_group, moe1_w_expert, moe1_w_gate_up, moe1_w_down, norm_final):
    shift, scale, gate = ada_modulation(c, ada0_mix_w, ada0_mix_b)
    w_in0 = w_in0.astype(BF16)
    p_conv = norm_project(x, norm0_mix, scale, shift, w_in0[:, :2 * CONV_CH], F32)
    qkv = norm_project(x, norm0_mix, scale, shift, w_in0[:, 2 * CONV_CH:], BF16, head_major=True)
    a = conv_module(p_conv, conv_w, conv_b, conv_norm_g, conv_norm_b)
    outs, stats = zip(*(dilated_attention_branch(qkv, dilation) for _, dilation in ATT_BRANCHES))
    o = merge_branches(outs, stats)
    w_out0 = w_out0.astype(BF16)
    x = project_residual([a, o], [w_out0[:CONV_CH], w_out0[CONV_CH:]], x, gate)
    x = hierarchical_moe(x, c, norm0_ffn, ada0_ffn_w, ada0_ffn_b, moe0_w_group, moe0_w_expert,
                         moe0_w_gate_up, moe0_w_down)

    x = lru_mixer_layer(x, c, norm1_mix, ada1_mix_w, ada1_mix_b, w_in1, lru_conv_w, lru_conv_b, lru_w_a, lru_b_a,
                        lru_w_x, lru_b_x, lru_lambda, w_out1)
    x = hierarchical_moe(x, c, norm1_ffn, ada1_ffn_w, ada1_ffn_b, moe1_w_group, moe1_w_expert,
                         moe1_w_gate_up, moe1_w_down, final_norm_g=norm_final)
    return x
```

```python
import functools

import jax
import jax.numpy as jnp
from jax import lax
from jax.experimental import pallas as pl
from jax.experimental.pallas import tpu as pltpu
from jax.experimental.pallas import tpu_sc as plsc

F32 = jnp.float32
BF16 = jnp.bfloat16

D_MODEL = 2048
BATCH = 4
SEQ = 4096
N_TOK = BATCH * SEQ
EPS = 1e-6
NEG_INF = -1e30

CONV_CH = 1024
CONV_GROUPS = 8
CONV_WIDTH = 31
CONV_HALO = 16

ATT_HEADS = 8
HEAD_DIM = 128
ATT_W = ATT_HEADS * HEAD_DIM
ATT_BRANCHES = ((128, 1), (512, 4), (2048, 16))
ATT_RADIUS = 64
ATT_QB = 128
ATT_KB = ATT_QB + 2 * ATT_RADIUS
ATT_COPY_ROWS = 512
ALIBI_MAX = 8.0

LRU_WIDTH = 2688
LRU_BLOCKS = 16
LRU_BW = LRU_WIDTH // LRU_BLOCKS
LRU_PW = 192
LRU_WP = LRU_BLOCKS * LRU_PW
LRU_GW = 2 * LRU_PW
LRU_CONV = 4
LRU_C = 8.0
LRU_CHUNKS = 8
LRU_CL = SEQ // LRU_CHUNKS
LRU_PITCH = LRU_CL + 8
LOG2E = 1.4426950408889634

N_GROUPS = 4
EXPERTS_PER_GROUP = 8
N_EXPERTS = 32
TOP_K = 2
EXPERT_FF = 1024
MOE_TB = 256
MOE_ROWS = N_TOK * TOP_K + N_EXPERTS * MOE_TB
MOE_NBLK = MOE_ROWS // MOE_TB
MOE_CAST_ROWS = 256
META_LANES = 256
assert MOE_NBLK <= META_LANES
SC_CORES = 2
SC_SUBCORES = 16
SC_CHUNK = 16
ROUTER_LANES = 128

MM_ROWS = 1024
VMEM_LIMIT = 48 * 1024 * 1024
BIG_VMEM_LIMIT = 56 * 1024 * 1024


def _cparams(sem):
    return pltpu.CompilerParams(dimension_semantics=sem, vmem_limit_bytes=VMEM_LIMIT)


def _sigmoid(x):
    return 0.5 * jnp.tanh(0.5 * x) + 0.5


def _ada_kernel(ct_ref, w_ref, b_ref, o_ref):
    ct = ct_ref[...]
    st = ct * _sigmoid(ct)
    w = w_ref[...]
    rows = [jnp.sum(w * st[:, b:b + 1], axis=0, keepdims=True) for b in range(BATCH)]
    o_ref[...] = jnp.concatenate(rows, axis=0) + b_ref[...]


def ada_modulation(c, w, b):
    tn = 512
    n = w.shape[1]
    mod = pl.pallas_call(
        _ada_kernel,
        grid=(n // tn,),
        in_specs=[
            pl.BlockSpec((D_MODEL, BATCH), lambda j: (0, 0)),
            pl.BlockSpec((D_MODEL, tn), lambda j: (0, j)),
            pl.BlockSpec((1, tn), lambda j: (0, j)),
        ],
        out_specs=pl.BlockSpec((BATCH, tn), lambda j: (0, j)),
        out_shape=jax.ShapeDtypeStruct((BATCH, n), F32),
        compiler_params=_cparams(("parallel",)),
        name="ada_modulation",
    )(c.T, w, b.reshape(1, n))
    shift, scale, gate = jnp.split(mod, 3, axis=-1)
    return shift[:, None, :], scale[:, None, :], gate[:, None, :]


def _modulated_norm(x, g, scale, shift):
    y = x * lax.rsqrt(jnp.mean(x * x, axis=-1, keepdims=True) + EPS)
    return (y * g) * (1.0 + scale) + shift


def _norm_kernel(x_ref, g_ref, sc_ref, sh_ref, o_ref):
    o_ref[0] = _modulated_norm(x_ref[0], g_ref[...], sc_ref[0], sh_ref[0]).astype(o_ref.dtype)


def modulated_norm(x, g, scale, shift, tm=512):
    return pl.pallas_call(
        _norm_kernel,
        grid=(BATCH, SEQ // tm),
        in_specs=[
            pl.BlockSpec((1, tm, D_MODEL), lambda b, i: (b, i, 0)),
            pl.BlockSpec((1, D_MODEL), lambda b, i: (0, 0)),
            pl.BlockSpec((1, 1, D_MODEL), lambda b, i: (b, 0, 0)),
            pl.BlockSpec((1, 1, D_MODEL), lambda b, i: (b, 0, 0)),
        ],
        out_specs=pl.BlockSpec((1, tm, D_MODEL), lambda b, i: (b, i, 0)),
        out_shape=jax.ShapeDtypeStruct((BATCH, SEQ, D_MODEL), BF16),
        compiler_params=_cparams(("parallel", "parallel")),
        name="modulated_norm",
    )(x, g.reshape(1, D_MODEL), scale, shift)


def _mm_kernel(a_ref, w_ref, o_ref, *, head_major):
    res = jnp.dot(a_ref[0], w_ref[...], preferred_element_type=F32)
    if head_major:
        for hh in range(ATT_HEADS):
            o_ref[0, 0, hh] = res[:, hh * HEAD_DIM:(hh + 1) * HEAD_DIM].astype(o_ref.dtype)
    else:
        o_ref[0] = res.astype(o_ref.dtype)


def project(a, w, out_dtype, head_major=False, tm=MM_ROWS, tn=1024):
    k, n = w.shape
    if head_major:
        assert tn == ATT_W
        out_shape = jax.ShapeDtypeStruct((n // tn, BATCH, ATT_HEADS, SEQ, HEAD_DIM), out_dtype)
        out_spec = pl.BlockSpec((1, 1, ATT_HEADS, tm, HEAD_DIM), lambda b, i, j: (j, b, 0, i, 0))
    else:
        out_shape = jax.ShapeDtypeStruct((BATCH, SEQ, n), out_dtype)
        out_spec = pl.BlockSpec((1, tm, tn), lambda b, i, j: (b, i, j))
    return pl.pallas_call(
        functools.partial(_mm_kernel, head_major=head_major),
        grid=(BATCH, SEQ // tm, n // tn),
        in_specs=[
            pl.BlockSpec((1, tm, k), lambda b, i, j: (b, i, 0)),
            pl.BlockSpec((k, tn), lambda b, i, j: (0, j)),
        ],
        out_specs=out_spec,
        out_shape=out_shape,
        compiler_params=_cparams(("parallel", "parallel", "parallel")),
        name="project",
    )(a, w)


def _mm_residual_kernel(*refs, n_in):
    a_refs, w_refs = refs[:n_in], refs[n_in:2 * n_in]
    x_ref, gate_ref, o_ref = refs[2 * n_in:]
    acc = jnp.dot(a_refs[0][0], w_refs[0][...], preferred_element_type=F32)
    for a_ref, w_ref in zip(a_refs[1:], w_refs[1:]):
        acc = acc + jnp.dot(a_ref[0], w_ref[...], preferred_element_type=F32)
    o_ref[0] = x_ref[0] + gate_ref[0] * acc


def project_residual(a_list, w_list, x, gate, tm=MM_ROWS, tn=1024):
    n_in = len(a_list)
    grid = (BATCH, SEQ // tm, D_MODEL // tn)
    in_specs = [pl.BlockSpec((1, tm, a.shape[-1]), lambda b, i, j: (b, i, 0)) for a in a_list]
    in_specs += [pl.BlockSpec((w.shape[0], tn), lambda b, i, j: (0, j)) for w in w_list]
    in_specs += [
        pl.BlockSpec((1, tm, tn), lambda b, i, j: (b, i, j)),
        pl.BlockSpec((1, 1, tn), lambda b, i, j: (b, 0, j)),
    ]
    return pl.pallas_call(
        functools.partial(_mm_residual_kernel, n_in=n_in),
        grid=grid,
        in_specs=in_specs,
        out_specs=pl.BlockSpec((1, tm, tn), lambda b, i, j: (b, i, j)),
        out_shape=jax.ShapeDtypeStruct((BATCH, SEQ, D_MODEL), F32),
        compiler_params=_cparams(("parallel", "parallel", "parallel")),
        name="project_residual",
    )(*a_list, *w_list, x, gate)


def _conv_kernel(v_ref, g_ref, cw_ref, cb_ref, ng_ref, nb_ref, o_ref, pad_scr, *, ts):
    zeros = jnp.zeros((CONV_HALO, 128), F32)
    pad_scr[0:CONV_HALO, :] = zeros
    pad_scr[SEQ + CONV_HALO:SEQ + 2 * CONV_HALO, :] = zeros
    for t0 in range(0, SEQ, ts):
        pad_scr[CONV_HALO + t0:CONV_HALO + t0 + ts, :] = v_ref[0, t0:t0 + ts, :] * _sigmoid(g_ref[0, t0:t0 + ts, :])
    first = CONV_HALO - CONV_WIDTH // 2
    for t0 in range(0, SEQ, ts):
        acc = cw_ref[0:1, :] * pad_scr[first + t0:first + t0 + ts, :] + cb_ref[...]
        for k in range(1, CONV_WIDTH):
            acc = acc + cw_ref[k:k + 1, :] * pad_scr[first + t0 + k:first + t0 + k + ts, :]
        mu = jnp.mean(acc, axis=-1, keepdims=True)
        cen = acc - mu
        var = jnp.mean(cen * cen, axis=-1, keepdims=True)
        y = cen * lax.rsqrt(var + EPS) * ng_ref[...] + nb_ref[...]
        o_ref[0, t0:t0 + ts, :] = (y * _sigmoid(y)).astype(o_ref.dtype)


def conv_module(p, conv_w, conv_b, norm_g, norm_b, ts=128):
    ng = CONV_GROUPS
    vec = lambda a: a.reshape(1, CONV_CH)
    vspec = pl.BlockSpec((1, 128), lambda b, c: (0, c))
    return pl.pallas_call(
        functools.partial(_conv_kernel, ts=ts),
        grid=(BATCH, ng),
        in_specs=[
            pl.BlockSpec((1, SEQ, 128), lambda b, c: (b, 0, c)),
            pl.BlockSpec((1, SEQ, 128), lambda b, c: (b, 0, c + ng)),
            pl.BlockSpec((CONV_WIDTH, 128), lambda b, c: (0, c)),
            vspec, vspec, vspec,
        ],
        out_specs=pl.BlockSpec((1, SEQ, 128), lambda b, c: (b, 0, c)),
        out_shape=jax.ShapeDtypeStruct((BATCH, SEQ, CONV_CH), BF16),
        scratch_shapes=[pltpu.VMEM((SEQ + 2 * CONV_HALO, 128), F32)],
        compiler_params=_cparams(("parallel", "parallel")),
        name="conv_module",
    )(p, p, conv_w, vec(conv_b), vec(norm_g), vec(norm_b))


def _attn_kernel(q_ref, k_ref, v_ref, *rest):
    nbr = len(ATT_BRANCHES)
    bias_refs, o_ref = rest[:nbr], rest[nbr]
    qf, kf, vf, qd, kpad, vpad, acc, mx, den = rest[nbr + 1:]
    scale = HEAD_DIM ** -0.5
    zeros = jnp.zeros((ATT_RADIUS, HEAD_DIM), BF16)
    for t0 in range(0, SEQ, ATT_COPY_ROWS):
        rows = slice(t0, t0 + ATT_COPY_ROWS)
        qf[rows, :] = q_ref[0, 0, rows, :].astype(F32)
        kf[rows, :] = k_ref[0, 0, rows, :].astype(F32)
        vf[rows, :] = v_ref[0, 0, rows, :].astype(F32)

    order = sorted(range(nbr), key=lambda n: -ATT_BRANCHES[n][1])
    for pos, n in enumerate(order):
        dil = ATT_BRANCHES[n][1]
        bias_ref = bias_refs[n]
        sub = SEQ // dil
        nblk = sub // ATT_QB
        first, last = pos == 0, pos == nbr - 1
        assert dil == 1 or not last

        def class_rows(r, start, count, dil=dil):
            if dil == 1:
                return pl.ds(start, count)
            return pl.ds(r + start * dil, count, stride=dil)

        def residue_class(r, carry, dil=dil, sub=sub, nblk=nblk, bias_ref=bias_ref, first=first, last=last,
                          class_rows=class_rows):
            kpad[0:ATT_RADIUS, :] = zeros
            kpad[sub + ATT_RADIUS:sub + 2 * ATT_RADIUS, :] = zeros
            vpad[0:ATT_RADIUS, :] = zeros
            vpad[sub + ATT_RADIUS:sub + 2 * ATT_RADIUS, :] = zeros
            for c0 in range(0, sub, ATT_COPY_ROWS):
                n_rows = min(ATT_COPY_ROWS, sub - c0)
                src = class_rows(r, c0, n_rows)
                qd[c0:c0 + n_rows, :] = qf[src, :].astype(BF16)
                kpad[ATT_RADIUS + c0:ATT_RADIUS + c0 + n_rows, :] = kf[src, :].astype(BF16)
                vpad[ATT_RADIUS + c0:ATT_RADIUS + c0 + n_rows, :] = vf[src, :].astype(BF16)

            def block(i, carry):
                q0 = pl.multiple_of(i * ATT_QB, ATT_QB)
                s = lax.dot_general(qd[pl.ds(q0, ATT_QB), :], kpad[pl.ds(q0, ATT_KB), :],
                                    (((1,), (1,)), ((), ())), preferred_element_type=F32)
                variant = jnp.where(i == 0, 0, jnp.where(i == nblk - 1, 2, 1))
                s = s * scale + bias_ref[0, variant]
                m = jnp.max(s, axis=-1, keepdims=True)
                e = jnp.exp(s - m)
                l = jnp.sum(e, axis=-1, keepdims=True)
                o = jnp.dot(e.astype(BF16), vpad[pl.ds(q0, ATT_KB), :], preferred_element_type=F32)
                nat = class_rows(r, q0, ATT_QB)
                wide = (ATT_QB, HEAD_DIM)
                if first:
                    acc[nat, :] = o
                    mx[nat, :] = jnp.broadcast_to(m, wide)
                    den[nat, :] = jnp.broadcast_to(l, wide)
                else:
                    m_old = mx[nat, :]
                    m_new = jnp.maximum(m_old, m)
                    alpha = jnp.exp(m_old - m_new)
                    beta = jnp.exp(m - m_new)
                    total = acc[nat, :] * alpha + o * beta
                    l_new = den[nat, :] * alpha + l * beta
                    if last:
                        o_ref[0, nat, :] = (total / l_new).astype(o_ref.dtype)
                    else:
                        acc[nat, :] = total
                        mx[nat, :] = m_new
                        den[nat, :] = l_new
                return carry

            lax.fori_loop(0, nblk, block, 0, unroll=min(nblk, 4))
            return carry

        if dil == 1:
            residue_class(0, 0)
        else:
            lax.fori_loop(0, dil, residue_class, 0)


def _alibi_bias(dilation):
    row = jnp.arange(ATT_QB)[:, None]
    colm = jnp.arange(ATT_KB)[None, :]
    rel = jnp.abs(colm - ATT_RADIUS - row)
    slopes = 2.0 ** (-ALIBI_MAX * jnp.arange(1, ATT_HEADS + 1, dtype=F32) / ATT_HEADS)
    bias = -slopes[:, None, None] * (rel * dilation).astype(F32)[None]
    band = rel <= ATT_RADIUS
    masks = jnp.stack([band & (colm >= ATT_RADIUS), band, band & (colm < ATT_KB - ATT_RADIUS)])
    return jnp.where(masks[None], bias[:, None], NEG_INF)


def dilated_attention(qkv):
    for window, dilation in ATT_BRANCHES:
        assert window // (2 * dilation) == ATT_RADIUS
        assert SEQ // dilation >= 2 * ATT_QB
    qspec = lambda which: pl.BlockSpec((None, 1, 1, SEQ, HEAD_DIM), lambda b, h, which=which: (which, b, h, 0, 0))
    bias_spec = pl.BlockSpec((1, 3, ATT_QB, ATT_KB), lambda b, h: (h, 0, 0, 0))
    rows = lambda dtype, n=SEQ: pltpu.VMEM((n, HEAD_DIM), dtype)
    return pl.pallas_call(
        _attn_kernel,
        grid=(BATCH, ATT_HEADS),
        in_specs=[qspec(0), qspec(1), qspec(2)] + [bias_spec] * len(ATT_BRANCHES),
        out_specs=pl.BlockSpec((1, SEQ, HEAD_DIM), lambda b, h: (b, 0, h)),
        out_shape=jax.ShapeDtypeStruct((BATCH, SEQ, ATT_W), BF16),
        scratch_shapes=[rows(F32), rows(F32), rows(F32), rows(BF16),
                        rows(BF16, SEQ + 2 * ATT_RADIUS), rows(BF16, SEQ + 2 * ATT_RADIUS),
                        rows(F32), rows(F32), rows(F32)],
        compiler_params=_cparams(("parallel", "parallel")),
        name="dilated_attention",
    )(qkv, qkv, qkv, *[_alibi_bias(dilation) for _, dilation in ATT_BRANCHES])


def conv_attention_layer(x, c, norm_g, ada_w, ada_b, w_in, conv_w, conv_b, conv_norm_g, conv_norm_b, w_out):
    shift, scale, gate = ada_modulation(c, ada_w, ada_b)
    w_in = w_in.astype(BF16)
    h = modulated_norm(x, norm_g, scale, shift)
    p_conv = project(h, w_in[:, :2 * CONV_CH], F32)
    qkv = project(h, w_in[:, 2 * CONV_CH:], BF16, head_major=True)
    a = conv_module(p_conv, conv_w, conv_b, conv_norm_g, conv_norm_b)
    o = dilated_attention(qkv)
    w_out = w_out.astype(BF16)
    return project_residual([a, o], [w_out[:CONV_CH], w_out[CONV_CH:]], x, gate)


def _softplus(x):
    return jnp.maximum(x, 0.0) + jnp.log1p(jnp.exp(-jnp.abs(x)))


def _gelu_tanh(x):
    return 0.5 * x * (1.0 + jnp.tanh(0.7978845608028654 * (x + 0.044715 * (x * x * x))))


def _lru_kernel(x_ref, cw_ref, cb_ref, w_ref, b_ref, lam_ref, o_ref, xc, a_scr, u_scr, yacc):
    gw = LRU_GW
    ntile = gw // 128
    lead = LRU_CONV // 2

    def shifted(r0, off):
        lo, hi = r0 + off, r0 + off + LRU_CL
        parts = []
        if lo < 0:
            parts.append(jnp.zeros((-lo, gw), F32))
        parts.append(x_ref[0, max(lo, 0):min(hi, SEQ), :])
        if hi > SEQ:
            parts.append(jnp.zeros((hi - SEQ, gw), F32))
        return parts[0] if len(parts) == 1 else jnp.concatenate(parts, axis=0)

    for j in range(LRU_CHUNKS):
        r0 = j * LRU_CL
        acc = cw_ref[0:1, :] * shifted(r0, -lead) + cb_ref[...]
        for k in range(1, LRU_CONV):
            acc = acc + cw_ref[k:k + 1, :] * shifted(r0, k - lead)
        xc[r0:r0 + LRU_CL, :] = acc

    for d in range(2):
        reverse = d == 1
        half_decay = (0.5 * LRU_C) * _softplus(-lam_ref[d:d + 1, :])
        for j in range(LRU_CHUNKS):
            xj = xc[j * LRU_CL:(j + 1) * LRU_CL, :]
            pre = jnp.dot(xj.astype(BF16), w_ref[d, 0], preferred_element_type=F32) + b_ref[d, 0]
            t_r = jnp.tanh(pre[:, :gw])
            t_i = jnp.tanh(pre[:, gw:])
            neg_log_a = t_r * half_decay + half_decay
            a = jnp.exp2(neg_log_a * (-LOG2E))
            one_minus_a2 = jnp.tanh(neg_log_a) * (a * a + 1.0)
            root = one_minus_a2 * lax.rsqrt(jnp.maximum(one_minus_a2, 1e-30))
            u = root * ((0.5 * t_i + 0.5) * xj)
            for lt in range(ntile):
                rows = slice(j * LRU_PITCH, j * LRU_PITCH + LRU_CL)
                a_scr[lt, rows, :] = a[:, lt * 128:(lt + 1) * 128]
                u_scr[lt, rows, :] = u[:, lt * 128:(lt + 1) * 128]

        def step(ii, carry):
            row = (LRU_CL - 1 - ii) if reverse else ii
            out = []
            for lt in range(ntile):
                h, prod = carry[lt]
                a = a_scr[lt, pl.ds(row, LRU_CHUNKS, stride=LRU_PITCH), :]
                h = a * h + u_scr[lt, pl.ds(row, LRU_CHUNKS, stride=LRU_PITCH), :]
                prod = a * prod
                u_scr[lt, pl.ds(row, LRU_CHUNKS, stride=LRU_PITCH), :] = h
                a_scr[lt, pl.ds(row, LRU_CHUNKS, stride=LRU_PITCH), :] = prod
                out.append((h, prod))
            return tuple(out)

        init = tuple((jnp.zeros((LRU_CHUNKS, 128), F32), jnp.ones((LRU_CHUNKS, 128), F32)) for _ in range(ntile))
        lax.fori_loop(0, LRU_CL, step, init, unroll=4)

        last = 0 if reverse else LRU_CL - 1
        for lt in range(ntile):
            lanes = slice(lt * 128, (lt + 1) * 128)
            h_end = u_scr[lt, pl.ds(last, LRU_CHUNKS, stride=LRU_PITCH), :]
            p_end = a_scr[lt, pl.ds(last, LRU_CHUNKS, stride=LRU_PITCH), :]
            carry = jnp.zeros((1, 128), F32)
            for j in (range(LRU_CHUNKS - 1, -1, -1) if reverse else range(LRU_CHUNKS)):
                src = slice(j * LRU_PITCH, j * LRU_PITCH + LRU_CL)
                dst = slice(j * LRU_CL, (j + 1) * LRU_CL)
                y = u_scr[lt, src, :] + a_scr[lt, src, :] * carry
                if reverse:
                    o_ref[0, dst, lanes] = (yacc[dst, lanes] + y).astype(o_ref.dtype)
                else:
                    yacc[dst, lanes] = y
                carry = h_end[j:j + 1, :] + p_end[j:j + 1, :] * carry


def rglru_block(p, conv_w, conv_b, w_gates, b_gates, lam):
    npair = LRU_WP // LRU_GW
    gw = LRU_GW
    big = lambda: pltpu.VMEM((SEQ, gw), F32)
    slabs = lambda: pltpu.VMEM((gw // 128, LRU_CHUNKS * LRU_PITCH, 128), F32)
    return pl.pallas_call(
        _lru_kernel,
        grid=(BATCH, npair),
        in_specs=[
            pl.BlockSpec((1, SEQ, gw), lambda b, n: (b, 0, n + npair)),
            pl.BlockSpec((LRU_CONV, gw), lambda b, n: (0, n)),
            pl.BlockSpec((1, gw), lambda b, n: (0, n)),
            pl.BlockSpec((2, 1, gw, 2 * gw), lambda b, n: (0, n, 0, 0)),
            pl.BlockSpec((2, 1, 1, 2 * gw), lambda b, n: (0, n, 0, 0)),
            pl.BlockSpec((2, gw), lambda b, n: (0, n)),
        ],
        out_specs=pl.BlockSpec((1, SEQ, gw), lambda b, n: (b, 0, n)),
        out_shape=jax.ShapeDtypeStruct((BATCH, SEQ, LRU_WP), BF16),
        scratch_shapes=[big(), slabs(), slabs(), big()],
        compiler_params=pltpu.CompilerParams(dimension_semantics=("parallel", "parallel"),
                                             vmem_limit_bytes=BIG_VMEM_LIMIT),
        name="rglru_block",
    )(p, conv_w, conv_b, w_gates, b_gates, lam)


def _gelu_gate_kernel(y_ref, gate_ref, o_ref):
    o_ref[0] = (_gelu_tanh(gate_ref[0]) * y_ref[0].astype(F32)).astype(o_ref.dtype)


def gelu_gate(y, p, tm=512):
    spec = pl.BlockSpec((1, tm, LRU_WP), lambda b, i: (b, i, 0))
    return pl.pallas_call(
        _gelu_gate_kernel,
        grid=(BATCH, SEQ // tm),
        in_specs=[spec, spec],
        out_specs=spec,
        out_shape=jax.ShapeDtypeStruct((BATCH, SEQ, LRU_WP), BF16),
        compiler_params=_cparams(("parallel", "parallel")),
        name="gelu_gate",
    )(y, p)


def _pad_blocks(a):
    lead = a.shape[:-1]
    a = a.reshape(*lead, LRU_BLOCKS, LRU_BW)
    a = jnp.pad(a, [(0, 0)] * len(lead) + [(0, 0), (0, LRU_PW - LRU_BW)])
    return a.reshape(*lead, LRU_WP)


def _pair_block_diagonal(w):
    pad = LRU_PW - LRU_BW
    w = jnp.pad(w, ((0, 0), (0, 0), (0, pad), (0, pad))).reshape(2, LRU_BLOCKS // 2, 2, LRU_PW, LRU_PW)
    zero = jnp.zeros_like(w[:, :, 0])
    top = jnp.concatenate([w[:, :, 0], zero], axis=-1)
    bottom = jnp.concatenate([zero, w[:, :, 1]], axis=-1)
    return jnp.concatenate([top, bottom], axis=-2)


def lru_mixer_layer(x, c, norm_g, ada_w, ada_b, w_in, conv_w, conv_b, w_a, b_a, w_x, b_x, lam, w_out):
    shift, scale, gate = ada_modulation(c, ada_w, ada_b)
    w_in_p = jnp.concatenate([_pad_blocks(w_in[:, :LRU_WIDTH]), _pad_blocks(w_in[:, LRU_WIDTH:])], axis=1)
    p = project(modulated_norm(x, norm_g, scale, shift), w_in_p.astype(BF16), F32)
    npair = LRU_WP // LRU_GW
    w_gates = jnp.concatenate([_pair_block_diagonal(w_a), _pair_block_diagonal(w_x)], axis=-1)
    b_gates = jnp.concatenate([_pad_blocks(b_a).reshape(2, npair, 1, LRU_GW),
                               _pad_blocks(b_x).reshape(2, npair, 1, LRU_GW)], axis=-1)
    y = rglru_block(p, _pad_blocks(conv_w), _pad_blocks(conv_b).reshape(1, LRU_WP),
                    (0.5 * w_gates).astype(BF16), 0.5 * b_gates, _pad_blocks(lam))
    w_out_p = jnp.pad(w_out.reshape(LRU_BLOCKS, LRU_BW, D_MODEL), ((0, 0), (0, LRU_PW - LRU_BW), (0, 0)))
    return project_residual([gelu_gate(y, p)], [w_out_p.reshape(LRU_WP, D_MODEL).astype(BF16)], x, gate)


def _router_kernel(x_ref, g_ref, sc_ref, sh_ref, wr_ref, h_ref, eid_ref, gt_ref):
    h = _modulated_norm(x_ref[0], g_ref[...], sc_ref[0], sh_ref[0])
    hb = h.astype(BF16)
    h_ref[0] = h
    h_lo = (h - hb.astype(F32)).astype(BF16)
    w = wr_ref[...]
    w_hi = w.astype(BF16)
    w_lo = (w - w_hi.astype(F32)).astype(BF16)
    lg = (jnp.dot(hb, w_hi, preferred_element_type=F32) + jnp.dot(h_lo, w_hi, preferred_element_type=F32)
          + jnp.dot(hb, w_lo, preferred_element_type=F32))
    lt = lg.T
    tm = lt.shape[1]
    row = lax.broadcasted_iota(jnp.int32, (8, tm), 0)
    big = jnp.int32(99)
    gl = jnp.where(row < N_GROUPS, lt[0:8], -jnp.inf)
    g_max = jnp.max(gl, axis=0, keepdims=True)
    g_sel = jnp.min(jnp.where(gl == g_max, row, big), axis=0, keepdims=True)
    g_prob = 1.0 / jnp.sum(jnp.exp(gl - g_max), axis=0, keepdims=True)
    el = jnp.zeros((8, tm), F32)
    for g in range(N_GROUPS):
        el = jnp.where(g_sel == g, lt[8 + 8 * g:16 + 8 * g], el)
    v1 = jnp.max(el, axis=0, keepdims=True)
    i1 = jnp.min(jnp.where(el == v1, row, big), axis=0, keepdims=True)
    el2 = jnp.where(row == i1, -jnp.inf, el)
    v2 = jnp.max(el2, axis=0, keepdims=True)
    i2 = jnp.min(jnp.where(el2 == v2, row, big), axis=0, keepdims=True)
    e2 = jnp.exp(v2 - v1)
    p1 = 1.0 / (1.0 + e2)
    p2 = e2 * p1
    eid_ref[0] = jnp.where(row == 0, g_sel * EXPERTS_PER_GROUP + i1,
                           jnp.where(row == 1, g_sel * EXPERTS_PER_GROUP + i2, 0))
    rows = lax.broadcasted_iota(jnp.int32, (ROUTER_LANES, tm), 0)
    gates = jnp.where(rows == 0, g_prob * p1, jnp.where(rows == 1, g_prob * p2, 0.0))
    gt_ref[0] = gates.T


def route(x, g, scale, shift, w_router, tm=512):
    return pl.pallas_call(
        _router_kernel,
        grid=(BATCH, SEQ // tm),
        in_specs=[
            pl.BlockSpec((1, tm, D_MODEL), lambda b, i: (b, i, 0)),
            pl.BlockSpec((1, D_MODEL), lambda b, i: (0, 0)),
            pl.BlockSpec((1, 1, D_MODEL), lambda b, i: (b, 0, 0)),
            pl.BlockSpec((1, 1, D_MODEL), lambda b, i: (b, 0, 0)),
            pl.BlockSpec((D_MODEL, ROUTER_LANES), lambda b, i: (0, 0)),
        ],
        out_specs=[
            pl.BlockSpec((1, tm, D_MODEL), lambda b, i: (b, i, 0)),
            pl.BlockSpec((1, 8, tm), lambda b, i: (b, 0, i)),
            pl.BlockSpec((1, tm, ROUTER_LANES), lambda b, i: (b, i, 0)),
        ],
        out_shape=[
            jax.ShapeDtypeStruct((BATCH, SEQ, D_MODEL), F32),
            jax.ShapeDtypeStruct((BATCH, 8, SEQ), jnp.int32),
            jax.ShapeDtypeStruct((BATCH, SEQ, ROUTER_LANES), F32),
        ],
        compiler_params=_cparams(("parallel", "parallel")),
        name="route",
    )(x, g.reshape(1, D_MODEL), scale, shift, w_router)


def _expert_kernel(be_ref, nu_ref, first_ref, nxt_ref, x_ref, wgu_hbm, wd_hbm, o_ref,
                   gu_stage, d_stage, wgu, wd, sem):
    i = pl.program_id(0)
    active = i < nu_ref[0]

    def weight_copies(e):
        return (pltpu.make_async_copy(wgu_hbm.at[e], gu_stage, sem.at[0]),
                pltpu.make_async_copy(wd_hbm.at[e], d_stage, sem.at[1]))

    @pl.when(i == 0)
    def _():
        for cp in weight_copies(be_ref[0]):
            cp.start()

    @pl.when(active & (first_ref[i] == 1))
    def _():
        for cp in weight_copies(be_ref[i]):
            cp.wait()

        def cast_rows(stage, dst):
            def body(c, carry):
                r0 = pl.multiple_of(c * MOE_CAST_ROWS, MOE_CAST_ROWS)
                dst[pl.ds(r0, MOE_CAST_ROWS), :] = stage[pl.ds(r0, MOE_CAST_ROWS), :].astype(BF16)
                return carry
            lax.fori_loop(0, stage.shape[0] // MOE_CAST_ROWS, body, 0)

        cast_rows(gu_stage, wgu)
        cast_rows(d_stage, wd)

        @pl.when(nxt_ref[i] >= 0)
        def _():
            for cp in weight_copies(nxt_ref[i]):
                cp.start()

    @pl.when(active)
    def _():
        gu = jnp.dot(x_ref[...].astype(BF16), wgu[...], preferred_element_type=F32)
        g = gu[:, :EXPERT_FF]
        u = gu[:, EXPERT_FF:]
        act = (g * _sigmoid(g) * u).astype(BF16)
        o_ref[...] = jnp.dot(act, wd[...], preferred_element_type=F32)

    @pl.when(jnp.logical_not(active))
    def _():
        o_ref[...] = jnp.zeros(o_ref.shape, F32)


def expert_blocks(xs, blk_expert, n_used, first, nxt, w_gate_up, w_down):
    row_map = lambda i, be, nu, first, nxt: (jnp.minimum(i, nu[0] - 1), 0)
    return pl.pallas_call(
        _expert_kernel,
        grid_spec=pltpu.PrefetchScalarGridSpec(
            num_scalar_prefetch=4,
            grid=(MOE_NBLK,),
            in_specs=[
                pl.BlockSpec((MOE_TB, D_MODEL), row_map),
                pl.BlockSpec(memory_space=pl.ANY),
                pl.BlockSpec(memory_space=pl.ANY),
            ],
            out_specs=pl.BlockSpec((MOE_TB, D_MODEL), lambda i, be, nu, first, nxt: (i, 0)),
            scratch_shapes=[
                pltpu.VMEM((D_MODEL, 2 * EXPERT_FF), F32),
                pltpu.VMEM((EXPERT_FF, D_MODEL), F32),
                pltpu.VMEM((D_MODEL, 2 * EXPERT_FF), BF16),
                pltpu.VMEM((EXPERT_FF, D_MODEL), BF16),
                pltpu.SemaphoreType.DMA((2,)),
            ],
        ),
        out_shape=jax.ShapeDtypeStruct((MOE_ROWS, D_MODEL), F32),
        compiler_params=pltpu.CompilerParams(dimension_semantics=("arbitrary",), vmem_limit_bytes=BIG_VMEM_LIMIT),
        name="expert_blocks",
    )(blk_expert, n_used, first, nxt, xs, w_gate_up, w_down)


def _router_weights(w_group, w_expert):
    w = jnp.zeros((D_MODEL, ROUTER_LANES), F32)
    w = w.at[:, 0:N_GROUPS].set(w_group)
    return w.at[:, 8:8 + N_EXPERTS].set(w_expert)


def _route_meta_kernel(e_ref, dest_ref, be_ref, first_ref, nxt_ref, nu_ref):
    nrow = e_ref.shape[0]
    expert = lax.broadcasted_iota(jnp.int32, (N_EXPERTS, META_LANES), 0)
    upto = (lax.broadcasted_iota(jnp.int32, (META_LANES, META_LANES), 0)
            <= lax.broadcasted_iota(jnp.int32, (META_LANES, META_LANES), 1)).astype(BF16)

    def count_row(c, acc):
        return acc + jnp.where(expert == e_ref[c], 1.0, 0.0)

    acc = lax.fori_loop(0, nrow, count_row, jnp.zeros((N_EXPERTS, META_LANES), F32))
    counts = jnp.sum(acc, axis=1, keepdims=True)
    pcounts = jnp.floor((counts + (MOE_TB - 1.0)) * (1.0 / MOE_TB)) * MOE_TB
    ends = []
    run = jnp.zeros((1, 1), F32)
    for e in range(N_EXPERTS):
        run = run + pcounts[e:e + 1, :]
        ends.append(run)
    pend = jnp.concatenate(ends, axis=0)
    pstart = pend - pcounts

    def dest_row(c, running):
        hit = expert == e_ref[c]
        seen = jnp.dot(jnp.where(hit, 1.0, 0.0).astype(BF16), upto, preferred_element_type=F32)
        slot = seen - 1.0 + (running + pstart)
        dest_ref[c] = jnp.sum(jnp.where(hit, slot, 0.0), axis=0, keepdims=True).astype(jnp.int32)
        return running + seen[:, META_LANES - 1:META_LANES]

    lax.fori_loop(0, nrow, dest_row, jnp.zeros((N_EXPERTS, 1), F32))

    blk = lax.broadcasted_iota(jnp.int32, (1, META_LANES), 1).astype(F32)
    n_used = pend[N_EXPERTS - 1:N_EXPERTS, :] * (1.0 / MOE_TB)

    def expert_at(b):
        start = jnp.minimum(b, n_used - 1.0) * MOE_TB
        return jnp.minimum(jnp.sum(jnp.where(pend <= start, 1.0, 0.0), axis=0, keepdims=True), N_EXPERTS - 1.0)

    be = expert_at(blk)
    is_first = jnp.logical_and(jnp.logical_or(blk == 0.0, be != expert_at(blk - 1.0)), blk < n_used)
    later = jnp.logical_and(expert.astype(F32) > be, counts > 0.0)
    nxt = jnp.min(jnp.where(later, expert.astype(F32), 2.0 * N_EXPERTS), axis=0, keepdims=True)
    be_ref[...] = be.astype(jnp.int32)
    first_ref[...] = jnp.where(is_first, 1, 0)
    nxt_ref[...] = jnp.where(nxt < N_EXPERTS, nxt, -1.0).astype(jnp.int32)
    nu_ref[...] = jnp.broadcast_to(n_used, (1, META_LANES)).astype(jnp.int32)


def route_metadata(e_flat):
    nrow = e_flat.shape[0] // META_LANES
    lane_row = jax.ShapeDtypeStruct((1, META_LANES), jnp.int32)
    dest, be, first, nxt, nu = pl.pallas_call(
        _route_meta_kernel,
        out_shape=[jax.ShapeDtypeStruct((nrow, 1, META_LANES), jnp.int32), lane_row, lane_row, lane_row, lane_row],
        compiler_params=pltpu.CompilerParams(vmem_limit_bytes=VMEM_LIMIT),
        name="route_metadata",
    )(e_flat.reshape(nrow, 1, META_LANES))
    return dest.reshape(-1), be[0, :MOE_NBLK], first[0, :MOE_NBLK], nxt[0, :MOE_NBLK], nu[0, :1]


def sc_move_rows(src, idx, n_out, scatter):
    n_idx = idx.shape[0]
    n_src, width = src.shape
    n_workers = SC_CORES * SC_SUBCORES
    per_w = n_idx // n_workers
    nchunk = per_w // SC_CHUNK
    assert per_w * n_workers == n_idx and nchunk * SC_CHUNK == per_w and nchunk % 2 == 0
    assert per_w % n_src == 0 or n_src % per_w == 0
    mesh = plsc.VectorSubcoreMesh(core_axis_name="c", subcore_axis_name="s")

    @functools.partial(
        pl.kernel, mesh=mesh,
        out_type=jax.ShapeDtypeStruct((n_out, width), src.dtype),
        scratch_types=[
            pltpu.VMEM((nchunk, SC_CHUNK), jnp.int32),
            pltpu.VMEM((2, SC_CHUNK, width), src.dtype),
            pltpu.SemaphoreType.DMA((2,)),
            pltpu.SemaphoreType.DMA((2,)),
        ],
    )
    def move(src_hbm, idx_hbm, out_hbm, idx_v, rows_v, in_sem, out_sem):
        wid = lax.axis_index("s") * SC_CORES + lax.axis_index("c")
        base = wid * per_w
        pltpu.sync_copy(idx_hbm.at[wid], idx_v)

        def load(c, b):
            if scatter:
                rows = src_hbm.at[pl.ds(lax.rem(base, n_src) + c * SC_CHUNK, SC_CHUNK)]
            else:
                rows = src_hbm.at[idx_v.at[c]]
            return pltpu.make_async_copy(rows, rows_v.at[b], in_sem.at[b])

        def store(c, b):
            if scatter:
                rows = out_hbm.at[idx_v.at[c]]
            else:
                rows = out_hbm.at[pl.ds(base + c * SC_CHUNK, SC_CHUNK)]
            return pltpu.make_async_copy(rows_v.at[b], rows, out_sem.at[b])

        load(0, 0).start()

        @pl.loop(0, nchunk, step=2)
        def _(c0):
            for b in (0, 1):
                c = c0 + b
                load(c, b).wait()

                @pl.when(c + 1 < nchunk)
                def _():
                    @pl.when(c >= 1)
                    def _():
                        store(c - 1, 1 - b).wait()

                    load(c + 1, 1 - b).start()

                store(c, b).start()

        store(nchunk - 2, 0).wait()
        store(nchunk - 1, 1).wait()

    return move(src, idx.reshape(n_workers, nchunk, SC_CHUNK))


def _combine_kernel(x_ref, z0_ref, z1_ref, gates_ref, ada_ref, ng_ref, o_ref, *, final):
    gates = gates_ref[0]
    y = gates[:, 0:1] * z0_ref[...] + gates[:, 1:2] * z1_ref[...]
    x = x_ref[0] + ada_ref[0] * y
    if final:
        x = x * lax.rsqrt(jnp.mean(x * x, axis=-1, keepdims=True) + EPS) * ng_ref[...]
    o_ref[0] = x


def moe_combine(x, z, gates, gate_ada, norm_g, tm=512):
    final = norm_g is not None
    if not final:
        norm_g = jnp.ones((D_MODEL,), F32)
    nt = SEQ // tm
    return pl.pallas_call(
        functools.partial(_combine_kernel, final=final),
        grid=(BATCH, nt),
        in_specs=[
            pl.BlockSpec((1, tm, D_MODEL), lambda b, i: (b, i, 0)),
            pl.BlockSpec((tm, D_MODEL), lambda b, i: (b * nt + i, 0)),
            pl.BlockSpec((tm, D_MODEL), lambda b, i: (BATCH * nt + b * nt + i, 0)),
            pl.BlockSpec((1, tm, ROUTER_LANES), lambda b, i: (b, i, 0)),
            pl.BlockSpec((1, 1, D_MODEL), lambda b, i: (b, 0, 0)),
            pl.BlockSpec((1, D_MODEL), lambda b, i: (0, 0)),
        ],
        out_specs=pl.BlockSpec((1, tm, D_MODEL), lambda b, i: (b, i, 0)),
        out_shape=jax.ShapeDtypeStruct((BATCH, SEQ, D_MODEL), F32),
        compiler_params=_cparams(("parallel", "parallel")),
        name="moe_combine",
    )(x, z, z, gates, gate_ada, norm_g.reshape(1, D_MODEL))


def hierarchical_moe(x, c, norm_g, ada_w, ada_b, w_group, w_expert, w_gate_up, w_down, final_norm_g=None):
    shift, scale, gate_ada = ada_modulation(c, ada_w, ada_b)
    h, eid, gates = route(x, norm_g, scale, shift, _router_weights(w_group, w_expert))
    e_flat = jnp.concatenate([eid[:, 0, :].reshape(N_TOK), eid[:, 1, :].reshape(N_TOK)])
    dest, blk_expert, first, nxt, n_used = route_metadata(e_flat)
    xs = sc_move_rows(h.reshape(N_TOK, D_MODEL), dest, MOE_ROWS, scatter=True)
    yb = expert_blocks(xs, blk_expert, n_used, first, nxt, w_gate_up, w_down)
    z = sc_move_rows(yb, dest, TOP_K * N_TOK, scatter=False)
    return moe_combine(x, z, gates, gate_ada, final_norm_g)


def kernel(x, c, norm0_mix, ada0_mix_w, ada0_mix_b, w_in0, conv_w, conv_b, conv_norm_g, conv_norm_b, w_out0, norm0_ffn, ada0_ffn_w, ada0_ffn_b, moe0_w_group, moe0_w_expert, moe0_w_gate_up, moe0_w_down, norm1_mix, ada1_mix_w, ada1_mix_b, w_in1, lru_conv_w, lru_conv_b, lru_w_a, lru_b_a, lru_w_x, lru_b_x, lru_lambda, w_out1, norm1_ffn, ada1_ffn_w, ada1_ffn_b, moe1_w_group, moe1_w_expert, moe1_w_gate_up, moe1_w_down, norm_final):
    x = conv_attention_layer(x, c, norm0_mix, ada0_mix_w, ada0_mix_b, w_in0, conv_w, conv_b, conv_norm_g, conv_norm_b,
                             w_out0)
    x = hierarchical_moe(x, c, norm0_ffn, ada0_ffn_w, ada0_ffn_b, moe0_w_group, moe0_w_expert,
                         moe0_w_gate_up, moe0_w_down)

    x = lru_mixer_layer(x, c, norm1_mix, ada1_mix_w, ada1_mix_b, w_in1, lru_conv_w, lru_conv_b, lru_w_a, lru_b_a,
                        lru_w_x, lru_b_x, lru_lambda, w_out1)
    x = hierarchical_moe(x, c, norm1_ffn, ada1_ffn_w, ada1_ffn_b, moe1_w_group, moe1_w_expert,
                         moe1_w_gate_up, moe1_w_down, final_norm_g=norm_final)
    return x
```

```python
import functools

import jax
import jax.numpy as jnp
from jax import lax
from jax.experimental import pallas as pl
from jax.experimental.pallas import tpu as pltpu
from jax.experimental.pallas import tpu_sc as plsc

F32 = jnp.float32
BF16 = jnp.bfloat16

D_MODEL = 2048
BATCH = 4
SEQ = 4096
N_TOK = BATCH * SEQ
EPS = 1e-6
NEG_INF = -1e30

CONV_CH = 1024
CONV_GROUPS = 8
CONV_WIDTH = 31
CONV_HALO = 16

ATT_HEADS = 8
HEAD_DIM = 128
ATT_W = ATT_HEADS * HEAD_DIM
ATT_BRANCHES = ((128, 1), (512, 4), (2048, 16))
ATT_RADIUS = 64
ATT_QB = 128
ATT_KB = ATT_QB + 2 * ATT_RADIUS
ATT_COPY_ROWS = 512
ALIBI_MAX = 8.0

LRU_WIDTH = 2688
LRU_BLOCKS = 16
LRU_BW = LRU_WIDTH // LRU_BLOCKS
LRU_PW = 192
LRU_WP = LRU_BLOCKS * LRU_PW
LRU_GW = 2 * LRU_PW
LRU_CONV = 4
LRU_C = 8.0
LRU_CHUNKS = 8
LRU_CL = SEQ // LRU_CHUNKS
LRU_PITCH = LRU_CL + 8
LOG2E = 1.4426950408889634

N_GROUPS = 4
EXPERTS_PER_GROUP = 8
N_EXPERTS = 32
TOP_K = 2
EXPERT_FF = 1024
MOE_TB = 256
MOE_ROWS = N_TOK * TOP_K + N_EXPERTS * MOE_TB
MOE_NBLK = MOE_ROWS // MOE_TB
MOE_CAST_ROWS = 256
META_LANES = 256
assert MOE_NBLK <= META_LANES
SC_CORES = 2
SC_SUBCORES = 16
SC_CHUNK = 32
ROUTER_LANES = 128

MM_ROWS = 1024
VMEM_LIMIT = 48 * 1024 * 1024
BIG_VMEM_LIMIT = 56 * 1024 * 1024


def _cparams(sem):
    return pltpu.CompilerParams(dimension_semantics=sem, vmem_limit_bytes=VMEM_LIMIT)


def _sigmoid(x):
    return 0.5 * jnp.tanh(0.5 * x) + 0.5


def _ada_kernel(ct_ref, w_ref, b_ref, o_ref):
    ct = ct_ref[...]
    st = ct * _sigmoid(ct)
    w = w_ref[...]
    rows = [jnp.sum(w * st[:, b:b + 1], axis=0, keepdims=True) for b in range(BATCH)]
    o_ref[...] = jnp.concatenate(rows, axis=0) + b_ref[...]


def ada_modulation(c, w, b):
    tn = 512
    n = w.shape[1]
    mod = pl.pallas_call(
        _ada_kernel,
        grid=(n // tn,),
        in_specs=[
            pl.BlockSpec((D_MODEL, BATCH), lambda j: (0, 0)),
            pl.BlockSpec((D_MODEL, tn), lambda j: (0, j)),
            pl.BlockSpec((1, tn), lambda j: (0, j)),
        ],
        out_specs=pl.BlockSpec((BATCH, tn), lambda j: (0, j)),
        out_shape=jax.ShapeDtypeStruct((BATCH, n), F32),
        compiler_params=_cparams(("parallel",)),
        name="ada_modulation",
    )(c.T, w, b.reshape(1, n))
    shift, scale, gate = jnp.split(mod, 3, axis=-1)
    return shift[:, None, :], scale[:, None, :], gate[:, None, :]


def _modulated_norm(x, g, scale, shift):
    y = x * lax.rsqrt(jnp.mean(x * x, axis=-1, keepdims=True) + EPS)
    return (y * g) * (1.0 + scale) + shift


def _norm_kernel(x_ref, g_ref, sc_ref, sh_ref, o_ref):
    o_ref[0] = _modulated_norm(x_ref[0], g_ref[...], sc_ref[0], sh_ref[0]).astype(o_ref.dtype)


def modulated_norm(x, g, scale, shift, tm=512):
    return pl.pallas_call(
        _norm_kernel,
        grid=(BATCH, SEQ // tm),
        in_specs=[
            pl.BlockSpec((1, tm, D_MODEL), lambda b, i: (b, i, 0)),
            pl.BlockSpec((1, D_MODEL), lambda b, i: (0, 0)),
            pl.BlockSpec((1, 1, D_MODEL), lambda b, i: (b, 0, 0)),
            pl.BlockSpec((1, 1, D_MODEL), lambda b, i: (b, 0, 0)),
        ],
        out_specs=pl.BlockSpec((1, tm, D_MODEL), lambda b, i: (b, i, 0)),
        out_shape=jax.ShapeDtypeStruct((BATCH, SEQ, D_MODEL), BF16),
        compiler_params=_cparams(("parallel", "parallel")),
        name="modulated_norm",
    )(x, g.reshape(1, D_MODEL), scale, shift)


def _mm_kernel(a_ref, w_ref, o_ref, *, head_major):
    res = jnp.dot(a_ref[0], w_ref[...], preferred_element_type=F32)
    if head_major:
        for hh in range(ATT_HEADS):
            o_ref[0, 0, hh] = res[:, hh * HEAD_DIM:(hh + 1) * HEAD_DIM].astype(o_ref.dtype)
    else:
        o_ref[0] = res.astype(o_ref.dtype)


def project(a, w, out_dtype, head_major=False, tm=MM_ROWS, tn=1024):
    k, n = w.shape
    if head_major:
        assert tn == ATT_W
        out_shape = jax.ShapeDtypeStruct((n // tn, BATCH, ATT_HEADS, SEQ, HEAD_DIM), out_dtype)
        out_spec = pl.BlockSpec((1, 1, ATT_HEADS, tm, HEAD_DIM), lambda b, i, j: (j, b, 0, i, 0))
    else:
        out_shape = jax.ShapeDtypeStruct((BATCH, SEQ, n), out_dtype)
        out_spec = pl.BlockSpec((1, tm, tn), lambda b, i, j: (b, i, j))
    return pl.pallas_call(
        functools.partial(_mm_kernel, head_major=head_major),
        grid=(BATCH, SEQ // tm, n // tn),
        in_specs=[
            pl.BlockSpec((1, tm, k), lambda b, i, j: (b, i, 0)),
            pl.BlockSpec((k, tn), lambda b, i, j: (0, j)),
        ],
        out_specs=out_spec,
        out_shape=out_shape,
        compiler_params=_cparams(("parallel", "parallel", "parallel")),
        name="project",
    )(a, w)


def _mm_residual_kernel(*refs, n_in):
    a_refs, w_refs = refs[:n_in], refs[n_in:2 * n_in]
    x_ref, gate_ref, o_ref = refs[2 * n_in:]
    acc = jnp.dot(a_refs[0][0], w_refs[0][...], preferred_element_type=F32)
    for a_ref, w_ref in zip(a_refs[1:], w_refs[1:]):
        acc = acc + jnp.dot(a_ref[0], w_ref[...], preferred_element_type=F32)
    o_ref[0] = x_ref[0] + gate_ref[0] * acc


def project_residual(a_list, w_list, x, gate, tm=MM_ROWS, tn=1024):
    n_in = len(a_list)
    grid = (BATCH, SEQ // tm, D_MODEL // tn)
    in_specs = [pl.BlockSpec((1, tm, a.shape[-1]), lambda b, i, j: (b, i, 0)) for a in a_list]
    in_specs += [pl.BlockSpec((w.shape[0], tn), lambda b, i, j: (0, j)) for w in w_list]
    in_specs += [
        pl.BlockSpec((1, tm, tn), lambda b, i, j: (b, i, j)),
        pl.BlockSpec((1, 1, tn), lambda b, i, j: (b, 0, j)),
    ]
    return pl.pallas_call(
        functools.partial(_mm_residual_kernel, n_in=n_in),
        grid=grid,
        in_specs=in_specs,
        out_specs=pl.BlockSpec((1, tm, tn), lambda b, i, j: (b, i, j)),
        out_shape=jax.ShapeDtypeStruct((BATCH, SEQ, D_MODEL), F32),
        compiler_params=_cparams(("parallel", "parallel", "parallel")),
        name="project_residual",
    )(*a_list, *w_list, x, gate)


def _conv_kernel(v_ref, g_ref, cw_ref, cb_ref, ng_ref, nb_ref, o_ref, pad_scr, *, ts):
    zeros = jnp.zeros((CONV_HALO, 128), F32)
    pad_scr[0:CONV_HALO, :] = zeros
    pad_scr[SEQ + CONV_HALO:SEQ + 2 * CONV_HALO, :] = zeros
    for t0 in range(0, SEQ, ts):
        pad_scr[CONV_HALO + t0:CONV_HALO + t0 + ts, :] = v_ref[0, t0:t0 + ts, :] * _sigmoid(g_ref[0, t0:t0 + ts, :])
    first = CONV_HALO - CONV_WIDTH // 2
    for t0 in range(0, SEQ, ts):
        acc = cw_ref[0:1, :] * pad_scr[first + t0:first + t0 + ts, :] + cb_ref[...]
        for k in range(1, CONV_WIDTH):
            acc = acc + cw_ref[k:k + 1, :] * pad_scr[first + t0 + k:first + t0 + k + ts, :]
        mu = jnp.mean(acc, axis=-1, keepdims=True)
        cen = acc - mu
        var = jnp.mean(cen * cen, axis=-1, keepdims=True)
        y = cen * lax.rsqrt(var + EPS) * ng_ref[...] + nb_ref[...]
        o_ref[0, t0:t0 + ts, :] = (y * _sigmoid(y)).astype(o_ref.dtype)


def conv_module(p, conv_w, conv_b, norm_g, norm_b, ts=128):
    ng = CONV_GROUPS
    vec = lambda a: a.reshape(1, CONV_CH)
    vspec = pl.BlockSpec((1, 128), lambda b, c: (0, c))
    return pl.pallas_call(
        functools.partial(_conv_kernel, ts=ts),
        grid=(BATCH, ng),
        in_specs=[
            pl.BlockSpec((1, SEQ, 128), lambda b, c: (b, 0, c)),
            pl.BlockSpec((1, SEQ, 128), lambda b, c: (b, 0, c + ng)),
            pl.BlockSpec((CONV_WIDTH, 128), lambda b, c: (0, c)),
            vspec, vspec, vspec,
        ],
        out_specs=pl.BlockSpec((1, SEQ, 128), lambda b, c: (b, 0, c)),
        out_shape=jax.ShapeDtypeStruct((BATCH, SEQ, CONV_CH), BF16),
        scratch_shapes=[pltpu.VMEM((SEQ + 2 * CONV_HALO, 128), F32)],
        compiler_params=_cparams(("parallel", "parallel")),
        name="conv_module",
    )(p, p, conv_w, vec(conv_b), vec(norm_g), vec(norm_b))


def _attn_kernel(q_ref, k_ref, v_ref, *rest):
    nbr = len(ATT_BRANCHES)
    bias_refs, o_ref = rest[:nbr], rest[nbr]
    qf, kf, vf, qd, kpad, vpad, acc, mx, den = rest[nbr + 1:]
    scale = HEAD_DIM ** -0.5
    zeros = jnp.zeros((ATT_RADIUS, HEAD_DIM), BF16)
    for t0 in range(0, SEQ, ATT_COPY_ROWS):
        rows = slice(t0, t0 + ATT_COPY_ROWS)
        qf[rows, :] = q_ref[0, 0, rows, :].astype(F32)
        kf[rows, :] = k_ref[0, 0, rows, :].astype(F32)
        vf[rows, :] = v_ref[0, 0, rows, :].astype(F32)

    order = sorted(range(nbr), key=lambda n: -ATT_BRANCHES[n][1])
    for pos, n in enumerate(order):
        dil = ATT_BRANCHES[n][1]
        bias_ref = bias_refs[n]
        sub = SEQ // dil
        nblk = sub // ATT_QB
        first, last = pos == 0, pos == nbr - 1
        assert dil == 1 or not last

        def class_rows(r, start, count, dil=dil):
            if dil == 1:
                return pl.ds(start, count)
            return pl.ds(r + start * dil, count, stride=dil)

        def residue_class(r, carry, dil=dil, sub=sub, nblk=nblk, bias_ref=bias_ref, first=first, last=last,
                          class_rows=class_rows):
            kpad[0:ATT_RADIUS, :] = zeros
            kpad[sub + ATT_RADIUS:sub + 2 * ATT_RADIUS, :] = zeros
            vpad[0:ATT_RADIUS, :] = zeros
            vpad[sub + ATT_RADIUS:sub + 2 * ATT_RADIUS, :] = zeros
            for c0 in range(0, sub, ATT_COPY_ROWS):
                n_rows = min(ATT_COPY_ROWS, sub - c0)
                src = class_rows(r, c0, n_rows)
                qd[c0:c0 + n_rows, :] = qf[src, :].astype(BF16)
                kpad[ATT_RADIUS + c0:ATT_RADIUS + c0 + n_rows, :] = kf[src, :].astype(BF16)
                vpad[ATT_RADIUS + c0:ATT_RADIUS + c0 + n_rows, :] = vf[src, :].astype(BF16)

            def block(i, carry):
                q0 = pl.multiple_of(i * ATT_QB, ATT_QB)
                s = lax.dot_general(qd[pl.ds(q0, ATT_QB), :], kpad[pl.ds(q0, ATT_KB), :],
                                    (((1,), (1,)), ((), ())), preferred_element_type=F32)
                variant = jnp.where(i == 0, 0, jnp.where(i == nblk - 1, 2, 1))
                s = s * scale + bias_ref[0, variant]
                m = jnp.max(s, axis=-1, keepdims=True)
                e = jnp.exp(s - m)
                l = jnp.sum(e, axis=-1, keepdims=True)
                o = jnp.dot(e.astype(BF16), vpad[pl.ds(q0, ATT_KB), :], preferred_element_type=F32)
                nat = class_rows(r, q0, ATT_QB)
                wide = (ATT_QB, HEAD_DIM)
                if first:
                    acc[nat, :] = o
                    mx[nat, :] = jnp.broadcast_to(m, wide)
                    den[nat, :] = jnp.broadcast_to(l, wide)
                else:
                    m_old = mx[nat, :]
                    m_new = jnp.maximum(m_old, m)
                    alpha = jnp.exp(m_old - m_new)
                    beta = jnp.exp(m - m_new)
                    total = acc[nat, :] * alpha + o * beta
                    l_new = den[nat, :] * alpha + l * beta
                    if last:
                        o_ref[0, nat, :] = (total / l_new).astype(o_ref.dtype)
                    else:
                        acc[nat, :] = total
                        mx[nat, :] = m_new
                        den[nat, :] = l_new
                return carry

            lax.fori_loop(0, nblk, block, 0, unroll=min(nblk, 4))
            return carry

        if dil == 1:
            residue_class(0, 0)
        else:
            lax.fori_loop(0, dil, residue_class, 0)


def _alibi_bias(dilation):
    row = jnp.arange(ATT_QB)[:, None]
    colm = jnp.arange(ATT_KB)[None, :]
    rel = jnp.abs(colm - ATT_RADIUS - row)
    slopes = 2.0 ** (-ALIBI_MAX * jnp.arange(1, ATT_HEADS + 1, dtype=F32) / ATT_HEADS)
    bias = -slopes[:, None, None] * (rel * dilation).astype(F32)[None]
    band = rel <= ATT_RADIUS
    masks = jnp.stack([band & (colm >= ATT_RADIUS), band, band & (colm < ATT_KB - ATT_RADIUS)])
    return jnp.where(masks[None], bias[:, None], NEG_INF)


def dilated_attention(qkv):
    for window, dilation in ATT_BRANCHES:
        assert window // (2 * dilation) == ATT_RADIUS
        assert SEQ // dilation >= 2 * ATT_QB
    qspec = lambda which: pl.BlockSpec((None, 1, 1, SEQ, HEAD_DIM), lambda b, h, which=which: (which, b, h, 0, 0))
    bias_spec = pl.BlockSpec((1, 3, ATT_QB, ATT_KB), lambda b, h: (h, 0, 0, 0))
    rows = lambda dtype, n=SEQ: pltpu.VMEM((n, HEAD_DIM), dtype)
    return pl.pallas_call(
        _attn_kernel,
        grid=(BATCH, ATT_HEADS),
        in_specs=[qspec(0), qspec(1), qspec(2)] + [bias_spec] * len(ATT_BRANCHES),
        out_specs=pl.BlockSpec((1, SEQ, HEAD_DIM), lambda b, h: (b, 0, h)),
        out_shape=jax.ShapeDtypeStruct((BATCH, SEQ, ATT_W), BF16),
        scratch_shapes=[rows(F32), rows(F32), rows(F32), rows(BF16),
                        rows(BF16, SEQ + 2 * ATT_RADIUS), rows(BF16, SEQ + 2 * ATT_RADIUS),
                        rows(F32), rows(F32), rows(F32)],
        compiler_params=_cparams(("parallel", "parallel")),
        name="dilated_attention",
    )(qkv, qkv, qkv, *[_alibi_bias(dilation) for _, dilation in ATT_BRANCHES])


def conv_attention_layer(x, c, norm_g, ada_w, ada_b, w_in, conv_w, conv_b, conv_norm_g, conv_norm_b, w_out):
    shift, scale, gate = ada_modulation(c, ada_w, ada_b)
    w_in = w_in.astype(BF16)
    h = modulated_norm(x, norm_g, scale, shift)
    p_conv = project(h, w_in[:, :2 * CONV_CH], F32)
    qkv = project(h, w_in[:, 2 * CONV_CH:], BF16, head_major=True)
    a = conv_module(p_conv, conv_w, conv_b, conv_norm_g, conv_norm_b)
    o = dilated_attention(qkv)
    w_out = w_out.astype(BF16)
    return project_residual([a, o], [w_out[:CONV_CH], w_out[CONV_CH:]], x, gate)


def _softplus(x):
    return jnp.maximum(x, 0.0) + jnp.log1p(jnp.exp(-jnp.abs(x)))


def _gelu_tanh(x):
    return 0.5 * x * (1.0 + jnp.tanh(0.7978845608028654 * (x + 0.044715 * (x * x * x))))


def _lru_kernel(x_ref, cw_ref, cb_ref, w_ref, b_ref, lam_ref, o_ref, xc, a_scr, u_scr, yacc):
    gw = LRU_GW
    ntile = gw // 128
    lead = LRU_CONV // 2

    def shifted(r0, off):
        lo, hi = r0 + off, r0 + off + LRU_CL
        parts = []
        if lo < 0:
            parts.append(jnp.zeros((-lo, gw), F32))
        parts.append(x_ref[0, max(lo, 0):min(hi, SEQ), :])
        if hi > SEQ:
            parts.append(jnp.zeros((hi - SEQ, gw), F32))
        return parts[0] if len(parts) == 1 else jnp.concatenate(parts, axis=0)

    for j in range(LRU_CHUNKS):
        r0 = j * LRU_CL
        acc = cw_ref[0:1, :] * shifted(r0, -lead) + cb_ref[...]
        for k in range(1, LRU_CONV):
            acc = acc + cw_ref[k:k + 1, :] * shifted(r0, k - lead)
        xc[r0:r0 + LRU_CL, :] = acc

    for d in range(2):
        reverse = d == 1
        half_decay = (0.5 * LRU_C) * _softplus(-lam_ref[d:d + 1, :])
        for j in range(LRU_CHUNKS):
            xj = xc[j * LRU_CL:(j + 1) * LRU_CL, :]
            pre = jnp.dot(xj.astype(BF16), w_ref[d, 0], preferred_element_type=F32) + b_ref[d, 0]
            t_r = jnp.tanh(pre[:, :gw])
            t_i = jnp.tanh(pre[:, gw:])
            neg_log_a = t_r * half_decay + half_decay
            a = jnp.exp2(neg_log_a * (-LOG2E))
            one_minus_a2 = jnp.tanh(neg_log_a) * (a * a + 1.0)
            root = one_minus_a2 * lax.rsqrt(jnp.maximum(one_minus_a2, 1e-30))
            u = root * ((0.5 * t_i + 0.5) * xj)
            for lt in range(ntile):
                rows = slice(j * LRU_PITCH, j * LRU_PITCH + LRU_CL)
                a_scr[lt, rows, :] = a[:, lt * 128:(lt + 1) * 128]
                u_scr[lt, rows, :] = u[:, lt * 128:(lt + 1) * 128]

        def step(ii, carry):
            row = (LRU_CL - 1 - ii) if reverse else ii
            out = []
            for lt in range(ntile):
                h, prod = carry[lt]
                a = a_scr[lt, pl.ds(row, LRU_CHUNKS, stride=LRU_PITCH), :]
                h = a * h + u_scr[lt, pl.ds(row, LRU_CHUNKS, stride=LRU_PITCH), :]
                prod = a * prod
                u_scr[lt, pl.ds(row, LRU_CHUNKS, stride=LRU_PITCH), :] = h
                a_scr[lt, pl.ds(row, LRU_CHUNKS, stride=LRU_PITCH), :] = prod
                out.append((h, prod))
            return tuple(out)

        init = tuple((jnp.zeros((LRU_CHUNKS, 128), F32), jnp.ones((LRU_CHUNKS, 128), F32)) for _ in range(ntile))
        lax.fori_loop(0, LRU_CL, step, init, unroll=4)

        last = 0 if reverse else LRU_CL - 1
        for lt in range(ntile):
            lanes = slice(lt * 128, (lt + 1) * 128)
            h_end = u_scr[lt, pl.ds(last, LRU_CHUNKS, stride=LRU_PITCH), :]
            p_end = a_scr[lt, pl.ds(last, LRU_CHUNKS, stride=LRU_PITCH), :]
            carry = jnp.zeros((1, 128), F32)
            for j in (range(LRU_CHUNKS - 1, -1, -1) if reverse else range(LRU_CHUNKS)):
                src = slice(j * LRU_PITCH, j * LRU_PITCH + LRU_CL)
                dst = slice(j * LRU_CL, (j + 1) * LRU_CL)
                y = u_scr[lt, src, :] + a_scr[lt, src, :] * carry
                if reverse:
                    o_ref[0, dst, lanes] = (yacc[dst, lanes] + y).astype(o_ref.dtype)
                else:
                    yacc[dst, lanes] = y
                carry = h_end[j:j + 1, :] + p_end[j:j + 1, :] * carry


def rglru_block(p, conv_w, conv_b, w_gates, b_gates, lam):
    npair = LRU_WP // LRU_GW
    gw = LRU_GW
    big = lambda: pltpu.VMEM((SEQ, gw), F32)
    slabs = lambda: pltpu.VMEM((gw // 128, LRU_CHUNKS * LRU_PITCH, 128), F32)
    return pl.pallas_call(
        _lru_kernel,
        grid=(BATCH, npair),
        in_specs=[
            pl.BlockSpec((1, SEQ, gw), lambda b, n: (b, 0, n + npair)),
            pl.BlockSpec((LRU_CONV, gw), lambda b, n: (0, n)),
            pl.BlockSpec((1, gw), lambda b, n: (0, n)),
            pl.BlockSpec((2, 1, gw, 2 * gw), lambda b, n: (0, n, 0, 0)),
            pl.BlockSpec((2, 1, 1, 2 * gw), lambda b, n: (0, n, 0, 0)),
            pl.BlockSpec((2, gw), lambda b, n: (0, n)),
        ],
        out_specs=pl.BlockSpec((1, SEQ, gw), lambda b, n: (b, 0, n)),
        out_shape=jax.ShapeDtypeStruct((BATCH, SEQ, LRU_WP), BF16),
        scratch_shapes=[big(), slabs(), slabs(), big()],
        compiler_params=pltpu.CompilerParams(dimension_semantics=("parallel", "parallel"),
                                             vmem_limit_bytes=BIG_VMEM_LIMIT),
        name="rglru_block",
    )(p, conv_w, conv_b, w_gates, b_gates, lam)


def _gelu_gate_kernel(y_ref, gate_ref, o_ref):
    o_ref[0] = (_gelu_tanh(gate_ref[0]) * y_ref[0].astype(F32)).astype(o_ref.dtype)


def gelu_gate(y, p, tm=512):
    spec = pl.BlockSpec((1, tm, LRU_WP), lambda b, i: (b, i, 0))
    return pl.pallas_call(
        _gelu_gate_kernel,
        grid=(BATCH, SEQ // tm),
        in_specs=[spec, spec],
        out_specs=spec,
        out_shape=jax.ShapeDtypeStruct((BATCH, SEQ, LRU_WP), BF16),
        compiler_params=_cparams(("parallel", "parallel")),
        name="gelu_gate",
    )(y, p)


def _pad_blocks(a):
    lead = a.shape[:-1]
    a = a.reshape(*lead, LRU_BLOCKS, LRU_BW)
    a = jnp.pad(a, [(0, 0)] * len(lead) + [(0, 0), (0, LRU_PW - LRU_BW)])
    return a.reshape(*lead, LRU_WP)


def _pair_block_diagonal(w):
    pad = LRU_PW - LRU_BW
    w = jnp.pad(w, ((0, 0), (0, 0), (0, pad), (0, pad))).reshape(2, LRU_BLOCKS // 2, 2, LRU_PW, LRU_PW)
    zero = jnp.zeros_like(w[:, :, 0])
    top = jnp.concatenate([w[:, :, 0], zero], axis=-1)
    bottom = jnp.concatenate([zero, w[:, :, 1]], axis=-1)
    return jnp.concatenate([top, bottom], axis=-2)


def lru_mixer_layer(x, c, norm_g, ada_w, ada_b, w_in, conv_w, conv_b, w_a, b_a, w_x, b_x, lam, w_out):
    shift, scale, gate = ada_modulation(c, ada_w, ada_b)
    w_in_p = jnp.concatenate([_pad_blocks(w_in[:, :LRU_WIDTH]), _pad_blocks(w_in[:, LRU_WIDTH:])], axis=1)
    p = project(modulated_norm(x, norm_g, scale, shift), w_in_p.astype(BF16), F32)
    npair = LRU_WP // LRU_GW
    w_gates = jnp.concatenate([_pair_block_diagonal(w_a), _pair_block_diagonal(w_x)], axis=-1)
    b_gates = jnp.concatenate([_pad_blocks(b_a).reshape(2, npair, 1, LRU_GW),
                               _pad_blocks(b_x).reshape(2, npair, 1, LRU_GW)], axis=-1)
    y = rglru_block(p, _pad_blocks(conv_w), _pad_blocks(conv_b).reshape(1, LRU_WP),
                    (0.5 * w_gates).astype(BF16), 0.5 * b_gates, _pad_blocks(lam))
    w_out_p = jnp.pad(w_out.reshape(LRU_BLOCKS, LRU_BW, D_MODEL), ((0, 0), (0, LRU_PW - LRU_BW), (0, 0)))
    return project_residual([gelu_gate(y, p)], [w_out_p.reshape(LRU_WP, D_MODEL).astype(BF16)], x, gate)


def _pack_halves(v):
    half = v.shape[1] // 2
    as_bits = lambda t: pltpu.bitcast(t.astype(BF16).astype(F32), jnp.uint32)
    word = as_bits(v[:, :half]) | (as_bits(v[:, half:]) >> 16)
    return pltpu.bitcast(word, jnp.int32)


def _unpack_halves(word):
    bits = pltpu.bitcast(word, jnp.uint32)
    first = pltpu.bitcast(bits & jnp.uint32(0xFFFF0000), F32)
    second = pltpu.bitcast(bits << 16, F32)
    return first, second


def _router_kernel(x_ref, g_ref, sc_ref, sh_ref, wr_ref, h_ref, eid_ref, gt_ref):
    h = _modulated_norm(x_ref[0], g_ref[...], sc_ref[0], sh_ref[0])
    hb = h.astype(BF16)
    h_ref[0] = _pack_halves(h)
    h_lo = (h - hb.astype(F32)).astype(BF16)
    w = wr_ref[...]
    w_hi = w.astype(BF16)
    w_lo = (w - w_hi.astype(F32)).astype(BF16)
    lg = (jnp.dot(hb, w_hi, preferred_element_type=F32) + jnp.dot(h_lo, w_hi, preferred_element_type=F32)
          + jnp.dot(hb, w_lo, preferred_element_type=F32))
    lt = lg.T
    tm = lt.shape[1]
    row = lax.broadcasted_iota(jnp.int32, (8, tm), 0)
    big = jnp.int32(99)
    gl = jnp.where(row < N_GROUPS, lt[0:8], -jnp.inf)
    g_max = jnp.max(gl, axis=0, keepdims=True)
    g_sel = jnp.min(jnp.where(gl == g_max, row, big), axis=0, keepdims=True)
    g_prob = 1.0 / jnp.sum(jnp.exp(gl - g_max), axis=0, keepdims=True)
    el = jnp.zeros((8, tm), F32)
    for g in range(N_GROUPS):
        el = jnp.where(g_sel == g, lt[8 + 8 * g:16 + 8 * g], el)
    v1 = jnp.max(el, axis=0, keepdims=True)
    i1 = jnp.min(jnp.where(el == v1, row, big), axis=0, keepdims=True)
    el2 = jnp.where(row == i1, -jnp.inf, el)
    v2 = jnp.max(el2, axis=0, keepdims=True)
    i2 = jnp.min(jnp.where(el2 == v2, row, big), axis=0, keepdims=True)
    e2 = jnp.exp(v2 - v1)
    p1 = 1.0 / (1.0 + e2)
    p2 = e2 * p1
    eid_ref[0] = jnp.where(row == 0, g_sel * EXPERTS_PER_GROUP + i1,
                           jnp.where(row == 1, g_sel * EXPERTS_PER_GROUP + i2, 0))
    rows = lax.broadcasted_iota(jnp.int32, (ROUTER_LANES, tm), 0)
    gates = jnp.where(rows == 0, g_prob * p1, jnp.where(rows == 1, g_prob * p2, 0.0))
    gt_ref[0] = gates.T


def route(x, g, scale, shift, w_router, tm=512):
    return pl.pallas_call(
        _router_kernel,
        grid=(BATCH, SEQ // tm),
        in_specs=[
            pl.BlockSpec((1, tm, D_MODEL), lambda b, i: (b, i, 0)),
            pl.BlockSpec((1, D_MODEL), lambda b, i: (0, 0)),
            pl.BlockSpec((1, 1, D_MODEL), lambda b, i: (b, 0, 0)),
            pl.BlockSpec((1, 1, D_MODEL), lambda b, i: (b, 0, 0)),
            pl.BlockSpec((D_MODEL, ROUTER_LANES), lambda b, i: (0, 0)),
        ],
        out_specs=[
            pl.BlockSpec((1, tm, D_MODEL // 2), lambda b, i: (b, i, 0)),
            pl.BlockSpec((1, 8, tm), lambda b, i: (b, 0, i)),
            pl.BlockSpec((1, tm, ROUTER_LANES), lambda b, i: (b, i, 0)),
        ],
        out_shape=[
            jax.ShapeDtypeStruct((BATCH, SEQ, D_MODEL // 2), jnp.int32),
            jax.ShapeDtypeStruct((BATCH, 8, SEQ), jnp.int32),
            jax.ShapeDtypeStruct((BATCH, SEQ, ROUTER_LANES), F32),
        ],
        compiler_params=_cparams(("parallel", "parallel")),
        name="route",
    )(x, g.reshape(1, D_MODEL), scale, shift, w_router)


def _expert_kernel(be_ref, nu_ref, first_ref, nxt_ref, x_ref, wgu_hbm, wd_hbm, o_ref,
                   gu_stage, d_stage, wgu, wd, sem):
    i = pl.program_id(0)
    active = i < nu_ref[0]

    def weight_copies(e):
        return (pltpu.make_async_copy(wgu_hbm.at[e], gu_stage, sem.at[0]),
                pltpu.make_async_copy(wd_hbm.at[e], d_stage, sem.at[1]))

    @pl.when(i == 0)
    def _():
        for cp in weight_copies(be_ref[0]):
            cp.start()

    @pl.when(active & (first_ref[i] == 1))
    def _():
        for cp in weight_copies(be_ref[i]):
            cp.wait()

        def cast_rows(stage, dst):
            def body(c, carry):
                r0 = pl.multiple_of(c * MOE_CAST_ROWS, MOE_CAST_ROWS)
                dst[pl.ds(r0, MOE_CAST_ROWS), :] = stage[pl.ds(r0, MOE_CAST_ROWS), :].astype(BF16)
                return carry
            lax.fori_loop(0, stage.shape[0] // MOE_CAST_ROWS, body, 0)

        cast_rows(gu_stage, wgu)
        cast_rows(d_stage, wd)

        @pl.when(nxt_ref[i] >= 0)
        def _():
            for cp in weight_copies(nxt_ref[i]):
                cp.start()

    @pl.when(active)
    def _():
        x_first, x_second = _unpack_halves(x_ref[...])
        half = D_MODEL // 2
        gu = (jnp.dot(x_first.astype(BF16), wgu[0:half, :], preferred_element_type=F32)
              + jnp.dot(x_second.astype(BF16), wgu[half:, :], preferred_element_type=F32))
        g = gu[:, :EXPERT_FF]
        u = gu[:, EXPERT_FF:]
        act = (g * _sigmoid(g) * u).astype(BF16)
        o_ref[...] = _pack_halves(jnp.dot(act, wd[...], preferred_element_type=F32))

    @pl.when(jnp.logical_not(active))
    def _():
        o_ref[...] = jnp.zeros(o_ref.shape, o_ref.dtype)


def expert_blocks(xs, blk_expert, n_used, first, nxt, w_gate_up, w_down):
    row_map = lambda i, be, nu, first, nxt: (jnp.minimum(i, nu[0] - 1), 0)
    return pl.pallas_call(
        _expert_kernel,
        grid_spec=pltpu.PrefetchScalarGridSpec(
            num_scalar_prefetch=4,
            grid=(MOE_NBLK,),
            in_specs=[
                pl.BlockSpec((MOE_TB, D_MODEL // 2), row_map),
                pl.BlockSpec(memory_space=pl.ANY),
                pl.BlockSpec(memory_space=pl.ANY),
            ],
            out_specs=pl.BlockSpec((MOE_TB, D_MODEL // 2), lambda i, be, nu, first, nxt: (i, 0)),
            scratch_shapes=[
                pltpu.VMEM((D_MODEL, 2 * EXPERT_FF), F32),
                pltpu.VMEM((EXPERT_FF, D_MODEL), F32),
                pltpu.VMEM((D_MODEL, 2 * EXPERT_FF), BF16),
                pltpu.VMEM((EXPERT_FF, D_MODEL), BF16),
                pltpu.SemaphoreType.DMA((2,)),
            ],
        ),
        out_shape=jax.ShapeDtypeStruct((MOE_ROWS, D_MODEL // 2), jnp.int32),
        compiler_params=pltpu.CompilerParams(dimension_semantics=("arbitrary",), vmem_limit_bytes=BIG_VMEM_LIMIT),
        name="expert_blocks",
    )(blk_expert, n_used, first, nxt, xs, w_gate_up, w_down)


def _router_weights(w_group, w_expert):
    w = jnp.zeros((D_MODEL, ROUTER_LANES), F32)
    w = w.at[:, 0:N_GROUPS].set(w_group)
    return w.at[:, 8:8 + N_EXPERTS].set(w_expert)


def _route_meta_kernel(e_ref, dest_ref, be_ref, first_ref, nxt_ref, nu_ref):
    nrow = e_ref.shape[0]
    expert = lax.broadcasted_iota(jnp.int32, (N_EXPERTS, META_LANES), 0)
    upto = (lax.broadcasted_iota(jnp.int32, (META_LANES, META_LANES), 0)
            <= lax.broadcasted_iota(jnp.int32, (META_LANES, META_LANES), 1)).astype(BF16)

    def count_row(c, acc):
        return acc + jnp.where(expert == e_ref[c], 1.0, 0.0)

    acc = lax.fori_loop(0, nrow, count_row, jnp.zeros((N_EXPERTS, META_LANES), F32))
    counts = jnp.sum(acc, axis=1, keepdims=True)
    pcounts = jnp.floor((counts + (MOE_TB - 1.0)) * (1.0 / MOE_TB)) * MOE_TB
    ends = []
    run = jnp.zeros((1, 1), F32)
    for e in range(N_EXPERTS):
        run = run + pcounts[e:e + 1, :]
        ends.append(run)
    pend = jnp.concatenate(ends, axis=0)
    pstart = pend - pcounts

    def dest_row(c, running):
        hit = expert == e_ref[c]
        seen = jnp.dot(jnp.where(hit, 1.0, 0.0).astype(BF16), upto, preferred_element_type=F32)
        slot = seen - 1.0 + (running + pstart)
        dest_ref[c] = jnp.sum(jnp.where(hit, slot, 0.0), axis=0, keepdims=True).astype(jnp.int32)
        return running + seen[:, META_LANES - 1:META_LANES]

    lax.fori_loop(0, nrow, dest_row, jnp.zeros((N_EXPERTS, 1), F32))

    blk = lax.broadcasted_iota(jnp.int32, (1, META_LANES), 1).astype(F32)
    n_used = pend[N_EXPERTS - 1:N_EXPERTS, :] * (1.0 / MOE_TB)

    def expert_at(b):
        start = jnp.minimum(b, n_used - 1.0) * MOE_TB
        return jnp.minimum(jnp.sum(jnp.where(pend <= start, 1.0, 0.0), axis=0, keepdims=True), N_EXPERTS - 1.0)

    be = expert_at(blk)
    is_first = jnp.logical_and(jnp.logical_or(blk == 0.0, be != expert_at(blk - 1.0)), blk < n_used)
    later = jnp.logical_and(expert.astype(F32) > be, counts > 0.0)
    nxt = jnp.min(jnp.where(later, expert.astype(F32), 2.0 * N_EXPERTS), axis=0, keepdims=True)
    be_ref[...] = be.astype(jnp.int32)
    first_ref[...] = jnp.where(is_first, 1, 0)
    nxt_ref[...] = jnp.where(nxt < N_EXPERTS, nxt, -1.0).astype(jnp.int32)
    nu_ref[...] = jnp.broadcast_to(n_used, (1, META_LANES)).astype(jnp.int32)


def route_metadata(e_flat):
    nrow = e_flat.shape[0] // META_LANES
    lane_row = jax.ShapeDtypeStruct((1, META_LANES), jnp.int32)
    dest, be, first, nxt, nu = pl.pallas_call(
        _route_meta_kernel,
        out_shape=[jax.ShapeDtypeStruct((nrow, 1, META_LANES), jnp.int32), lane_row, lane_row, lane_row, lane_row],
        compiler_params=pltpu.CompilerParams(vmem_limit_bytes=VMEM_LIMIT),
        name="route_metadata",
    )(e_flat.reshape(nrow, 1, META_LANES))
    return dest.reshape(-1), be[0, :MOE_NBLK], first[0, :MOE_NBLK], nxt[0, :MOE_NBLK], nu[0, :1]


def sc_move_rows(src, idx, n_out, scatter):
    n_idx = idx.shape[0]
    n_src, width = src.shape
    n_workers = SC_CORES * SC_SUBCORES
    per_w = n_idx // n_workers
    nchunk = per_w // SC_CHUNK
    assert per_w * n_workers == n_idx and nchunk * SC_CHUNK == per_w and nchunk % 2 == 0
    assert per_w % n_src == 0 or n_src % per_w == 0
    mesh = plsc.VectorSubcoreMesh(core_axis_name="c", subcore_axis_name="s")

    @functools.partial(
        pl.kernel, mesh=mesh,
        out_type=jax.ShapeDtypeStruct((n_out, width), src.dtype),
        scratch_types=[
            pltpu.VMEM((nchunk, SC_CHUNK), jnp.int32),
            pltpu.VMEM((2, SC_CHUNK, width), src.dtype),
            pltpu.SemaphoreType.DMA((2,)),
            pltpu.SemaphoreType.DMA((2,)),
        ],
    )
    def move(src_hbm, idx_hbm, out_hbm, idx_v, rows_v, in_sem, out_sem):
        wid = lax.axis_index("s") * SC_CORES + lax.axis_index("c")
        base = wid * per_w
        pltpu.sync_copy(idx_hbm.at[wid], idx_v)

        def load(c, b):
            if scatter:
                rows = src_hbm.at[pl.ds(lax.rem(base, n_src) + c * SC_CHUNK, SC_CHUNK)]
            else:
                rows = src_hbm.at[idx_v.at[c]]
            return pltpu.make_async_copy(rows, rows_v.at[b], in_sem.at[b])

        def store(c, b):
            if scatter:
                rows = out_hbm.at[idx_v.at[c]]
            else:
                rows = out_hbm.at[pl.ds(base + c * SC_CHUNK, SC_CHUNK)]
            return pltpu.make_async_copy(rows_v.at[b], rows, out_sem.at[b])

        load(0, 0).start()

        @pl.loop(0, nchunk, step=2)
        def _(c0):
            for b in (0, 1):
                c = c0 + b
                load(c, b).wait()

                @pl.when(c + 1 < nchunk)
                def _():
                    @pl.when(c >= 1)
                    def _():
                        store(c - 1, 1 - b).wait()

                    load(c + 1, 1 - b).start()

                store(c, b).start()

        store(nchunk - 2, 0).wait()
        store(nchunk - 1, 1).wait()

    return move(src, idx.reshape(n_workers, nchunk, SC_CHUNK))


def _combine_kernel(x_ref, z0_ref, z1_ref, gates_ref, ada_ref, ng_ref, o_ref, *, final):
    gates = gates_ref[0]
    g0, g1 = gates[:, 0:1], gates[:, 1:2]
    z0_first, z0_second = _unpack_halves(z0_ref[...])
    z1_first, z1_second = _unpack_halves(z1_ref[...])
    y = jnp.concatenate([g0 * z0_first + g1 * z1_first, g0 * z0_second + g1 * z1_second], axis=1)
    x = x_ref[0] + ada_ref[0] * y
    if final:
        x = x * lax.rsqrt(jnp.mean(x * x, axis=-1, keepdims=True) + EPS) * ng_ref[...]
    o_ref[0] = x


def moe_combine(x, z, gates, gate_ada, norm_g, tm=512):
    final = norm_g is not None
    if not final:
        norm_g = jnp.ones((D_MODEL,), F32)
    nt = SEQ // tm
    return pl.pallas_call(
        functools.partial(_combine_kernel, final=final),
        grid=(BATCH, nt),
        in_specs=[
            pl.BlockSpec((1, tm, D_MODEL), lambda b, i: (b, i, 0)),
            pl.BlockSpec((tm, D_MODEL // 2), lambda b, i: (b * nt + i, 0)),
            pl.BlockSpec((tm, D_MODEL // 2), lambda b, i: (BATCH * nt + b * nt + i, 0)),
            pl.BlockSpec((1, tm, ROUTER_LANES), lambda b, i: (b, i, 0)),
            pl.BlockSpec((1, 1, D_MODEL), lambda b, i: (b, 0, 0)),
            pl.BlockSpec((1, D_MODEL), lambda b, i: (0, 0)),
        ],
        out_specs=pl.BlockSpec((1, tm, D_MODEL), lambda b, i: (b, i, 0)),
        out_shape=jax.ShapeDtypeStruct((BATCH, SEQ, D_MODEL), F32),
        compiler_params=_cparams(("parallel", "parallel")),
        name="moe_combine",
    )(x, z, z, gates, gate_ada, norm_g.reshape(1, D_MODEL))


def hierarchical_moe(x, c, norm_g, ada_w, ada_b, w_group, w_expert, w_gate_up, w_down, final_norm_g=None):
    shift, scale, gate_ada = ada_modulation(c, ada_w, ada_b)
    h, eid, gates = route(x, norm_g, scale, shift, _router_weights(w_group, w_expert))
    e_flat = jnp.concatenate([eid[:, 0, :].reshape(N_TOK), eid[:, 1, :].reshape(N_TOK)])
    dest, blk_expert, first, nxt, n_used = route_metadata(e_flat)
    xs = sc_move_rows(h.reshape(N_TOK, D_MODEL // 2), dest, MOE_ROWS, scatter=True)
    yb = expert_blocks(xs, blk_expert, n_used, first, nxt, w_gate_up, w_down)
    z = sc_move_rows(yb, dest, TOP_K * N_TOK, scatter=False)
    return moe_combine(x, z, gates, gate_ada, final_norm_g)


def kernel(x, c, norm0_mix, ada0_mix_w, ada0_mix_b, w_in0, conv_w, conv_b, conv_norm_g, conv_norm_b, w_out0, norm0_ffn, ada0_ffn_w, ada0_ffn_b, moe0_w_group, moe0_w_expert, moe0_w_gate_up, moe0_w_down, norm1_mix, ada1_mix_w, ada1_mix_b, w_in1, lru_conv_w, lru_conv_b, lru_w_a, lru_b_a, lru_w_x, lru_b_x, lru_lambda, w_out1, norm1_ffn, ada1_ffn_w, ada1_ffn_b, moe1_w_group, moe1_w_expert, moe1_w_gate_up, moe1_w_down, norm_final):
    x = conv_attention_layer(x, c, norm0_mix, ada0_mix_w, ada0_mix_b, w_in0, conv_w, conv_b, conv_norm_g, conv_norm_b,
                             w_out0)
    x = hierarchical_moe(x, c, norm0_ffn, ada0_ffn_w, ada0_ffn_b, moe0_w_group, moe0_w_expert,
                         moe0_w_gate_up, moe0_w_down)

    x = lru_mixer_layer(x, c, norm1_mix, ada1_mix_w, ada1_mix_b, w_in1, lru_conv_w, lru_conv_b, lru_w_a, lru_b_a,
                        lru_w_x, lru_b_x, lru_lambda, w_out1)
    x = hierarchical_moe(x, c, norm1_ffn, ada1_ffn_w, ada1_ffn_b, moe1_w_group, moe1_w_expert,
                         moe1_w_gate_up, moe1_w_down, final_norm_g=norm_final)
    return x
```

```python
import functools

import jax
import jax.numpy as jnp
from jax import lax
from jax.experimental import pallas as pl
from jax.experimental.pallas import tpu as pltpu
from jax.experimental.pallas import tpu_sc as plsc

F32 = jnp.float32
BF16 = jnp.bfloat16

D_MODEL = 2048
BATCH = 4
SEQ = 4096
N_TOK = BATCH * SEQ
EPS = 1e-6
NEG_INF = -1e30

CONV_CH = 1024
CONV_GROUPS = 8
CONV_WIDTH = 31
CONV_HALO = 16
CONV_COPY_ROWS = 512

ATT_HEADS = 8
HEAD_DIM = 128
ATT_W = ATT_HEADS * HEAD_DIM
ATT_BRANCHES = ((128, 1), (512, 4), (2048, 16))
ATT_RADIUS = 64
ATT_QB = 128
ATT_KB = ATT_QB + 2 * ATT_RADIUS
ATT_UNROLL = 8
ATT_COPY_ROWS = 512
ALIBI_MAX = 8.0

LRU_WIDTH = 2688
LRU_BLOCKS = 16
LRU_BW = LRU_WIDTH // LRU_BLOCKS
LRU_PW = 192
LRU_WP = LRU_BLOCKS * LRU_PW
LRU_GW = 2 * LRU_PW
LRU_CONV = 4
LRU_C = 8.0
LRU_CHUNKS = 8
LRU_CL = SEQ // LRU_CHUNKS
LRU_PITCH = LRU_CL + 8
LOG2E = 1.4426950408889634

N_GROUPS = 4
EXPERTS_PER_GROUP = 8
N_EXPERTS = 32
TOP_K = 2
EXPERT_FF = 1024
MOE_TB = 256
MOE_ROWS = N_TOK * TOP_K + N_EXPERTS * MOE_TB
MOE_NBLK = MOE_ROWS // MOE_TB
MOE_CAST_ROWS = 256
META_LANES = 256
assert MOE_NBLK <= META_LANES
SC_CORES = 2
SC_SUBCORES = 16
SC_CHUNK = 32
ROUTER_LANES = 128

MM_ROWS = 1024
VMEM_LIMIT = 48 * 1024 * 1024
BIG_VMEM_LIMIT = 56 * 1024 * 1024


def _cparams(sem):
    return pltpu.CompilerParams(dimension_semantics=sem, vmem_limit_bytes=VMEM_LIMIT)


def _sigmoid(x):
    return 0.5 * jnp.tanh(0.5 * x) + 0.5


def _ada_kernel(ct_ref, w_ref, b_ref, o_ref):
    ct = ct_ref[...]
    st = ct * _sigmoid(ct)
    w = w_ref[...]
    rows = [jnp.sum(w * st[:, b:b + 1], axis=0, keepdims=True) for b in range(BATCH)]
    o_ref[...] = jnp.concatenate(rows, axis=0) + b_ref[...]


def ada_modulation(c, w, b):
    tn = 512
    n = w.shape[1]
    mod = pl.pallas_call(
        _ada_kernel,
        grid=(n // tn,),
        in_specs=[
            pl.BlockSpec((D_MODEL, BATCH), lambda j: (0, 0)),
            pl.BlockSpec((D_MODEL, tn), lambda j: (0, j)),
            pl.BlockSpec((1, tn), lambda j: (0, j)),
        ],
        out_specs=pl.BlockSpec((BATCH, tn), lambda j: (0, j)),
        out_shape=jax.ShapeDtypeStruct((BATCH, n), F32),
        compiler_params=_cparams(("parallel",)),
        name="ada_modulation",
    )(c.T, w, b.reshape(1, n))
    shift, scale, gate = jnp.split(mod, 3, axis=-1)
    return shift[:, None, :], scale[:, None, :], gate[:, None, :]


def _modulated_norm(x, g, scale, shift):
    y = x * lax.rsqrt(jnp.mean(x * x, axis=-1, keepdims=True) + EPS)
    return (y * g) * (1.0 + scale) + shift


def _norm_kernel(x_ref, g_ref, sc_ref, sh_ref, o_ref):
    o_ref[0] = _modulated_norm(x_ref[0], g_ref[...], sc_ref[0], sh_ref[0]).astype(o_ref.dtype)


def modulated_norm(x, g, scale, shift, tm=512):
    return pl.pallas_call(
        _norm_kernel,
        grid=(BATCH, SEQ // tm),
        in_specs=[
            pl.BlockSpec((1, tm, D_MODEL), lambda b, i: (b, i, 0)),
            pl.BlockSpec((1, D_MODEL), lambda b, i: (0, 0)),
            pl.BlockSpec((1, 1, D_MODEL), lambda b, i: (b, 0, 0)),
            pl.BlockSpec((1, 1, D_MODEL), lambda b, i: (b, 0, 0)),
        ],
        out_specs=pl.BlockSpec((1, tm, D_MODEL), lambda b, i: (b, i, 0)),
        out_shape=jax.ShapeDtypeStruct((BATCH, SEQ, D_MODEL), BF16),
        compiler_params=_cparams(("parallel", "parallel")),
        name="modulated_norm",
    )(x, g.reshape(1, D_MODEL), scale, shift)


def _mm_kernel(a_ref, w_ref, o_ref, *, head_major):
    res = jnp.dot(a_ref[0], w_ref[...], preferred_element_type=F32)
    if head_major:
        for hh in range(ATT_HEADS):
            o_ref[0, 0, hh] = res[:, hh * HEAD_DIM:(hh + 1) * HEAD_DIM].astype(o_ref.dtype)
    else:
        o_ref[0] = res.astype(o_ref.dtype)


def project(a, w, out_dtype, head_major=False, tm=MM_ROWS, tn=1024):
    k, n = w.shape
    if head_major:
        assert tn == ATT_W
        out_shape = jax.ShapeDtypeStruct((n // tn, BATCH, ATT_HEADS, SEQ, HEAD_DIM), out_dtype)
        out_spec = pl.BlockSpec((1, 1, ATT_HEADS, tm, HEAD_DIM), lambda b, i, j: (j, b, 0, i, 0))
    else:
        out_shape = jax.ShapeDtypeStruct((BATCH, SEQ, n), out_dtype)
        out_spec = pl.BlockSpec((1, tm, tn), lambda b, i, j: (b, i, j))
    return pl.pallas_call(
        functools.partial(_mm_kernel, head_major=head_major),
        grid=(BATCH, SEQ // tm, n // tn),
        in_specs=[
            pl.BlockSpec((1, tm, k), lambda b, i, j: (b, i, 0)),
            pl.BlockSpec((k, tn), lambda b, i, j: (0, j)),
        ],
        out_specs=out_spec,
        out_shape=out_shape,
        compiler_params=_cparams(("parallel", "parallel", "parallel")),
        name="project",
    )(a, w)


def _mm_residual_kernel(*refs, n_in):
    a_refs, w_refs = refs[:n_in], refs[n_in:2 * n_in]
    x_ref, gate_ref, o_ref = refs[2 * n_in:]
    acc = jnp.dot(a_refs[0][0], w_refs[0][...], preferred_element_type=F32)
    for a_ref, w_ref in zip(a_refs[1:], w_refs[1:]):
        acc = acc + jnp.dot(a_ref[0], w_ref[...], preferred_element_type=F32)
    o_ref[0] = x_ref[0] + gate_ref[0] * acc


def project_residual(a_list, w_list, x, gate, tm=MM_ROWS, tn=1024):
    n_in = len(a_list)
    grid = (BATCH, SEQ // tm, D_MODEL // tn)
    in_specs = [pl.BlockSpec((1, tm, a.shape[-1]), lambda b, i, j: (b, i, 0)) for a in a_list]
    in_specs += [pl.BlockSpec((w.shape[0], tn), lambda b, i, j: (0, j)) for w in w_list]
    in_specs += [
        pl.BlockSpec((1, tm, tn), lambda b, i, j: (b, i, j)),
        pl.BlockSpec((1, 1, tn), lambda b, i, j: (b, 0, j)),
    ]
    return pl.pallas_call(
        functools.partial(_mm_residual_kernel, n_in=n_in),
        grid=grid,
        in_specs=in_specs,
        out_specs=pl.BlockSpec((1, tm, tn), lambda b, i, j: (b, i, j)),
        out_shape=jax.ShapeDtypeStruct((BATCH, SEQ, D_MODEL), F32),
        compiler_params=_cparams(("parallel", "parallel", "parallel")),
        name="project_residual",
    )(*a_list, *w_list, x, gate)


def _conv_kernel(v_ref, g_ref, cw_ref, cb_ref, ng_ref, nb_ref, o_ref, pad_scr, *, ts):
    zeros = jnp.zeros((CONV_HALO, 128), F32)
    pad_scr[0, 0:CONV_HALO, :] = zeros
    pad_scr[0, SEQ + CONV_HALO:SEQ + 2 * CONV_HALO, :] = zeros
    for t0 in range(0, SEQ, CONV_COPY_ROWS):
        rows = slice(t0, t0 + CONV_COPY_ROWS)
        pad_scr[0, CONV_HALO + t0:CONV_HALO + t0 + CONV_COPY_ROWS, :] = v_ref[0, rows, :] * _sigmoid(g_ref[0, rows, :])
    shifted_len = SEQ + 2 * CONV_HALO - 8
    for b in range(1, 8):
        for t0 in range(0, shifted_len, CONV_COPY_ROWS):
            n = min(CONV_COPY_ROWS, shifted_len - t0)
            pad_scr[b, t0:t0 + n, :] = pad_scr[0, t0 + b:t0 + b + n, :]
    first = CONV_HALO - CONV_WIDTH // 2

    def tap(t0, k):
        off = first + k
        return pad_scr[off % 8, t0 + off - off % 8:t0 + off - off % 8 + ts, :]

    for t0 in range(0, SEQ, ts):
        acc = cw_ref[0:1, :] * tap(t0, 0) + cb_ref[...]
        for k in range(1, CONV_WIDTH):
            acc = acc + cw_ref[k:k + 1, :] * tap(t0, k)
        mu = jnp.mean(acc, axis=-1, keepdims=True)
        cen = acc - mu
        var = jnp.mean(cen * cen, axis=-1, keepdims=True)
        y = cen * lax.rsqrt(var + EPS) * ng_ref[...] + nb_ref[...]
        o_ref[0, t0:t0 + ts, :] = (y * _sigmoid(y)).astype(o_ref.dtype)


def conv_module(p, conv_w, conv_b, norm_g, norm_b, ts=128):
    ng = CONV_GROUPS
    vec = lambda a: a.reshape(1, CONV_CH)
    vspec = pl.BlockSpec((1, 128), lambda b, c: (0, c))
    return pl.pallas_call(
        functools.partial(_conv_kernel, ts=ts),
        grid=(BATCH, ng),
        in_specs=[
            pl.BlockSpec((1, SEQ, 128), lambda b, c: (b, 0, c)),
            pl.BlockSpec((1, SEQ, 128), lambda b, c: (b, 0, c + ng)),
            pl.BlockSpec((CONV_WIDTH, 128), lambda b, c: (0, c)),
            vspec, vspec, vspec,
        ],
        out_specs=pl.BlockSpec((1, SEQ, 128), lambda b, c: (b, 0, c)),
        out_shape=jax.ShapeDtypeStruct((BATCH, SEQ, CONV_CH), BF16),
        scratch_shapes=[pltpu.VMEM((8, SEQ + 2 * CONV_HALO, 128), F32)],
        compiler_params=_cparams(("parallel", "parallel")),
        name="conv_module",
    )(p, p, conv_w, vec(conv_b), vec(norm_g), vec(norm_b))


def _attn_kernel(q_ref, k_ref, v_ref, *rest):
    nbr = len(ATT_BRANCHES)
    bias_refs, o_ref = rest[:nbr], rest[nbr]
    qf, kf, vf, qd, kpad, vpad, acc, mx, den = rest[nbr + 1:]
    scale = HEAD_DIM ** -0.5
    zeros = jnp.zeros((ATT_RADIUS, HEAD_DIM), BF16)
    for t0 in range(0, SEQ, ATT_COPY_ROWS):
        rows = slice(t0, t0 + ATT_COPY_ROWS)
        qf[rows, :] = q_ref[0, 0, rows, :].astype(F32)
        kf[rows, :] = k_ref[0, 0, rows, :].astype(F32)
        vf[rows, :] = v_ref[0, 0, rows, :].astype(F32)

    order = sorted(range(nbr), key=lambda n: -ATT_BRANCHES[n][1])
    for pos, n in enumerate(order):
        dil = ATT_BRANCHES[n][1]
        bias_ref = bias_refs[n]
        sub = SEQ // dil
        nblk = sub // ATT_QB
        first, last = pos == 0, pos == nbr - 1
        assert dil == 1 or not last

        def class_rows(r, start, count, dil=dil):
            if dil == 1:
                return pl.ds(start, count)
            return pl.ds(r + start * dil, count, stride=dil)

        group = max(1, min(dil, ATT_UNROLL // nblk))
        kstride = sub + 2 * ATT_RADIUS

        def residue_classes(g, carry, dil=dil, sub=sub, nblk=nblk, bias_ref=bias_ref, first=first, last=last,
                            class_rows=class_rows, group=group, kstride=kstride):
            for cc in range(group):
                r = g * group + cc
                k0 = cc * kstride
                for pad in (kpad, vpad):
                    pad[k0:k0 + ATT_RADIUS, :] = zeros
                    pad[k0 + sub + ATT_RADIUS:k0 + kstride, :] = zeros
                for c0 in range(0, sub, ATT_COPY_ROWS):
                    n_rows = min(ATT_COPY_ROWS, sub - c0)
                    src = class_rows(r, c0, n_rows)
                    qd[cc * sub + c0:cc * sub + c0 + n_rows, :] = qf[src, :].astype(BF16)
                    kpad[k0 + ATT_RADIUS + c0:k0 + ATT_RADIUS + c0 + n_rows, :] = kf[src, :].astype(BF16)
                    vpad[k0 + ATT_RADIUS + c0:k0 + ATT_RADIUS + c0 + n_rows, :] = vf[src, :].astype(BF16)

            def block(i, r, cc):
                if isinstance(i, int):
                    q0 = i * ATT_QB
                    variant = 0 if i == 0 else (2 if i == nblk - 1 else 1)
                else:
                    q0 = pl.multiple_of(i * ATT_QB, ATT_QB)
                    variant = jnp.where(i == 0, 0, jnp.where(i == nblk - 1, 2, 1))
                s = lax.dot_general(qd[pl.ds(cc * sub + q0, ATT_QB), :], kpad[pl.ds(cc * kstride + q0, ATT_KB), :],
                                    (((1,), (1,)), ((), ())), preferred_element_type=F32)
                s = s * scale + bias_ref[0, variant]
                m = jnp.max(s, axis=-1, keepdims=True)
                e = jnp.exp(s - m)
                l = jnp.sum(e, axis=-1, keepdims=True)
                o = jnp.dot(e.astype(BF16), vpad[pl.ds(cc * kstride + q0, ATT_KB), :], preferred_element_type=F32)
                nat = class_rows(r, q0, ATT_QB)
                wide = (ATT_QB, HEAD_DIM)
                if first:
                    acc[nat, :] = o
                    mx[nat, :] = jnp.broadcast_to(m, wide)
                    den[nat, :] = jnp.broadcast_to(l, wide)
                else:
                    m_old = mx[nat, :]
                    m_new = jnp.maximum(m_old, m)
                    alpha = jnp.exp(m_old - m_new)
                    beta = jnp.exp(m - m_new)
                    total = acc[nat, :] * alpha + o * beta
                    l_new = den[nat, :] * alpha + l * beta
                    if last:
                        o_ref[0, nat, :] = (total / l_new).astype(o_ref.dtype)
                    else:
                        acc[nat, :] = total
                        mx[nat, :] = m_new
                        den[nat, :] = l_new

            if group == 1:
                def body(i, carry):
                    block(i, g, 0)
                    return carry

                lax.fori_loop(0, nblk, body, 0, unroll=min(nblk, ATT_UNROLL))
            else:
                for cc in range(group):
                    for i in range(nblk):
                        block(i, g * group + cc, cc)
            return carry

        if dil == 1:
            residue_classes(0, 0)
        else:
            lax.fori_loop(0, dil // group, residue_classes, 0)


def _alibi_bias(dilation):
    row = jnp.arange(ATT_QB)[:, None]
    colm = jnp.arange(ATT_KB)[None, :]
    rel = jnp.abs(colm - ATT_RADIUS - row)
    slopes = 2.0 ** (-ALIBI_MAX * jnp.arange(1, ATT_HEADS + 1, dtype=F32) / ATT_HEADS)
    bias = -slopes[:, None, None] * (rel * dilation).astype(F32)[None]
    band = rel <= ATT_RADIUS
    masks = jnp.stack([band & (colm >= ATT_RADIUS), band, band & (colm < ATT_KB - ATT_RADIUS)])
    return jnp.where(masks[None], bias[:, None], NEG_INF)


def dilated_attention(qkv):
    for window, dilation in ATT_BRANCHES:
        assert window // (2 * dilation) == ATT_RADIUS
        assert SEQ // dilation >= 2 * ATT_QB
    qspec = lambda which: pl.BlockSpec((None, 1, 1, SEQ, HEAD_DIM), lambda b, h, which=which: (which, b, h, 0, 0))
    bias_spec = pl.BlockSpec((1, 3, ATT_QB, ATT_KB), lambda b, h: (h, 0, 0, 0))
    rows = lambda dtype, n=SEQ: pltpu.VMEM((n, HEAD_DIM), dtype)
    return pl.pallas_call(
        _attn_kernel,
        grid=(BATCH, ATT_HEADS),
        in_specs=[qspec(0), qspec(1), qspec(2)] + [bias_spec] * len(ATT_BRANCHES),
        out_specs=pl.BlockSpec((1, SEQ, HEAD_DIM), lambda b, h: (b, 0, h)),
        out_shape=jax.ShapeDtypeStruct((BATCH, SEQ, ATT_W), BF16),
        scratch_shapes=[rows(F32), rows(F32), rows(F32), rows(BF16),
                        rows(BF16, SEQ + 2 * ATT_RADIUS), rows(BF16, SEQ + 2 * ATT_RADIUS),
                        rows(F32), rows(F32), rows(F32)],
        compiler_params=_cparams(("parallel", "parallel")),
        name="dilated_attention",
    )(qkv, qkv, qkv, *[_alibi_bias(dilation) for _, dilation in ATT_BRANCHES])


def conv_attention_layer(x, c, norm_g, ada_w, ada_b, w_in, conv_w, conv_b, conv_norm_g, conv_norm_b, w_out):
    shift, scale, gate = ada_modulation(c, ada_w, ada_b)
    w_in = w_in.astype(BF16)
    h = modulated_norm(x, norm_g, scale, shift)
    p_conv = project(h, w_in[:, :2 * CONV_CH], F32)
    qkv = project(h, w_in[:, 2 * CONV_CH:], BF16, head_major=True)
    a = conv_module(p_conv, conv_w, conv_b, conv_norm_g, conv_norm_b)
    o = dilated_attention(qkv)
    w_out = w_out.astype(BF16)
    return project_residual([a, o], [w_out[:CONV_CH], w_out[CONV_CH:]], x, gate)


def _softplus(x):
    return jnp.maximum(x, 0.0) + jnp.log1p(jnp.exp(-jnp.abs(x)))


def _gelu_tanh(x):
    return 0.5 * x * (1.0 + jnp.tanh(0.7978845608028654 * (x + 0.044715 * (x * x * x))))


def _lru_kernel(x_ref, cw_ref, cb_ref, w_ref, b_ref, lam_ref, o_ref, xc, a_scr, u_scr, yacc):
    gw = LRU_GW
    ntile = gw // 128
    lead = LRU_CONV // 2

    def shifted(r0, off):
        lo, hi = r0 + off, r0 + off + LRU_CL
        parts = []
        if lo < 0:
            parts.append(jnp.zeros((-lo, gw), F32))
        parts.append(x_ref[0, max(lo, 0):min(hi, SEQ), :])
        if hi > SEQ:
            parts.append(jnp.zeros((hi - SEQ, gw), F32))
        return parts[0] if len(parts) == 1 else jnp.concatenate(parts, axis=0)

    for j in range(LRU_CHUNKS):
        r0 = j * LRU_CL
        acc = cw_ref[0:1, :] * shifted(r0, -lead) + cb_ref[...]
        for k in range(1, LRU_CONV):
            acc = acc + cw_ref[k:k + 1, :] * shifted(r0, k - lead)
        xc[r0:r0 + LRU_CL, :] = acc

    for d in range(2):
        reverse = d == 1
        half_decay = (0.5 * LRU_C) * _softplus(-lam_ref[d:d + 1, :])
        for j in range(LRU_CHUNKS):
            xj = xc[j * LRU_CL:(j + 1) * LRU_CL, :]
            pre = jnp.dot(xj.astype(BF16), w_ref[d, 0], preferred_element_type=F32) + b_ref[d, 0]
            t_r = jnp.tanh(pre[:, :gw])
            t_i = jnp.tanh(pre[:, gw:])
            neg_log_a = t_r * half_decay + half_decay
            a = jnp.exp2(neg_log_a * (-LOG2E))
            one_minus_a2 = jnp.tanh(neg_log_a) * (a * a + 1.0)
            root = one_minus_a2 * lax.rsqrt(jnp.maximum(one_minus_a2, 1e-30))
            u = root * ((0.5 * t_i + 0.5) * xj)
            for lt in range(ntile):
                rows = slice(j * LRU_PITCH, j * LRU_PITCH + LRU_CL)
                a_scr[lt, rows, :] = a[:, lt * 128:(lt + 1) * 128]
                u_scr[lt, rows, :] = u[:, lt * 128:(lt + 1) * 128]

        def step(ii, carry):
            row = (LRU_CL - 1 - ii) if reverse else ii
            out = []
            for lt in range(ntile):
                h, prod = carry[lt]
                a = a_scr[lt, pl.ds(row, LRU_CHUNKS, stride=LRU_PITCH), :]
                h = a * h + u_scr[lt, pl.ds(row, LRU_CHUNKS, stride=LRU_PITCH), :]
                prod = a * prod
                u_scr[lt, pl.ds(row, LRU_CHUNKS, stride=LRU_PITCH), :] = h
                a_scr[lt, pl.ds(row, LRU_CHUNKS, stride=LRU_PITCH), :] = prod
                out.append((h, prod))
            return tuple(out)

        init = tuple((jnp.zeros((LRU_CHUNKS, 128), F32), jnp.ones((LRU_CHUNKS, 128), F32)) for _ in range(ntile))
        lax.fori_loop(0, LRU_CL, step, init, unroll=4)

        last = 0 if reverse else LRU_CL - 1
        for lt in range(ntile):
            lanes = slice(lt * 128, (lt + 1) * 128)
            h_end = u_scr[lt, pl.ds(last, LRU_CHUNKS, stride=LRU_PITCH), :]
            p_end = a_scr[lt, pl.ds(last, LRU_CHUNKS, stride=LRU_PITCH), :]
            carry = jnp.zeros((1, 128), F32)
            for j in (range(LRU_CHUNKS - 1, -1, -1) if reverse else range(LRU_CHUNKS)):
                src = slice(j * LRU_PITCH, j * LRU_PITCH + LRU_CL)
                dst = slice(j * LRU_CL, (j + 1) * LRU_CL)
                y = u_scr[lt, src, :] + a_scr[lt, src, :] * carry
                if reverse:
                    o_ref[0, dst, lanes] = (yacc[dst, lanes] + y).astype(o_ref.dtype)
                else:
                    yacc[dst, lanes] = y
                carry = h_end[j:j + 1, :] + p_end[j:j + 1, :] * carry


def rglru_block(p, conv_w, conv_b, w_gates, b_gates, lam):
    npair = LRU_WP // LRU_GW
    gw = LRU_GW
    big = lambda: pltpu.VMEM((SEQ, gw), F32)
    slabs = lambda: pltpu.VMEM((gw // 128, LRU_CHUNKS * LRU_PITCH, 128), F32)
    return pl.pallas_call(
        _lru_kernel,
        grid=(BATCH, npair),
        in_specs=[
            pl.BlockSpec((1, SEQ, gw), lambda b, n: (b, 0, n + npair)),
            pl.BlockSpec((LRU_CONV, gw), lambda b, n: (0, n)),
            pl.BlockSpec((1, gw), lambda b, n: (0, n)),
            pl.BlockSpec((2, 1, gw, 2 * gw), lambda b, n: (0, n, 0, 0)),
            pl.BlockSpec((2, 1, 1, 2 * gw), lambda b, n: (0, n, 0, 0)),
            pl.BlockSpec((2, gw), lambda b, n: (0, n)),
        ],
        out_specs=pl.BlockSpec((1, SEQ, gw), lambda b, n: (b, 0, n)),
        out_shape=jax.ShapeDtypeStruct((BATCH, SEQ, LRU_WP), BF16),
        scratch_shapes=[big(), slabs(), slabs(), big()],
        compiler_params=pltpu.CompilerParams(dimension_semantics=("parallel", "parallel"),
                                             vmem_limit_bytes=BIG_VMEM_LIMIT),
        name="rglru_block",
    )(p, conv_w, conv_b, w_gates, b_gates, lam)


def _gelu_gate_kernel(y_ref, gate_ref, o_ref):
    o_ref[0] = (_gelu_tanh(gate_ref[0]) * y_ref[0].astype(F32)).astype(o_ref.dtype)


def gelu_gate(y, p, tm=512):
    spec = pl.BlockSpec((1, tm, LRU_WP), lambda b, i: (b, i, 0))
    return pl.pallas_call(
        _gelu_gate_kernel,
        grid=(BATCH, SEQ // tm),
        in_specs=[spec, spec],
        out_specs=spec,
        out_shape=jax.ShapeDtypeStruct((BATCH, SEQ, LRU_WP), BF16),
        compiler_params=_cparams(("parallel", "parallel")),
        name="gelu_gate",
    )(y, p)


def _pad_blocks(a):
    lead = a.shape[:-1]
    a = a.reshape(*lead, LRU_BLOCKS, LRU_BW)
    a = jnp.pad(a, [(0, 0)] * len(lead) + [(0, 0), (0, LRU_PW - LRU_BW)])
    return a.reshape(*lead, LRU_WP)


def _pair_block_diagonal(w):
    pad = LRU_PW - LRU_BW
    w = jnp.pad(w, ((0, 0), (0, 0), (0, pad), (0, pad))).reshape(2, LRU_BLOCKS // 2, 2, LRU_PW, LRU_PW)
    zero = jnp.zeros_like(w[:, :, 0])
    top = jnp.concatenate([w[:, :, 0], zero], axis=-1)
    bottom = jnp.concatenate([zero, w[:, :, 1]], axis=-1)
    return jnp.concatenate([top, bottom], axis=-2)


def lru_mixer_layer(x, c, norm_g, ada_w, ada_b, w_in, conv_w, conv_b, w_a, b_a, w_x, b_x, lam, w_out):
    shift, scale, gate = ada_modulation(c, ada_w, ada_b)
    w_in_p = jnp.concatenate([_pad_blocks(w_in[:, :LRU_WIDTH]), _pad_blocks(w_in[:, LRU_WIDTH:])], axis=1)
    p = project(modulated_norm(x, norm_g, scale, shift), w_in_p.astype(BF16), F32)
    npair = LRU_WP // LRU_GW
    w_gates = jnp.concatenate([_pair_block_diagonal(w_a), _pair_block_diagonal(w_x)], axis=-1)
    b_gates = jnp.concatenate([_pad_blocks(b_a).reshape(2, npair, 1, LRU_GW),
                               _pad_blocks(b_x).reshape(2, npair, 1, LRU_GW)], axis=-1)
    y = rglru_block(p, _pad_blocks(conv_w), _pad_blocks(conv_b).reshape(1, LRU_WP),
                    (0.5 * w_gates).astype(BF16), 0.5 * b_gates, _pad_blocks(lam))
    w_out_p = jnp.pad(w_out.reshape(LRU_BLOCKS, LRU_BW, D_MODEL), ((0, 0), (0, LRU_PW - LRU_BW), (0, 0)))
    return project_residual([gelu_gate(y, p)], [w_out_p.reshape(LRU_WP, D_MODEL).astype(BF16)], x, gate)


def _pack_halves(v):
    half = v.shape[1] // 2
    as_bits = lambda t: pltpu.bitcast(t.astype(BF16).astype(F32), jnp.uint32)
    word = as_bits(v[:, :half]) | (as_bits(v[:, half:]) >> 16)
    return pltpu.bitcast(word, jnp.int32)


def _unpack_halves(word):
    bits = pltpu.bitcast(word, jnp.uint32)
    first = pltpu.bitcast(bits & jnp.uint32(0xFFFF0000), F32)
    second = pltpu.bitcast(bits << 16, F32)
    return first, second


def _router_kernel(x_ref, g_ref, sc_ref, sh_ref, wr_ref, h_ref, eid_ref, gt_ref):
    h = _modulated_norm(x_ref[0], g_ref[...], sc_ref[0], sh_ref[0])
    hb = h.astype(BF16)
    h_ref[0] = _pack_halves(h)
    h_lo = (h - hb.astype(F32)).astype(BF16)
    w = wr_ref[...]
    w_hi = w.astype(BF16)
    w_lo = (w - w_hi.astype(F32)).astype(BF16)
    lg = (jnp.dot(hb, w_hi, preferred_element_type=F32) + jnp.dot(h_lo, w_hi, preferred_element_type=F32)
          + jnp.dot(hb, w_lo, preferred_element_type=F32))
    lt = lg.T
    tm = lt.shape[1]
    row = lax.broadcasted_iota(jnp.int32, (8, tm), 0)
    big = jnp.int32(99)
    gl = jnp.where(row < N_GROUPS, lt[0:8], -jnp.inf)
    g_max = jnp.max(gl, axis=0, keepdims=True)
    g_sel = jnp.min(jnp.where(gl == g_max, row, big), axis=0, keepdims=True)
    g_prob = 1.0 / jnp.sum(jnp.exp(gl - g_max), axis=0, keepdims=True)
    el = jnp.zeros((8, tm), F32)
    for g in range(N_GROUPS):
        el = jnp.where(g_sel == g, lt[8 + 8 * g:16 + 8 * g], el)
    v1 = jnp.max(el, axis=0, keepdims=True)
    i1 = jnp.min(jnp.where(el == v1, row, big), axis=0, keepdims=True)
    el2 = jnp.where(row == i1, -jnp.inf, el)
    v2 = jnp.max(el2, axis=0, keepdims=True)
    i2 = jnp.min(jnp.where(el2 == v2, row, big), axis=0, keepdims=True)
    e2 = jnp.exp(v2 - v1)
    p1 = 1.0 / (1.0 + e2)
    p2 = e2 * p1
    eid_ref[0] = jnp.where(row == 0, g_sel * EXPERTS_PER_GROUP + i1,
                           jnp.where(row == 1, g_sel * EXPERTS_PER_GROUP + i2, 0))
    rows = lax.broadcasted_iota(jnp.int32, (ROUTER_LANES, tm), 0)
    gates = jnp.where(rows == 0, g_prob * p1, jnp.where(rows == 1, g_prob * p2, 0.0))
    gt_ref[0] = gates.T


def route(x, g, scale, shift, w_router, tm=512):
    return pl.pallas_call(
        _router_kernel,
        grid=(BATCH, SEQ // tm),
        in_specs=[
            pl.BlockSpec((1, tm, D_MODEL), lambda b, i: (b, i, 0)),
            pl.BlockSpec((1, D_MODEL), lambda b, i: (0, 0)),
            pl.BlockSpec((1, 1, D_MODEL), lambda b, i: (b, 0, 0)),
            pl.BlockSpec((1, 1, D_MODEL), lambda b, i: (b, 0, 0)),
            pl.BlockSpec((D_MODEL, ROUTER_LANES), lambda b, i: (0, 0)),
        ],
        out_specs=[
            pl.BlockSpec((1, tm, D_MODEL // 2), lambda b, i: (b, i, 0)),
            pl.BlockSpec((1, 8, tm), lambda b, i: (b, 0, i)),
            pl.BlockSpec((1, tm, ROUTER_LANES), lambda b, i: (b, i, 0)),
        ],
        out_shape=[
            jax.ShapeDtypeStruct((BATCH, SEQ, D_MODEL // 2), jnp.int32),
            jax.ShapeDtypeStruct((BATCH, 8, SEQ), jnp.int32),
            jax.ShapeDtypeStruct((BATCH, SEQ, ROUTER_LANES), F32),
        ],
        compiler_params=_cparams(("parallel", "parallel")),
        name="route",
    )(x, g.reshape(1, D_MODEL), scale, shift, w_router)


def _expert_kernel(be_ref, nu_ref, first_ref, nxt_ref, x_ref, wgu_hbm, wd_hbm, o_ref,
                   gu_stage, d_stage, wgu, wd, sem):
    i = pl.program_id(0)
    active = i < nu_ref[0]

    def weight_copies(e):
        return (pltpu.make_async_copy(wgu_hbm.at[e], gu_stage, sem.at[0]),
                pltpu.make_async_copy(wd_hbm.at[e], d_stage, sem.at[1]))

    @pl.when(i == 0)
    def _():
        for cp in weight_copies(be_ref[0]):
            cp.start()

    @pl.when(active & (first_ref[i] == 1))
    def _():
        for cp in weight_copies(be_ref[i]):
            cp.wait()

        def cast_rows(stage, dst):
            def body(c, carry):
                r0 = pl.multiple_of(c * MOE_CAST_ROWS, MOE_CAST_ROWS)
                dst[pl.ds(r0, MOE_CAST_ROWS), :] = stage[pl.ds(r0, MOE_CAST_ROWS), :].astype(BF16)
                return carry
            lax.fori_loop(0, stage.shape[0] // MOE_CAST_ROWS, body, 0)

        cast_rows(gu_stage, wgu)
        cast_rows(d_stage, wd)

        @pl.when(nxt_ref[i] >= 0)
        def _():
            for cp in weight_copies(nxt_ref[i]):
                cp.start()

    @pl.when(active)
    def _():
        x_first, x_second = _unpack_halves(x_ref[...])
        half = D_MODEL // 2
        gu = (jnp.dot(x_first.astype(BF16), wgu[0:half, :], preferred_element_type=F32)
              + jnp.dot(x_second.astype(BF16), wgu[half:, :], preferred_element_type=F32))
        g = gu[:, :EXPERT_FF]
        u = gu[:, EXPERT_FF:]
        act = (g * _sigmoid(g) * u).astype(BF16)
        o_ref[...] = _pack_halves(jnp.dot(act, wd[...], preferred_element_type=F32))

    @pl.when(jnp.logical_not(active))
    def _():
        o_ref[...] = jnp.zeros(o_ref.shape, o_ref.dtype)


def expert_blocks(xs, blk_expert, n_used, first, nxt, w_gate_up, w_down):
    row_map = lambda i, be, nu, first, nxt: (jnp.minimum(i, nu[0] - 1), 0)
    return pl.pallas_call(
        _expert_kernel,
        grid_spec=pltpu.PrefetchScalarGridSpec(
            num_scalar_prefetch=4,
            grid=(MOE_NBLK,),
            in_specs=[
                pl.BlockSpec((MOE_TB, D_MODEL // 2), row_map),
                pl.BlockSpec(memory_space=pl.ANY),
                pl.BlockSpec(memory_space=pl.ANY),
            ],
            out_specs=pl.BlockSpec((MOE_TB, D_MODEL // 2), lambda i, be, nu, first, nxt: (i, 0)),
            scratch_shapes=[
                pltpu.VMEM((D_MODEL, 2 * EXPERT_FF), F32),
                pltpu.VMEM((EXPERT_FF, D_MODEL), F32),
                pltpu.VMEM((D_MODEL, 2 * EXPERT_FF), BF16),
                pltpu.VMEM((EXPERT_FF, D_MODEL), BF16),
                pltpu.SemaphoreType.DMA((2,)),
            ],
        ),
        out_shape=jax.ShapeDtypeStruct((MOE_ROWS, D_MODEL // 2), jnp.int32),
        compiler_params=pltpu.CompilerParams(dimension_semantics=("arbitrary",), vmem_limit_bytes=BIG_VMEM_LIMIT),
        name="expert_blocks",
    )(blk_expert, n_used, first, nxt, xs, w_gate_up, w_down)


def _router_weights(w_group, w_expert):
    w = jnp.zeros((D_MODEL, ROUTER_LANES), F32)
    w = w.at[:, 0:N_GROUPS].set(w_group)
    return w.at[:, 8:8 + N_EXPERTS].set(w_expert)


def _route_meta_kernel(e_ref, dest_ref, be_ref, first_ref, nxt_ref, nu_ref):
    nrow = e_ref.shape[0]
    expert = lax.broadcasted_iota(jnp.int32, (N_EXPERTS, META_LANES), 0)
    upto = (lax.broadcasted_iota(jnp.int32, (META_LANES, META_LANES), 0)
            <= lax.broadcasted_iota(jnp.int32, (META_LANES, META_LANES), 1)).astype(BF16)

    def count_row(c, acc):
        return acc + jnp.where(expert == e_ref[c], 1.0, 0.0)

    acc = lax.fori_loop(0, nrow, count_row, jnp.zeros((N_EXPERTS, META_LANES), F32))
    counts = jnp.sum(acc, axis=1, keepdims=True)
    pcounts = jnp.floor((counts + (MOE_TB - 1.0)) * (1.0 / MOE_TB)) * MOE_TB
    ends = []
    run = jnp.zeros((1, 1), F32)
    for e in range(N_EXPERTS):
        run = run + pcounts[e:e + 1, :]
        ends.append(run)
    pend = jnp.concatenate(ends, axis=0)
    pstart = pend - pcounts

    def dest_row(c, running):
        hit = expert == e_ref[c]
        seen = jnp.dot(jnp.where(hit, 1.0, 0.0).astype(BF16), upto, preferred_element_type=F32)
        slot = seen - 1.0 + (running + pstart)
        dest_ref[c] = jnp.sum(jnp.where(hit, slot, 0.0), axis=0, keepdims=True).astype(jnp.int32)
        return running + seen[:, META_LANES - 1:META_LANES]

    lax.fori_loop(0, nrow, dest_row, jnp.zeros((N_EXPERTS, 1), F32))

    blk = lax.broadcasted_iota(jnp.int32, (1, META_LANES), 1).astype(F32)
    n_used = pend[N_EXPERTS - 1:N_EXPERTS, :] * (1.0 / MOE_TB)

    def expert_at(b):
        start = jnp.minimum(b, n_used - 1.0) * MOE_TB
        return jnp.minimum(jnp.sum(jnp.where(pend <= start, 1.0, 0.0), axis=0, keepdims=True), N_EXPERTS - 1.0)

    be = expert_at(blk)
    is_first = jnp.logical_and(jnp.logical_or(blk == 0.0, be != expert_at(blk - 1.0)), blk < n_used)
    later = jnp.logical_and(expert.astype(F32) > be, counts > 0.0)
    nxt = jnp.min(jnp.where(later, expert.astype(F32), 2.0 * N_EXPERTS), axis=0, keepdims=True)
    be_ref[...] = be.astype(jnp.int32)
    first_ref[...] = jnp.where(is_first, 1, 0)
    nxt_ref[...] = jnp.where(nxt < N_EXPERTS, nxt, -1.0).astype(jnp.int32)
    nu_ref[...] = jnp.broadcast_to(n_used, (1, META_LANES)).astype(jnp.int32)


def route_metadata(e_flat):
    nrow = e_flat.shape[0] // META_LANES
    lane_row = jax.ShapeDtypeStruct((1, META_LANES), jnp.int32)
    dest, be, first, nxt, nu = pl.pallas_call(
        _route_meta_kernel,
        out_shape=[jax.ShapeDtypeStruct((nrow, 1, META_LANES), jnp.int32), lane_row, lane_row, lane_row, lane_row],
        compiler_params=pltpu.CompilerParams(vmem_limit_bytes=VMEM_LIMIT),
        name="route_metadata",
    )(e_flat.reshape(nrow, 1, META_LANES))
    return dest.reshape(-1), be[0, :MOE_NBLK], first[0, :MOE_NBLK], nxt[0, :MOE_NBLK], nu[0, :1]


def sc_move_rows(src, idx, n_out, scatter):
    n_idx = idx.shape[0]
    n_src, width = src.shape
    n_workers = SC_CORES * SC_SUBCORES
    per_w = n_idx // n_workers
    nchunk = per_w // SC_CHUNK
    assert per_w * n_workers == n_idx and nchunk * SC_CHUNK == per_w and nchunk % 2 == 0
    assert per_w % n_src == 0 or n_src % per_w == 0
    mesh = plsc.VectorSubcoreMesh(core_axis_name="c", subcore_axis_name="s")

    @functools.partial(
        pl.kernel, mesh=mesh,
        out_type=jax.ShapeDtypeStruct((n_out, width), src.dtype),
        scratch_types=[
            pltpu.VMEM((nchunk, SC_CHUNK), jnp.int32),
            pltpu.VMEM((2, SC_CHUNK, width), src.dtype),
            pltpu.SemaphoreType.DMA((2,)),
            pltpu.SemaphoreType.DMA((2,)),
        ],
    )
    def move(src_hbm, idx_hbm, out_hbm, idx_v, rows_v, in_sem, out_sem):
        wid = lax.axis_index("s") * SC_CORES + lax.axis_index("c")
        base = wid * per_w
        pltpu.sync_copy(idx_hbm.at[wid], idx_v)

        def load(c, b):
            if scatter:
                rows = src_hbm.at[pl.ds(lax.rem(base, n_src) + c * SC_CHUNK, SC_CHUNK)]
            else:
                rows = src_hbm.at[idx_v.at[c]]
            return pltpu.make_async_copy(rows, rows_v.at[b], in_sem.at[b])

        def store(c, b):
            if scatter:
                rows = out_hbm.at[idx_v.at[c]]
            else:
                rows = out_hbm.at[pl.ds(base + c * SC_CHUNK, SC_CHUNK)]
            return pltpu.make_async_copy(rows_v.at[b], rows, out_sem.at[b])

        load(0, 0).start()

        @pl.loop(0, nchunk, step=2)
        def _(c0):
            for b in (0, 1):
                c = c0 + b
                load(c, b).wait()

                @pl.when(c + 1 < nchunk)
                def _():
                    @pl.when(c >= 1)
                    def _():
                        store(c - 1, 1 - b).wait()

                    load(c + 1, 1 - b).start()

                store(c, b).start()

        store(nchunk - 2, 0).wait()
        store(nchunk - 1, 1).wait()

    return move(src, idx.reshape(n_workers, nchunk, SC_CHUNK))


def _combine_kernel(x_ref, z0_ref, z1_ref, gates_ref, ada_ref, ng_ref, o_ref, *, final):
    gates = gates_ref[0]
    g0, g1 = gates[:, 0:1], gates[:, 1:2]
    z0_first, z0_second = _unpack_halves(z0_ref[...])
    z1_first, z1_second = _unpack_halves(z1_ref[...])
    y = jnp.concatenate([g0 * z0_first + g1 * z1_first, g0 * z0_second + g1 * z1_second], axis=1)
    x = x_ref[0] + ada_ref[0] * y
    if final:
        x = x * lax.rsqrt(jnp.mean(x * x, axis=-1, keepdims=True) + EPS) * ng_ref[...]
    o_ref[0] = x


def moe_combine(x, z, gates, gate_ada, norm_g, tm=512):
    final = norm_g is not None
    if not final:
        norm_g = jnp.ones((D_MODEL,), F32)
    nt = SEQ // tm
    return pl.pallas_call(
        functools.partial(_combine_kernel, final=final),
        grid=(BATCH, nt),
        in_specs=[
            pl.BlockSpec((1, tm, D_MODEL), lambda b, i: (b, i, 0)),
            pl.BlockSpec((tm, D_MODEL // 2), lambda b, i: (b * nt + i, 0)),
            pl.BlockSpec((tm, D_MODEL // 2), lambda b, i: (BATCH * nt + b * nt + i, 0)),
            pl.BlockSpec((1, tm, ROUTER_LANES), lambda b, i: (b, i, 0)),
            pl.BlockSpec((1, 1, D_MODEL), lambda b, i: (b, 0, 0)),
            pl.BlockSpec((1, D_MODEL), lambda b, i: (0, 0)),
        ],
        out_specs=pl.BlockSpec((1, tm, D_MODEL), lambda b, i: (b, i, 0)),
        out_shape=jax.ShapeDtypeStruct((BATCH, SEQ, D_MODEL), F32),
        compiler_params=_cparams(("parallel", "parallel")),
        name="moe_combine",
    )(x, z, z, gates, gate_ada, norm_g.reshape(1, D_MODEL))


def hierarchical_moe(x, c, norm_g, ada_w, ada_b, w_group, w_expert, w_gate_up, w_down, final_norm_g=None):
    shift, scale, gate_ada = ada_modulation(c, ada_w, ada_b)
    h, eid, gates = route(x, norm_g, scale, shift, _router_weights(w_group, w_expert))
    e_flat = jnp.concatenate([eid[:, 0, :].reshape(N_TOK), eid[:, 1, :].reshape(N_TOK)])
    dest, blk_expert, first, nxt, n_used = route_metadata(e_flat)
    xs = sc_move_rows(h.reshape(N_TOK, D_MODEL // 2), dest, MOE_ROWS, scatter=True)
    yb = expert_blocks(xs, blk_expert, n_used, first, nxt, w_gate_up, w_down)
    z = sc_move_rows(yb, dest, TOP_K * N_TOK, scatter=False)
    return moe_combine(x, z, gates, gate_ada, final_norm_g)


def kernel(x, c, norm0_mix, ada0_mix_w, ada0_mix_b, w_in0, conv_w, conv_b, conv_norm_g, conv_norm_b, w_out0, norm0_ffn, ada0_ffn_w, ada0_ffn_b, moe0_w_group, moe0_w_expert, moe0_w_gate_up, moe0_w_down, norm1_mix, ada1_mix_w, ada1_mix_b, w_in1, lru_conv_w, lru_conv_b, lru_w_a, lru_b_a, lru_w_x, lru_b_x, lru_lambda, w_out1, norm1_ffn, ada1_ffn_w, ada1_ffn_b, moe1_w_group, moe1_w_expert, moe1_w_gate_up, moe1_w_down, norm_final):
    x = conv_attention_layer(x, c, norm0_mix, ada0_mix_w, ada0_mix_b, w_in0, conv_w, conv_b, conv_norm_g, conv_norm_b,
                             w_out0)
    x = hierarchical_moe(x, c, norm0_ffn, ada0_ffn_w, ada0_ffn_b, moe0_w_group, moe0_w_expert,
                         moe0_w_gate_up, moe0_w_down)

    x = lru_mixer_layer(x, c, norm1_mix, ada1_mix_w, ada1_mix_b, w_in1, lru_conv_w, lru_conv_b, lru_w_a, lru_b_a,
                        lru_w_x, lru_b_x, lru_lambda, w_out1)
    x = hierarchical_moe(x, c, norm1_ffn, ada1_ffn_w, ada1_ffn_b, moe1_w_group, moe1_w_expert,
                         moe1_w_gate_up, moe1_w_down, final_norm_g=norm_final)
    return x
```

```python
import functools

import jax
import jax.numpy as jnp
from jax import lax
from jax.experimental import pallas as pl
from jax.experimental.pallas import tpu as pltpu
from jax.experimental.pallas import tpu_sc as plsc

F32 = jnp.float32
BF16 = jnp.bfloat16

D_MODEL = 2048
BATCH = 4
SEQ = 4096
N_TOK = BATCH * SEQ
EPS = 1e-6
NEG_INF = -1e30

CONV_CH = 1024
CONV_GROUPS = 8
CONV_WIDTH = 31
CONV_HALO = 16
CONV_COPY_ROWS = 512

ATT_HEADS = 8
HEAD_DIM = 128
ATT_W = ATT_HEADS * HEAD_DIM
ATT_BRANCHES = ((128, 1), (512, 4), (2048, 16))
ATT_RADIUS = 64
ATT_QB = 128
ATT_KB = ATT_QB + 2 * ATT_RADIUS
ATT_UNROLL = 8
ATT_COPY_ROWS = 512
ALIBI_MAX = 8.0

LRU_WIDTH = 2688
LRU_BLOCKS = 16
LRU_BW = LRU_WIDTH // LRU_BLOCKS
LRU_PW = 192
LRU_WP = LRU_BLOCKS * LRU_PW
LRU_GW = 2 * LRU_PW
LRU_CONV = 4
LRU_C = 8.0
LRU_CHUNKS = 8
LRU_CL = SEQ // LRU_CHUNKS
LRU_PITCH = LRU_CL + 8
LOG2E = 1.4426950408889634

N_GROUPS = 4
EXPERTS_PER_GROUP = 8
N_EXPERTS = 32
TOP_K = 2
EXPERT_FF = 1024
MOE_TB = 256
MOE_ROWS = N_TOK * TOP_K + N_EXPERTS * MOE_TB
MOE_NBLK = MOE_ROWS // MOE_TB
MOE_CAST_ROWS = 256
META_LANES = 256
assert MOE_NBLK <= META_LANES
SC_CORES = 2
SC_SUBCORES = 16
SC_CHUNK = 32
ROUTER_LANES = 128

MM_ROWS = 1024
MM_WIDE_COLS = 2048
VMEM_LIMIT = 48 * 1024 * 1024
BIG_VMEM_LIMIT = 56 * 1024 * 1024


def _cparams(sem):
    return pltpu.CompilerParams(dimension_semantics=sem, vmem_limit_bytes=VMEM_LIMIT)


def _sigmoid(x):
    return 0.5 * jnp.tanh(0.5 * x) + 0.5


def _ada_kernel(ct_ref, w_ref, b_ref, o_ref):
    ct = ct_ref[...]
    st = ct * _sigmoid(ct)
    w = w_ref[...]
    rows = [jnp.sum(w * st[:, b:b + 1], axis=0, keepdims=True) for b in range(BATCH)]
    o_ref[...] = jnp.concatenate(rows, axis=0) + b_ref[...]


def ada_modulation(c, w, b):
    tn = 512
    n = w.shape[1]
    mod = pl.pallas_call(
        _ada_kernel,
        grid=(n // tn,),
        in_specs=[
            pl.BlockSpec((D_MODEL, BATCH), lambda j: (0, 0)),
            pl.BlockSpec((D_MODEL, tn), lambda j: (0, j)),
            pl.BlockSpec((1, tn), lambda j: (0, j)),
        ],
        out_specs=pl.BlockSpec((BATCH, tn), lambda j: (0, j)),
        out_shape=jax.ShapeDtypeStruct((BATCH, n), F32),
        compiler_params=_cparams(("parallel",)),
        name="ada_modulation",
    )(c.T, w, b.reshape(1, n))
    shift, scale, gate = jnp.split(mod, 3, axis=-1)
    return shift[:, None, :], scale[:, None, :], gate[:, None, :]


def _modulated_norm(x, g, scale, shift):
    y = x * lax.rsqrt(jnp.mean(x * x, axis=-1, keepdims=True) + EPS)
    return (y * g) * (1.0 + scale) + shift


def _norm_kernel(x_ref, g_ref, sc_ref, sh_ref, o_ref):
    o_ref[0] = _modulated_norm(x_ref[0], g_ref[...], sc_ref[0], sh_ref[0]).astype(o_ref.dtype)


def modulated_norm(x, g, scale, shift, tm=512):
    return pl.pallas_call(
        _norm_kernel,
        grid=(BATCH, SEQ // tm),
        in_specs=[
            pl.BlockSpec((1, tm, D_MODEL), lambda b, i: (b, i, 0)),
            pl.BlockSpec((1, D_MODEL), lambda b, i: (0, 0)),
            pl.BlockSpec((1, 1, D_MODEL), lambda b, i: (b, 0, 0)),
            pl.BlockSpec((1, 1, D_MODEL), lambda b, i: (b, 0, 0)),
        ],
        out_specs=pl.BlockSpec((1, tm, D_MODEL), lambda b, i: (b, i, 0)),
        out_shape=jax.ShapeDtypeStruct((BATCH, SEQ, D_MODEL), BF16),
        compiler_params=_cparams(("parallel", "parallel")),
        name="modulated_norm",
    )(x, g.reshape(1, D_MODEL), scale, shift)


def _mm_kernel(a_ref, w_ref, o_ref, *, head_major):
    res = jnp.dot(a_ref[0], w_ref[...], preferred_element_type=F32)
    if head_major:
        for hh in range(ATT_HEADS):
            o_ref[0, 0, hh] = res[:, hh * HEAD_DIM:(hh + 1) * HEAD_DIM].astype(o_ref.dtype)
    else:
        o_ref[0] = res.astype(o_ref.dtype)


def project(a, w, out_dtype, head_major=False, tm=MM_ROWS, tn=1024):
    k, n = w.shape
    if head_major:
        assert tn == ATT_W
        out_shape = jax.ShapeDtypeStruct((n // tn, BATCH, ATT_HEADS, SEQ, HEAD_DIM), out_dtype)
        out_spec = pl.BlockSpec((1, 1, ATT_HEADS, tm, HEAD_DIM), lambda b, i, j: (j, b, 0, i, 0))
    else:
        out_shape = jax.ShapeDtypeStruct((BATCH, SEQ, n), out_dtype)
        out_spec = pl.BlockSpec((1, tm, tn), lambda b, i, j: (b, i, j))
    return pl.pallas_call(
        functools.partial(_mm_kernel, head_major=head_major),
        grid=(BATCH, SEQ // tm, n // tn),
        in_specs=[
            pl.BlockSpec((1, tm, k), lambda b, i, j: (b, i, 0)),
            pl.BlockSpec((k, tn), lambda b, i, j: (0, j)),
        ],
        out_specs=out_spec,
        out_shape=out_shape,
        compiler_params=_cparams(("parallel", "parallel", "parallel")),
        name="project",
    )(a, w)


def _mm_residual_kernel(*refs, n_in):
    a_refs, w_refs = refs[:n_in], refs[n_in:2 * n_in]
    x_ref, gate_ref, o_ref = refs[2 * n_in:]
    acc = jnp.dot(a_refs[0][0], w_refs[0][...], preferred_element_type=F32)
    for a_ref, w_ref in zip(a_refs[1:], w_refs[1:]):
        acc = acc + jnp.dot(a_ref[0], w_ref[...], preferred_element_type=F32)
    o_ref[0] = x_ref[0] + gate_ref[0] * acc


def project_residual(a_list, w_list, x, gate, tm=MM_ROWS, tn=1024):
    n_in = len(a_list)
    grid = (BATCH, SEQ // tm, D_MODEL // tn)
    in_specs = [pl.BlockSpec((1, tm, a.shape[-1]), lambda b, i, j: (b, i, 0)) for a in a_list]
    in_specs += [pl.BlockSpec((w.shape[0], tn), lambda b, i, j: (0, j)) for w in w_list]
    in_specs += [
        pl.BlockSpec((1, tm, tn), lambda b, i, j: (b, i, j)),
        pl.BlockSpec((1, 1, tn), lambda b, i, j: (b, 0, j)),
    ]
    return pl.pallas_call(
        functools.partial(_mm_residual_kernel, n_in=n_in),
        grid=grid,
        in_specs=in_specs,
        out_specs=pl.BlockSpec((1, tm, tn), lambda b, i, j: (b, i, j)),
        out_shape=jax.ShapeDtypeStruct((BATCH, SEQ, D_MODEL), F32),
        compiler_params=_cparams(("parallel", "parallel", "parallel")),
        name="project_residual",
    )(*a_list, *w_list, x, gate)


def _conv_kernel(v_ref, g_ref, cw_ref, cb_ref, ng_ref, nb_ref, o_ref, pad_scr, *, ts):
    zeros = jnp.zeros((CONV_HALO, 128), F32)
    pad_scr[0, 0:CONV_HALO, :] = zeros
    pad_scr[0, SEQ + CONV_HALO:SEQ + 2 * CONV_HALO, :] = zeros
    for t0 in range(0, SEQ, CONV_COPY_ROWS):
        rows = slice(t0, t0 + CONV_COPY_ROWS)
        pad_scr[0, CONV_HALO + t0:CONV_HALO + t0 + CONV_COPY_ROWS, :] = v_ref[0, rows, :] * _sigmoid(g_ref[0, rows, :])
    shifted_len = SEQ + 2 * CONV_HALO - 8
    for b in range(1, 8):
        for t0 in range(0, shifted_len, CONV_COPY_ROWS):
            n = min(CONV_COPY_ROWS, shifted_len - t0)
            pad_scr[b, t0:t0 + n, :] = pad_scr[0, t0 + b:t0 + b + n, :]
    first = CONV_HALO - CONV_WIDTH // 2

    def tap(t0, k):
        off = first + k
        return pad_scr[off % 8, t0 + off - off % 8:t0 + off - off % 8 + ts, :]

    for t0 in range(0, SEQ, ts):
        acc = cw_ref[0:1, :] * tap(t0, 0) + cb_ref[...]
        for k in range(1, CONV_WIDTH):
            acc = acc + cw_ref[k:k + 1, :] * tap(t0, k)
        mu = jnp.mean(acc, axis=-1, keepdims=True)
        cen = acc - mu
        var = jnp.mean(cen * cen, axis=-1, keepdims=True)
        y = cen * lax.rsqrt(var + EPS) * ng_ref[...] + nb_ref[...]
        o_ref[0, t0:t0 + ts, :] = (y * _sigmoid(y)).astype(o_ref.dtype)


def conv_module(p, conv_w, conv_b, norm_g, norm_b, ts=128):
    ng = CONV_GROUPS
    vec = lambda a: a.reshape(1, CONV_CH)
    vspec = pl.BlockSpec((1, 128), lambda b, c: (0, c))
    return pl.pallas_call(
        functools.partial(_conv_kernel, ts=ts),
        grid=(BATCH, ng),
        in_specs=[
            pl.BlockSpec((1, SEQ, 128), lambda b, c: (b, 0, c)),
            pl.BlockSpec((1, SEQ, 128), lambda b, c: (b, 0, c + ng)),
            pl.BlockSpec((CONV_WIDTH, 128), lambda b, c: (0, c)),
            vspec, vspec, vspec,
        ],
        out_specs=pl.BlockSpec((1, SEQ, 128), lambda b, c: (b, 0, c)),
        out_shape=jax.ShapeDtypeStruct((BATCH, SEQ, CONV_CH), BF16),
        scratch_shapes=[pltpu.VMEM((8, SEQ + 2 * CONV_HALO, 128), F32)],
        compiler_params=_cparams(("parallel", "parallel")),
        name="conv_module",
    )(p, p, conv_w, vec(conv_b), vec(norm_g), vec(norm_b))


def _attn_kernel(q_ref, k_ref, v_ref, *rest):
    nbr = len(ATT_BRANCHES)
    bias_refs, o_ref = rest[:nbr], rest[nbr]
    qf, kf, vf, qd, kpad, vpad, acc, mx, den = rest[nbr + 1:]
    scale = HEAD_DIM ** -0.5
    zeros = jnp.zeros((ATT_RADIUS, HEAD_DIM), BF16)
    for t0 in range(0, SEQ, ATT_COPY_ROWS):
        rows = slice(t0, t0 + ATT_COPY_ROWS)
        qf[rows, :] = q_ref[0, 0, rows, :].astype(F32)
        kf[rows, :] = k_ref[0, 0, rows, :].astype(F32)
        vf[rows, :] = v_ref[0, 0, rows, :].astype(F32)

    order = sorted(range(nbr), key=lambda n: -ATT_BRANCHES[n][1])
    for pos, n in enumerate(order):
        dil = ATT_BRANCHES[n][1]
        bias_ref = bias_refs[n]
        sub = SEQ // dil
        nblk = sub // ATT_QB
        first, last = pos == 0, pos == nbr - 1
        assert dil == 1 or not last

        def class_rows(r, start, count, dil=dil):
            if dil == 1:
                return pl.ds(start, count)
            return pl.ds(r + start * dil, count, stride=dil)

        group = max(1, min(dil, ATT_UNROLL // nblk))
        kstride = sub + 2 * ATT_RADIUS

        def residue_classes(g, carry, dil=dil, sub=sub, nblk=nblk, bias_ref=bias_ref, first=first, last=last,
                            class_rows=class_rows, group=group, kstride=kstride):
            for cc in range(group):
                r = g * group + cc
                k0 = cc * kstride
                for pad in (kpad, vpad):
                    pad[k0:k0 + ATT_RADIUS, :] = zeros
                    pad[k0 + sub + ATT_RADIUS:k0 + kstride, :] = zeros
                for c0 in range(0, sub, ATT_COPY_ROWS):
                    n_rows = min(ATT_COPY_ROWS, sub - c0)
                    src = class_rows(r, c0, n_rows)
                    qd[cc * sub + c0:cc * sub + c0 + n_rows, :] = qf[src, :].astype(BF16)
                    kpad[k0 + ATT_RADIUS + c0:k0 + ATT_RADIUS + c0 + n_rows, :] = kf[src, :].astype(BF16)
                    vpad[k0 + ATT_RADIUS + c0:k0 + ATT_RADIUS + c0 + n_rows, :] = vf[src, :].astype(BF16)

            def block(i, r, cc):
                if isinstance(i, int):
                    q0 = i * ATT_QB
                    variant = 0 if i == 0 else (2 if i == nblk - 1 else 1)
                else:
                    q0 = pl.multiple_of(i * ATT_QB, ATT_QB)
                    variant = jnp.where(i == 0, 0, jnp.where(i == nblk - 1, 2, 1))
                s = lax.dot_general(qd[pl.ds(cc * sub + q0, ATT_QB), :], kpad[pl.ds(cc * kstride + q0, ATT_KB), :],
                                    (((1,), (1,)), ((), ())), preferred_element_type=F32)
                s = s * scale + bias_ref[0, variant]
                m = jnp.max(s, axis=-1, keepdims=True)
                e = jnp.exp(s - m)
                l = jnp.sum(e, axis=-1, keepdims=True)
                o = jnp.dot(e.astype(BF16), vpad[pl.ds(cc * kstride + q0, ATT_KB), :], preferred_element_type=F32)
                nat = class_rows(r, q0, ATT_QB)
                wide = (ATT_QB, HEAD_DIM)
                if first:
                    acc[nat, :] = o
                    mx[nat, :] = jnp.broadcast_to(m, wide)
                    den[nat, :] = jnp.broadcast_to(l, wide)
                else:
                    m_old = mx[nat, :]
                    m_new = jnp.maximum(m_old, m)
                    alpha = jnp.exp(m_old - m_new)
                    beta = jnp.exp(m - m_new)
                    total = acc[nat, :] * alpha + o * beta
                    l_new = den[nat, :] * alpha + l * beta
                    if last:
                        o_ref[0, nat, :] = (total / l_new).astype(o_ref.dtype)
                    else:
                        acc[nat, :] = total
                        mx[nat, :] = m_new
                        den[nat, :] = l_new

            if group == 1:
                def body(i, carry):
                    block(i, g, 0)
                    return carry

                lax.fori_loop(0, nblk, body, 0, unroll=min(nblk, ATT_UNROLL))
            else:
                for cc in range(group):
                    for i in range(nblk):
                        block(i, g * group + cc, cc)
            return carry

        if dil == 1:
            residue_classes(0, 0)
        else:
            lax.fori_loop(0, dil // group, residue_classes, 0)


def _alibi_bias(dilation):
    row = jnp.arange(ATT_QB)[:, None]
    colm = jnp.arange(ATT_KB)[None, :]
    rel = jnp.abs(colm - ATT_RADIUS - row)
    slopes = 2.0 ** (-ALIBI_MAX * jnp.arange(1, ATT_HEADS + 1, dtype=F32) / ATT_HEADS)
    bias = -slopes[:, None, None] * (rel * dilation).astype(F32)[None]
    band = rel <= ATT_RADIUS
    masks = jnp.stack([band & (colm >= ATT_RADIUS), band, band & (colm < ATT_KB - ATT_RADIUS)])
    return jnp.where(masks[None], bias[:, None], NEG_INF)


def dilated_attention(qkv):
    for window, dilation in ATT_BRANCHES:
        assert window // (2 * dilation) == ATT_RADIUS
        assert SEQ // dilation >= 2 * ATT_QB
    qspec = lambda which: pl.BlockSpec((None, 1, 1, SEQ, HEAD_DIM), lambda b, h, which=which: (which, b, h, 0, 0))
    bias_spec = pl.BlockSpec((1, 3, ATT_QB, ATT_KB), lambda b, h: (h, 0, 0, 0))
    rows = lambda dtype, n=SEQ: pltpu.VMEM((n, HEAD_DIM), dtype)
    return pl.pallas_call(
        _attn_kernel,
        grid=(BATCH, ATT_HEADS),
        in_specs=[qspec(0), qspec(1), qspec(2)] + [bias_spec] * len(ATT_BRANCHES),
        out_specs=pl.BlockSpec((1, SEQ, HEAD_DIM), lambda b, h: (b, 0, h)),
        out_shape=jax.ShapeDtypeStruct((BATCH, SEQ, ATT_W), BF16),
        scratch_shapes=[rows(F32), rows(F32), rows(F32), rows(BF16),
                        rows(BF16, SEQ + 2 * ATT_RADIUS), rows(BF16, SEQ + 2 * ATT_RADIUS),
                        rows(F32), rows(F32), rows(F32)],
        compiler_params=_cparams(("parallel", "parallel")),
        name="dilated_attention",
    )(qkv, qkv, qkv, *[_alibi_bias(dilation) for _, dilation in ATT_BRANCHES])


def conv_attention_layer(x, c, norm_g, ada_w, ada_b, w_in, conv_w, conv_b, conv_norm_g, conv_norm_b, w_out):
    shift, scale, gate = ada_modulation(c, ada_w, ada_b)
    w_in = w_in.astype(BF16)
    h = modulated_norm(x, norm_g, scale, shift)
    p_conv = project(h, w_in[:, :2 * CONV_CH], F32, tn=MM_WIDE_COLS)
    qkv = project(h, w_in[:, 2 * CONV_CH:], BF16, head_major=True)
    a = conv_module(p_conv, conv_w, conv_b, conv_norm_g, conv_norm_b)
    o = dilated_attention(qkv)
    w_out = w_out.astype(BF16)
    return project_residual([a, o], [w_out[:CONV_CH], w_out[CONV_CH:]], x, gate)


def _softplus(x):
    return jnp.maximum(x, 0.0) + jnp.log1p(jnp.exp(-jnp.abs(x)))


def _gelu_tanh(x):
    return 0.5 * x * (1.0 + jnp.tanh(0.7978845608028654 * (x + 0.044715 * (x * x * x))))


def _lru_kernel(x_ref, cw_ref, cb_ref, w_ref, b_ref, lam_ref, o_ref, xc, a_scr, u_scr, yacc):
    gw = LRU_GW
    ntile = gw // 128
    lead = LRU_CONV // 2

    def shifted(r0, off):
        lo, hi = r0 + off, r0 + off + LRU_CL
        parts = []
        if lo < 0:
            parts.append(jnp.zeros((-lo, gw), F32))
        parts.append(x_ref[0, max(lo, 0):min(hi, SEQ), :])
        if hi > SEQ:
            parts.append(jnp.zeros((hi - SEQ, gw), F32))
        return parts[0] if len(parts) == 1 else jnp.concatenate(parts, axis=0)

    for j in range(LRU_CHUNKS):
        r0 = j * LRU_CL
        acc = cw_ref[0:1, :] * shifted(r0, -lead) + cb_ref[...]
        for k in range(1, LRU_CONV):
            acc = acc + cw_ref[k:k + 1, :] * shifted(r0, k - lead)
        xc[r0:r0 + LRU_CL, :] = acc

    for d in range(2):
        reverse = d == 1
        half_decay = (0.5 * LRU_C) * _softplus(-lam_ref[d:d + 1, :])
        for j in range(LRU_CHUNKS):
            xj = xc[j * LRU_CL:(j + 1) * LRU_CL, :]
            pre = jnp.dot(xj.astype(BF16), w_ref[d, 0], preferred_element_type=F32) + b_ref[d, 0]
            t_r = jnp.tanh(pre[:, :gw])
            t_i = jnp.tanh(pre[:, gw:])
            neg_log_a = t_r * half_decay + half_decay
            a = jnp.exp2(neg_log_a * (-LOG2E))
            one_minus_a2 = jnp.tanh(neg_log_a) * (a * a + 1.0)
            root = one_minus_a2 * lax.rsqrt(jnp.maximum(one_minus_a2, 1e-30))
            u = root * ((0.5 * t_i + 0.5) * xj)
            for lt in range(ntile):
                rows = slice(j * LRU_PITCH, j * LRU_PITCH + LRU_CL)
                a_scr[lt, rows, :] = a[:, lt * 128:(lt + 1) * 128]
                u_scr[lt, rows, :] = u[:, lt * 128:(lt + 1) * 128]

        def chunk_rows(row, grp):
            return pl.ds(row + grp * 8 * LRU_PITCH, 8, stride=LRU_PITCH)

        chains = [(lt, grp) for lt in range(ntile) for grp in range(LRU_CHUNKS // 8)]

        def step(ii, carry):
            row = (LRU_CL - 1 - ii) if reverse else ii
            out = []
            for (lt, grp), (h, prod) in zip(chains, carry):
                a = a_scr[lt, chunk_rows(row, grp), :]
                h = a * h + u_scr[lt, chunk_rows(row, grp), :]
                prod = a * prod
                u_scr[lt, chunk_rows(row, grp), :] = h
                a_scr[lt, chunk_rows(row, grp), :] = prod
                out.append((h, prod))
            return tuple(out)

        init = tuple((jnp.zeros((8, 128), F32), jnp.ones((8, 128), F32)) for _ in chains)
        lax.fori_loop(0, LRU_CL, step, init, unroll=4)

        last = 0 if reverse else LRU_CL - 1
        for lt in range(ntile):
            lanes = slice(lt * 128, (lt + 1) * 128)
            h_end = [u_scr[lt, chunk_rows(last, grp), :] for grp in range(LRU_CHUNKS // 8)]
            p_end = [a_scr[lt, chunk_rows(last, grp), :] for grp in range(LRU_CHUNKS // 8)]
            carry = jnp.zeros((1, 128), F32)
            for j in (range(LRU_CHUNKS - 1, -1, -1) if reverse else range(LRU_CHUNKS)):
                src = slice(j * LRU_PITCH, j * LRU_PITCH + LRU_CL)
                dst = slice(j * LRU_CL, (j + 1) * LRU_CL)
                y = u_scr[lt, src, :] + a_scr[lt, src, :] * carry
                if reverse:
                    o_ref[0, dst, lanes] = (yacc[dst, lanes] + y).astype(o_ref.dtype)
                else:
                    yacc[dst, lanes] = y
                carry = h_end[j // 8][j % 8:j % 8 + 1, :] + p_end[j // 8][j % 8:j % 8 + 1, :] * carry


def rglru_block(p, conv_w, conv_b, w_gates, b_gates, lam):
    npair = LRU_WP // LRU_GW
    gw = LRU_GW
    big = lambda: pltpu.VMEM((SEQ, gw), F32)
    slabs = lambda: pltpu.VMEM((gw // 128, LRU_CHUNKS * LRU_PITCH, 128), F32)
    return pl.pallas_call(
        _lru_kernel,
        grid=(BATCH, npair),
        in_specs=[
            pl.BlockSpec((1, SEQ, gw), lambda b, n: (b, 0, n + npair)),
            pl.BlockSpec((LRU_CONV, gw), lambda b, n: (0, n)),
            pl.BlockSpec((1, gw), lambda b, n: (0, n)),
            pl.BlockSpec((2, 1, gw, 2 * gw), lambda b, n: (0, n, 0, 0)),
            pl.BlockSpec((2, 1, 1, 2 * gw), lambda b, n: (0, n, 0, 0)),
            pl.BlockSpec((2, gw), lambda b, n: (0, n)),
        ],
        out_specs=pl.BlockSpec((1, SEQ, gw), lambda b, n: (b, 0, n)),
        out_shape=jax.ShapeDtypeStruct((BATCH, SEQ, LRU_WP), BF16),
        scratch_shapes=[big(), slabs(), slabs(), big()],
        compiler_params=pltpu.CompilerParams(dimension_semantics=("parallel", "parallel"),
                                             vmem_limit_bytes=BIG_VMEM_LIMIT),
        name="rglru_block",
    )(p, conv_w, conv_b, w_gates, b_gates, lam)


def _gelu_gate_kernel(y_ref, gate_ref, o_ref):
    o_ref[0] = (_gelu_tanh(gate_ref[0]) * y_ref[0].astype(F32)).astype(o_ref.dtype)


def gelu_gate(y, p, tm=512):
    spec = pl.BlockSpec((1, tm, LRU_WP), lambda b, i: (b, i, 0))
    return pl.pallas_call(
        _gelu_gate_kernel,
        grid=(BATCH, SEQ // tm),
        in_specs=[spec, spec],
        out_specs=spec,
        out_shape=jax.ShapeDtypeStruct((BATCH, SEQ, LRU_WP), BF16),
        compiler_params=_cparams(("parallel", "parallel")),
        name="gelu_gate",
    )(y, p)


def _pad_blocks(a):
    lead = a.shape[:-1]
    a = a.reshape(*lead, LRU_BLOCKS, LRU_BW)
    a = jnp.pad(a, [(0, 0)] * len(lead) + [(0, 0), (0, LRU_PW - LRU_BW)])
    return a.reshape(*lead, LRU_WP)


def _pair_block_diagonal(w):
    pad = LRU_PW - LRU_BW
    w = jnp.pad(w, ((0, 0), (0, 0), (0, pad), (0, pad))).reshape(2, LRU_BLOCKS // 2, 2, LRU_PW, LRU_PW)
    zero = jnp.zeros_like(w[:, :, 0])
    top = jnp.concatenate([w[:, :, 0], zero], axis=-1)
    bottom = jnp.concatenate([zero, w[:, :, 1]], axis=-1)
    return jnp.concatenate([top, bottom], axis=-2)


def lru_mixer_layer(x, c, norm_g, ada_w, ada_b, w_in, conv_w, conv_b, w_a, b_a, w_x, b_x, lam, w_out):
    shift, scale, gate = ada_modulation(c, ada_w, ada_b)
    w_in_p = jnp.concatenate([_pad_blocks(w_in[:, :LRU_WIDTH]), _pad_blocks(w_in[:, LRU_WIDTH:])], axis=1)
    p = project(modulated_norm(x, norm_g, scale, shift), w_in_p.astype(BF16), F32, tn=MM_WIDE_COLS)
    npair = LRU_WP // LRU_GW
    w_gates = jnp.concatenate([_pair_block_diagonal(w_a), _pair_block_diagonal(w_x)], axis=-1)
    b_gates = jnp.concatenate([_pad_blocks(b_a).reshape(2, npair, 1, LRU_GW),
                               _pad_blocks(b_x).reshape(2, npair, 1, LRU_GW)], axis=-1)
    y = rglru_block(p, _pad_blocks(conv_w), _pad_blocks(conv_b).reshape(1, LRU_WP),
                    (0.5 * w_gates).astype(BF16), 0.5 * b_gates, _pad_blocks(lam))
    w_out_p = jnp.pad(w_out.reshape(LRU_BLOCKS, LRU_BW, D_MODEL), ((0, 0), (0, LRU_PW - LRU_BW), (0, 0)))
    return project_residual([gelu_gate(y, p)], [w_out_p.reshape(LRU_WP, D_MODEL).astype(BF16)], x, gate)


def _pack_halves(v):
    half = v.shape[1] // 2
    as_bits = lambda t: pltpu.bitcast(t.astype(BF16).astype(F32), jnp.uint32)
    word = as_bits(v[:, :half]) | (as_bits(v[:, half:]) >> 16)
    return pltpu.bitcast(word, jnp.int32)


def _unpack_halves(word):
    bits = pltpu.bitcast(word, jnp.uint32)
    first = pltpu.bitcast(bits & jnp.uint32(0xFFFF0000), F32)
    second = pltpu.bitcast(bits << 16, F32)
    return first, second


def _router_kernel(x_ref, g_ref, sc_ref, sh_ref, wr_ref, h_ref, eid_ref, gt_ref):
    h = _modulated_norm(x_ref[0], g_ref[...], sc_ref[0], sh_ref[0])
    hb = h.astype(BF16)
    h_ref[0] = _pack_halves(h)
    h_lo = (h - hb.astype(F32)).astype(BF16)
    w = wr_ref[...]
    w_hi = w.astype(BF16)
    w_lo = (w - w_hi.astype(F32)).astype(BF16)
    lg = (jnp.dot(hb, w_hi, preferred_element_type=F32) + jnp.dot(h_lo, w_hi, preferred_element_type=F32)
          + jnp.dot(hb, w_lo, preferred_element_type=F32))
    lt = lg.T
    tm = lt.shape[1]
    row = lax.broadcasted_iota(jnp.int32, (8, tm), 0)
    big = jnp.int32(99)
    gl = jnp.where(row < N_GROUPS, lt[0:8], -jnp.inf)
    g_max = jnp.max(gl, axis=0, keepdims=True)
    g_sel = jnp.min(jnp.where(gl == g_max, row, big), axis=0, keepdims=True)
    g_prob = 1.0 / jnp.sum(jnp.exp(gl - g_max), axis=0, keepdims=True)
    el = jnp.zeros((8, tm), F32)
    for g in range(N_GROUPS):
        el = jnp.where(g_sel == g, lt[8 + 8 * g:16 + 8 * g], el)
    v1 = jnp.max(el, axis=0, keepdims=True)
    i1 = jnp.min(jnp.where(el == v1, row, big), axis=0, keepdims=True)
    el2 = jnp.where(row == i1, -jnp.inf, el)
    v2 = jnp.max(el2, axis=0, keepdims=True)
    i2 = jnp.min(jnp.where(el2 == v2, row, big), axis=0, keepdims=True)
    e2 = jnp.exp(v2 - v1)
    p1 = 1.0 / (1.0 + e2)
    p2 = e2 * p1
    eid_ref[0] = jnp.where(row == 0, g_sel * EXPERTS_PER_GROUP + i1,
                           jnp.where(row == 1, g_sel * EXPERTS_PER_GROUP + i2, 0))
    rows = lax.broadcasted_iota(jnp.int32, (ROUTER_LANES, tm), 0)
    gates = jnp.where(rows == 0, g_prob * p1, jnp.where(rows == 1, g_prob * p2, 0.0))
    gt_ref[0] = gates.T


def route(x, g, scale, shift, w_router, tm=512):
    return pl.pallas_call(
        _router_kernel,
        grid=(BATCH, SEQ // tm),
        in_specs=[
            pl.BlockSpec((1, tm, D_MODEL), lambda b, i: (b, i, 0)),
            pl.BlockSpec((1, D_MODEL), lambda b, i: (0, 0)),
            pl.BlockSpec((1, 1, D_MODEL), lambda b, i: (b, 0, 0)),
            pl.BlockSpec((1, 1, D_MODEL), lambda b, i: (b, 0, 0)),
            pl.BlockSpec((D_MODEL, ROUTER_LANES), lambda b, i: (0, 0)),
        ],
        out_specs=[
            pl.BlockSpec((1, tm, D_MODEL // 2), lambda b, i: (b, i, 0)),
            pl.BlockSpec((1, 8, tm), lambda b, i: (b, 0, i)),
            pl.BlockSpec((1, tm, ROUTER_LANES), lambda b, i: (b, i, 0)),
        ],
        out_shape=[
            jax.ShapeDtypeStruct((BATCH, SEQ, D_MODEL // 2), jnp.int32),
            jax.ShapeDtypeStruct((BATCH, 8, SEQ), jnp.int32),
            jax.ShapeDtypeStruct((BATCH, SEQ, ROUTER_LANES), F32),
        ],
        compiler_params=_cparams(("parallel", "parallel")),
        name="route",
    )(x, g.reshape(1, D_MODEL), scale, shift, w_router)


def _expert_kernel(be_ref, nu_ref, first_ref, nxt_ref, x_ref, wgu_hbm, wd_hbm, o_ref,
                   gu_stage, d_stage, wgu, wd, sem):
    i = pl.program_id(0)
    active = i < nu_ref[0]

    def weight_copies(e):
        return (pltpu.make_async_copy(wgu_hbm.at[e], gu_stage, sem.at[0]),
                pltpu.make_async_copy(wd_hbm.at[e], d_stage, sem.at[1]))

    @pl.when(i == 0)
    def _():
        for cp in weight_copies(be_ref[0]):
            cp.start()

    @pl.when(active & (first_ref[i] == 1))
    def _():
        for cp in weight_copies(be_ref[i]):
            cp.wait()

        def cast_rows(stage, dst):
            def body(c, carry):
                r0 = pl.multiple_of(c * MOE_CAST_ROWS, MOE_CAST_ROWS)
                dst[pl.ds(r0, MOE_CAST_ROWS), :] = stage[pl.ds(r0, MOE_CAST_ROWS), :].astype(BF16)
                return carry
            lax.fori_loop(0, stage.shape[0] // MOE_CAST_ROWS, body, 0)

        cast_rows(gu_stage, wgu)
        cast_rows(d_stage, wd)

        @pl.when(nxt_ref[i] >= 0)
        def _():
            for cp in weight_copies(nxt_ref[i]):
                cp.start(priority=1)

    @pl.when(active)
    def _():
        x_first, x_second = _unpack_halves(x_ref[...])
        half = D_MODEL // 2
        gu = (jnp.dot(x_first.astype(BF16), wgu[0:half, :], preferred_element_type=F32)
              + jnp.dot(x_second.astype(BF16), wgu[half:, :], preferred_element_type=F32))
        g = gu[:, :EXPERT_FF]
        u = gu[:, EXPERT_FF:]
        act = (g * _sigmoid(g) * u).astype(BF16)
        o_ref[...] = _pack_halves(jnp.dot(act, wd[...], preferred_element_type=F32))

    @pl.when(jnp.logical_not(active))
    def _():
        o_ref[...] = jnp.zeros(o_ref.shape, o_ref.dtype)


def expert_blocks(xs, blk_expert, n_used, first, nxt, w_gate_up, w_down):
    row_map = lambda i, be, nu, first, nxt: (jnp.minimum(i, nu[0] - 1), 0)
    return pl.pallas_call(
        _expert_kernel,
        grid_spec=pltpu.PrefetchScalarGridSpec(
            num_scalar_prefetch=4,
            grid=(MOE_NBLK,),
            in_specs=[
                pl.BlockSpec((MOE_TB, D_MODEL // 2), row_map),
                pl.BlockSpec(memory_space=pl.ANY),
                pl.BlockSpec(memory_space=pl.ANY),
            ],
            out_specs=pl.BlockSpec((MOE_TB, D_MODEL // 2), lambda i, be, nu, first, nxt: (i, 0)),
            scratch_shapes=[
                pltpu.VMEM((D_MODEL, 2 * EXPERT_FF), F32),
                pltpu.VMEM((EXPERT_FF, D_MODEL), F32),
                pltpu.VMEM((D_MODEL, 2 * EXPERT_FF), BF16),
                pltpu.VMEM((EXPERT_FF, D_MODEL), BF16),
                pltpu.SemaphoreType.DMA((2,)),
            ],
        ),
        out_shape=jax.ShapeDtypeStruct((MOE_ROWS, D_MODEL // 2), jnp.int32),
        compiler_params=pltpu.CompilerParams(dimension_semantics=("arbitrary",), vmem_limit_bytes=BIG_VMEM_LIMIT),
        name="expert_blocks",
    )(blk_expert, n_used, first, nxt, xs, w_gate_up, w_down)


def _router_weights(w_group, w_expert):
    w = jnp.zeros((D_MODEL, ROUTER_LANES), F32)
    w = w.at[:, 0:N_GROUPS].set(w_group)
    return w.at[:, 8:8 + N_EXPERTS].set(w_expert)


def _route_meta_kernel(e_ref, dest_ref, be_ref, first_ref, nxt_ref, nu_ref):
    nrow = e_ref.shape[0]
    expert = lax.broadcasted_iota(jnp.int32, (N_EXPERTS, META_LANES), 0)
    upto = (lax.broadcasted_iota(jnp.int32, (META_LANES, META_LANES), 0)
            <= lax.broadcasted_iota(jnp.int32, (META_LANES, META_LANES), 1)).astype(BF16)

    def count_row(c, acc):
        return acc + jnp.where(expert == e_ref[c], 1.0, 0.0)

    acc = lax.fori_loop(0, nrow, count_row, jnp.zeros((N_EXPERTS, META_LANES), F32))
    counts = jnp.sum(acc, axis=1, keepdims=True)
    pcounts = jnp.floor((counts + (MOE_TB - 1.0)) * (1.0 / MOE_TB)) * MOE_TB
    ends = []
    run = jnp.zeros((1, 1), F32)
    for e in range(N_EXPERTS):
        run = run + pcounts[e:e + 1, :]
        ends.append(run)
    pend = jnp.concatenate(ends, axis=0)
    pstart = pend - pcounts

    def dest_row(c, running):
        hit = expert == e_ref[c]
        seen = jnp.dot(jnp.where(hit, 1.0, 0.0).astype(BF16), upto, preferred_element_type=F32)
        slot = seen - 1.0 + (running + pstart)
        dest_ref[c] = jnp.sum(jnp.where(hit, slot, 0.0), axis=0, keepdims=True).astype(jnp.int32)
        return running + seen[:, META_LANES - 1:META_LANES]

    lax.fori_loop(0, nrow, dest_row, jnp.zeros((N_EXPERTS, 1), F32))

    blk = lax.broadcasted_iota(jnp.int32, (1, META_LANES), 1).astype(F32)
    n_used = pend[N_EXPERTS - 1:N_EXPERTS, :] * (1.0 / MOE_TB)

    def expert_at(b):
        start = jnp.minimum(b, n_used - 1.0) * MOE_TB
        return jnp.minimum(jnp.sum(jnp.where(pend <= start, 1.0, 0.0), axis=0, keepdims=True), N_EXPERTS - 1.0)

    be = expert_at(blk)
    is_first = jnp.logical_and(jnp.logical_or(blk == 0.0, be != expert_at(blk - 1.0)), blk < n_used)
    later = jnp.logical_and(expert.astype(F32) > be, counts > 0.0)
    nxt = jnp.min(jnp.where(later, expert.astype(F32), 2.0 * N_EXPERTS), axis=0, keepdims=True)
    be_ref[...] = be.astype(jnp.int32)
    first_ref[...] = jnp.where(is_first, 1, 0)
    nxt_ref[...] = jnp.where(nxt < N_EXPERTS, nxt, -1.0).astype(jnp.int32)
    nu_ref[...] = jnp.broadcast_to(n_used, (1, META_LANES)).astype(jnp.int32)


def route_metadata(e_flat):
    nrow = e_flat.shape[0] // META_LANES
    lane_row = jax.ShapeDtypeStruct((1, META_LANES), jnp.int32)
    dest, be, first, nxt, nu = pl.pallas_call(
        _route_meta_kernel,
        out_shape=[jax.ShapeDtypeStruct((nrow, 1, META_LANES), jnp.int32), lane_row, lane_row, lane_row, lane_row],
        compiler_params=pltpu.CompilerParams(vmem_limit_bytes=VMEM_LIMIT),
        name="route_metadata",
    )(e_flat.reshape(nrow, 1, META_LANES))
    return dest.reshape(-1), be[0, :MOE_NBLK], first[0, :MOE_NBLK], nxt[0, :MOE_NBLK], nu[0, :1]


def sc_move_rows(src, idx, n_out, scatter):
    n_idx = idx.shape[0]
    n_src, width = src.shape
    n_workers = SC_CORES * SC_SUBCORES
    per_w = n_idx // n_workers
    nchunk = per_w // SC_CHUNK
    assert per_w * n_workers == n_idx and nchunk * SC_CHUNK == per_w and nchunk % 2 == 0
    assert per_w % n_src == 0 or n_src % per_w == 0
    mesh = plsc.VectorSubcoreMesh(core_axis_name="c", subcore_axis_name="s")

    @functools.partial(
        pl.kernel, mesh=mesh,
        out_type=jax.ShapeDtypeStruct((n_out, width), src.dtype),
        scratch_types=[
            pltpu.VMEM((nchunk, SC_CHUNK), jnp.int32),
            pltpu.VMEM((2, SC_CHUNK, width), src.dtype),
            pltpu.SemaphoreType.DMA((2,)),
            pltpu.SemaphoreType.DMA((2,)),
        ],
    )
    def move(src_hbm, idx_hbm, out_hbm, idx_v, rows_v, in_sem, out_sem):
        wid = lax.axis_index("s") * SC_CORES + lax.axis_index("c")
        base = wid * per_w
        pltpu.sync_copy(idx_hbm.at[wid], idx_v)

        def load(c, b):
            if scatter:
                rows = src_hbm.at[pl.ds(lax.rem(base, n_src) + c * SC_CHUNK, SC_CHUNK)]
            else:
                rows = src_hbm.at[idx_v.at[c]]
            return pltpu.make_async_copy(rows, rows_v.at[b], in_sem.at[b])

        def store(c, b):
            if scatter:
                rows = out_hbm.at[idx_v.at[c]]
            else:
                rows = out_hbm.at[pl.ds(base + c * SC_CHUNK, SC_CHUNK)]
            return pltpu.make_async_copy(rows_v.at[b], rows, out_sem.at[b])

        load(0, 0).start()

        @pl.loop(0, nchunk, step=2)
        def _(c0):
            for b in (0, 1):
                c = c0 + b
                load(c, b).wait()

                @pl.when(c + 1 < nchunk)
                def _():
                    @pl.when(c >= 1)
                    def _():
                        store(c - 1, 1 - b).wait()

                    load(c + 1, 1 - b).start()

                store(c, b).start()

        store(nchunk - 2, 0).wait()
        store(nchunk - 1, 1).wait()

    return move(src, idx.reshape(n_workers, nchunk, SC_CHUNK))


def _combine_kernel(x_ref, z0_ref, z1_ref, gates_ref, ada_ref, ng_ref, o_ref, *, final):
    gates = gates_ref[0]
    g0, g1 = gates[:, 0:1], gates[:, 1:2]
    z0_first, z0_second = _unpack_halves(z0_ref[...])
    z1_first, z1_second = _unpack_halves(z1_ref[...])
    y = jnp.concatenate([g0 * z0_first + g1 * z1_first, g0 * z0_second + g1 * z1_second], axis=1)
    x = x_ref[0] + ada_ref[0] * y
    if final:
        x = x * lax.rsqrt(jnp.mean(x * x, axis=-1, keepdims=True) + EPS) * ng_ref[...]
    o_ref[0] = x


def moe_combine(x, z, gates, gate_ada, norm_g, tm=512):
    final = norm_g is not None
    if not final:
        norm_g = jnp.ones((D_MODEL,), F32)
    nt = SEQ // tm
    return pl.pallas_call(
        functools.partial(_combine_kernel, final=final),
        grid=(BATCH, nt),
        in_specs=[
            pl.BlockSpec((1, tm, D_MODEL), lambda b, i: (b, i, 0)),
            pl.BlockSpec((tm, D_MODEL // 2), lambda b, i: (b * nt + i, 0)),
            pl.BlockSpec((tm, D_MODEL // 2), lambda b, i: (BATCH * nt + b * nt + i, 0)),
            pl.BlockSpec((1, tm, ROUTER_LANES), lambda b, i: (b, i, 0)),
            pl.BlockSpec((1, 1, D_MODEL), lambda b, i: (b, 0, 0)),
            pl.BlockSpec((1, D_MODEL), lambda b, i: (0, 0)),
        ],
        out_specs=pl.BlockSpec((1, tm, D_MODEL), lambda b, i: (b, i, 0)),
        out_shape=jax.ShapeDtypeStruct((BATCH, SEQ, D_MODEL), F32),
        compiler_params=_cparams(("parallel", "parallel")),
        name="moe_combine",
    )(x, z, z, gates, gate_ada, norm_g.reshape(1, D_MODEL))


def hierarchical_moe(x, c, norm_g, ada_w, ada_b, w_group, w_expert, w_gate_up, w_down, final_norm_g=None):
    shift, scale, gate_ada = ada_modulation(c, ada_w, ada_b)
    h, eid, gates = route(x, norm_g, scale, shift, _router_weights(w_group, w_expert))
    e_flat = jnp.concatenate([eid[:, 0, :].reshape(N_TOK), eid[:, 1, :].reshape(N_TOK)])
    dest, blk_expert, first, nxt, n_used = route_metadata(e_flat)
    xs = sc_move_rows(h.reshape(N_TOK, D_MODEL // 2), dest, MOE_ROWS, scatter=True)
    yb = expert_blocks(xs, blk_expert, n_used, first, nxt, w_gate_up, w_down)
    z = sc_move_rows(yb, dest, TOP_K * N_TOK, scatter=False)
    return moe_combine(x, z, gates, gate_ada, final_norm_g)


def kernel(x, c, norm0_mix, ada0_mix_w, ada0_mix_b, w_in0, conv_w, conv_b, conv_norm_g, conv_norm_b, w_out0, norm0_ffn, ada0_ffn_w, ada0_ffn_b, moe0_w_group, moe0_w_expert, moe0_w_gate_up, moe0_w_down, norm1_mix, ada1_mix_w, ada1_mix_b, w_in1, lru_conv_w, lru_conv_b, lru_w_a, lru_b_a, lru_w_x, lru_b_x, lru_lambda, w_out1, norm1_ffn, ada1_ffn_w, ada1_ffn_b, moe1_w_group, moe1_w_expert, moe1_w_gate_up, moe1_w_down, norm_final):
    x = conv_attention_layer(x, c, norm0_mix, ada0_mix_w, ada0_mix_b, w_in0, conv_w, conv_b, conv_norm_g, conv_norm_b,
                             w_out0)
    x = hierarchical_moe(x, c, norm0_ffn, ada0_ffn_w, ada0_ffn_b, moe0_w_group, moe0_w_expert,
                         moe0_w_gate_up, moe0_w_down)

    x = lru_mixer_layer(x, c, norm1_mix, ada1_mix_w, ada1_mix_b, w_in1, lru_conv_w, lru_conv_b, lru_w_a, lru_b_a,
                        lru_w_x, lru_b_x, lru_lambda, w_out1)
    x = hierarchical_moe(x, c, norm1_ffn, ada1_ffn_w, ada1_ffn_b, moe1_w_group, moe1_w_expert,
                         moe1_w_gate_up, moe1_w_down, final_norm_g=norm_final)
    return x
```

```python
import functools

import jax
import jax.numpy as jnp
from jax import lax
from jax.experimental import pallas as pl
from jax.experimental.pallas import tpu as pltpu
from jax.experimental.pallas import tpu_sc as plsc

F32 = jnp.float32
BF16 = jnp.bfloat16

D_MODEL = 2048
BATCH = 4
SEQ = 4096
N_TOK = BATCH * SEQ
EPS = 1e-6
NEG_INF = -1e30

CONV_CH = 1024
CONV_GROUPS = 8
CONV_WIDTH = 31
CONV_HALO = 16
ATT_HEADS = 8
HEAD_DIM = 128
ATT_W = ATT_HEADS * HEAD_DIM
ATT_BRANCHES = ((128, 1), (512, 4), (2048, 16))
ATT_RADIUS = 64
ATT_QB = 128
ATT_KB = ATT_QB + 2 * ATT_RADIUS
ATT_UNROLL = 8
ATT_COPY_ROWS = 512
ALIBI_MAX = 8.0

LRU_WIDTH = 2688
LRU_BLOCKS = 16
LRU_BW = LRU_WIDTH // LRU_BLOCKS
LRU_PW = 192
LRU_WP = LRU_BLOCKS * LRU_PW
LRU_GW = 2 * LRU_PW
LRU_CONV = 4
LRU_C = 8.0
LRU_CHUNKS = 8
LRU_CL = SEQ // LRU_CHUNKS
LRU_PITCH = LRU_CL + 8
LOG2E = 1.4426950408889634

N_GROUPS = 4
EXPERTS_PER_GROUP = 8
N_EXPERTS = 32
TOP_K = 2
EXPERT_FF = 1024
MOE_TB = 256
MOE_ROWS = N_TOK * TOP_K + N_EXPERTS * MOE_TB
MOE_NBLK = MOE_ROWS // MOE_TB
MOE_CAST_ROWS = 256
META_LANES = 256
assert MOE_NBLK <= META_LANES
SC_CORES = 2
SC_SUBCORES = 16
SC_CHUNK = 32
ROUTER_LANES = 128

MM_ROWS = 1024
MM_WIDE_COLS = 2048
VMEM_LIMIT = 48 * 1024 * 1024
BIG_VMEM_LIMIT = 56 * 1024 * 1024


def _cparams(sem):
    return pltpu.CompilerParams(dimension_semantics=sem, vmem_limit_bytes=VMEM_LIMIT)


def _sigmoid(x):
    return 0.5 * jnp.tanh(0.5 * x) + 0.5


def _ada_kernel(ct_ref, w_ref, b_ref, o_ref):
    ct = ct_ref[...]
    st = ct * _sigmoid(ct)
    w = w_ref[...]
    rows = [jnp.sum(w * st[:, b:b + 1], axis=0, keepdims=True) for b in range(BATCH)]
    o_ref[...] = jnp.concatenate(rows, axis=0) + b_ref[...]


def ada_modulation(c, w, b):
    tn = 512
    n = w.shape[1]
    mod = pl.pallas_call(
        _ada_kernel,
        grid=(n // tn,),
        in_specs=[
            pl.BlockSpec((D_MODEL, BATCH), lambda j: (0, 0)),
            pl.BlockSpec((D_MODEL, tn), lambda j: (0, j)),
            pl.BlockSpec((1, tn), lambda j: (0, j)),
        ],
        out_specs=pl.BlockSpec((BATCH, tn), lambda j: (0, j)),
        out_shape=jax.ShapeDtypeStruct((BATCH, n), F32),
        compiler_params=_cparams(("parallel",)),
        name="ada_modulation",
    )(c.T, w, b.reshape(1, n))
    shift, scale, gate = jnp.split(mod, 3, axis=-1)
    return shift[:, None, :], scale[:, None, :], gate[:, None, :]


def _modulated_norm(x, g, scale, shift):
    y = x * lax.rsqrt(jnp.mean(x * x, axis=-1, keepdims=True) + EPS)
    return (y * g) * (1.0 + scale) + shift


def _norm_kernel(x_ref, g_ref, sc_ref, sh_ref, o_ref):
    o_ref[0] = _modulated_norm(x_ref[0], g_ref[...], sc_ref[0], sh_ref[0]).astype(o_ref.dtype)


def modulated_norm(x, g, scale, shift, tm=512):
    return pl.pallas_call(
        _norm_kernel,
        grid=(BATCH, SEQ // tm),
        in_specs=[
            pl.BlockSpec((1, tm, D_MODEL), lambda b, i: (b, i, 0)),
            pl.BlockSpec((1, D_MODEL), lambda b, i: (0, 0)),
            pl.BlockSpec((1, 1, D_MODEL), lambda b, i: (b, 0, 0)),
            pl.BlockSpec((1, 1, D_MODEL), lambda b, i: (b, 0, 0)),
        ],
        out_specs=pl.BlockSpec((1, tm, D_MODEL), lambda b, i: (b, i, 0)),
        out_shape=jax.ShapeDtypeStruct((BATCH, SEQ, D_MODEL), BF16),
        compiler_params=_cparams(("parallel", "parallel")),
        name="modulated_norm",
    )(x, g.reshape(1, D_MODEL), scale, shift)


def _mm_kernel(a_ref, w_ref, o_ref, *, head_major):
    res = jnp.dot(a_ref[0], w_ref[...], preferred_element_type=F32)
    if head_major:
        for hh in range(ATT_HEADS):
            o_ref[0, 0, hh] = res[:, hh * HEAD_DIM:(hh + 1) * HEAD_DIM].astype(o_ref.dtype)
    else:
        o_ref[0] = res.astype(o_ref.dtype)


def project(a, w, out_dtype, head_major=False, tm=MM_ROWS, tn=1024):
    k, n = w.shape
    if head_major:
        assert tn == ATT_W
        out_shape = jax.ShapeDtypeStruct((n // tn, BATCH, ATT_HEADS, SEQ, HEAD_DIM), out_dtype)
        out_spec = pl.BlockSpec((1, 1, ATT_HEADS, tm, HEAD_DIM), lambda b, i, j: (j, b, 0, i, 0))
    else:
        out_shape = jax.ShapeDtypeStruct((BATCH, SEQ, n), out_dtype)
        out_spec = pl.BlockSpec((1, tm, tn), lambda b, i, j: (b, i, j))
    return pl.pallas_call(
        functools.partial(_mm_kernel, head_major=head_major),
        grid=(BATCH, SEQ // tm, n // tn),
        in_specs=[
            pl.BlockSpec((1, tm, k), lambda b, i, j: (b, i, 0)),
            pl.BlockSpec((k, tn), lambda b, i, j: (0, j)),
        ],
        out_specs=out_spec,
        out_shape=out_shape,
        compiler_params=_cparams(("parallel", "parallel", "parallel")),
        name="project",
    )(a, w)


def _mm_residual_kernel(*refs, n_in):
    a_refs, w_refs = refs[:n_in], refs[n_in:2 * n_in]
    x_ref, gate_ref, o_ref = refs[2 * n_in:]
    acc = jnp.dot(a_refs[0][0], w_refs[0][...], preferred_element_type=F32)
    for a_ref, w_ref in zip(a_refs[1:], w_refs[1:]):
        acc = acc + jnp.dot(a_ref[0], w_ref[...], preferred_element_type=F32)
    o_ref[0] = x_ref[0] + gate_ref[0] * acc


def project_residual(a_list, w_list, x, gate, tm=MM_ROWS, tn=1024):
    n_in = len(a_list)
    grid = (BATCH, SEQ // tm, D_MODEL // tn)
    in_specs = [pl.BlockSpec((1, tm, a.shape[-1]), lambda b, i, j: (b, i, 0)) for a in a_list]
    in_specs += [pl.BlockSpec((w.shape[0], tn), lambda b, i, j: (0, j)) for w in w_list]
    in_specs += [
        pl.BlockSpec((1, tm, tn), lambda b, i, j: (b, i, j)),
        pl.BlockSpec((1, 1, tn), lambda b, i, j: (b, 0, j)),
    ]
    return pl.pallas_call(
        functools.partial(_mm_residual_kernel, n_in=n_in),
        grid=grid,
        in_specs=in_specs,
        out_specs=pl.BlockSpec((1, tm, tn), lambda b, i, j: (b, i, j)),
        out_shape=jax.ShapeDtypeStruct((BATCH, SEQ, D_MODEL), F32),
        compiler_params=_cparams(("parallel", "parallel", "parallel")),
        name="project_residual",
    )(*a_list, *w_list, x, gate)


def _conv_kernel(v_ref, g_ref, cw_ref, cb_ref, ng_ref, nb_ref, o_ref, pad_scr, *, ts):
    zeros = jnp.zeros((CONV_HALO, 128), F32)
    pad_scr[0:CONV_HALO, :] = zeros
    pad_scr[SEQ + CONV_HALO:SEQ + 2 * CONV_HALO, :] = zeros
    for t0 in range(0, SEQ, ts):
        pad_scr[CONV_HALO + t0:CONV_HALO + t0 + ts, :] = v_ref[0, t0:t0 + ts, :] * _sigmoid(g_ref[0, t0:t0 + ts, :])
    first = CONV_HALO - CONV_WIDTH // 2
    for t0 in range(0, SEQ, ts):
        acc = cw_ref[0:1, :] * pad_scr[first + t0:first + t0 + ts, :] + cb_ref[...]
        for k in range(1, CONV_WIDTH):
            acc = acc + cw_ref[k:k + 1, :] * pad_scr[first + t0 + k:first + t0 + k + ts, :]
        mu = jnp.mean(acc, axis=-1, keepdims=True)
        cen = acc - mu
        var = jnp.mean(cen * cen, axis=-1, keepdims=True)
        y = cen * lax.rsqrt(var + EPS) * ng_ref[...] + nb_ref[...]
        o_ref[0, t0:t0 + ts, :] = (y * _sigmoid(y)).astype(o_ref.dtype)


def conv_module(p, conv_w, conv_b, norm_g, norm_b, ts=128):
    ng = CONV_GROUPS
    vec = lambda a: a.reshape(1, CONV_CH)
    vspec = pl.BlockSpec((1, 128), lambda b, c: (0, c))
    return pl.pallas_call(
        functools.partial(_conv_kernel, ts=ts),
        grid=(BATCH, ng),
        in_specs=[
            pl.BlockSpec((1, SEQ, 128), lambda b, c: (b, 0, c)),
            pl.BlockSpec((1, SEQ, 128), lambda b, c: (b, 0, c + ng)),
            pl.BlockSpec((CONV_WIDTH, 128), lambda b, c: (0, c)),
            vspec, vspec, vspec,
        ],
        out_specs=pl.BlockSpec((1, SEQ, 128), lambda b, c: (b, 0, c)),
        out_shape=jax.ShapeDtypeStruct((BATCH, SEQ, CONV_CH), BF16),
        scratch_shapes=[pltpu.VMEM((SEQ + 2 * CONV_HALO, 128), F32)],
        compiler_params=_cparams(("parallel", "parallel")),
        name="conv_module",
    )(p, p, conv_w, vec(conv_b), vec(norm_g), vec(norm_b))


def _attn_kernel(q_ref, k_ref, v_ref, *rest):
    nbr = len(ATT_BRANCHES)
    bias_refs, o_ref = rest[:nbr], rest[nbr]
    qf, kf, vf, qd, kpad, vpad, acc, mx, den = rest[nbr + 1:]
    scale = HEAD_DIM ** -0.5
    zeros = jnp.zeros((ATT_RADIUS, HEAD_DIM), BF16)
    for t0 in range(0, SEQ, ATT_COPY_ROWS):
        rows = slice(t0, t0 + ATT_COPY_ROWS)
        qf[rows, :] = q_ref[0, 0, rows, :].astype(F32)
        kf[rows, :] = k_ref[0, 0, rows, :].astype(F32)
        vf[rows, :] = v_ref[0, 0, rows, :].astype(F32)

    order = sorted(range(nbr), key=lambda n: -ATT_BRANCHES[n][1])
    for pos, n in enumerate(order):
        dil = ATT_BRANCHES[n][1]
        bias_ref = bias_refs[n]
        sub = SEQ // dil
        nblk = sub // ATT_QB
        first, last = pos == 0, pos == nbr - 1
        assert dil == 1 or not last

        def class_rows(r, start, count, dil=dil):
            if dil == 1:
                return pl.ds(start, count)
            return pl.ds(r + start * dil, count, stride=dil)

        group = max(1, min(dil, ATT_UNROLL // nblk))
        kstride = sub + 2 * ATT_RADIUS

        def residue_classes(g, carry, dil=dil, sub=sub, nblk=nblk, bias_ref=bias_ref, first=first, last=last,
                            class_rows=class_rows, group=group, kstride=kstride):
            for cc in range(group):
                r = g * group + cc
                k0 = cc * kstride
                for pad in (kpad, vpad):
                    pad[k0:k0 + ATT_RADIUS, :] = zeros
                    pad[k0 + sub + ATT_RADIUS:k0 + kstride, :] = zeros
                for c0 in range(0, sub, ATT_COPY_ROWS):
                    n_rows = min(ATT_COPY_ROWS, sub - c0)
                    src = class_rows(r, c0, n_rows)
                    qd[cc * sub + c0:cc * sub + c0 + n_rows, :] = qf[src, :].astype(BF16)
                    kpad[k0 + ATT_RADIUS + c0:k0 + ATT_RADIUS + c0 + n_rows, :] = kf[src, :].astype(BF16)
                    vpad[k0 + ATT_RADIUS + c0:k0 + ATT_RADIUS + c0 + n_rows, :] = vf[src, :].astype(BF16)

            def block(i, r, cc):
                if isinstance(i, int):
                    q0 = i * ATT_QB
                    variant = 0 if i == 0 else (2 if i == nblk - 1 else 1)
                else:
                    q0 = pl.multiple_of(i * ATT_QB, ATT_QB)
                    variant = jnp.where(i == 0, 0, jnp.where(i == nblk - 1, 2, 1))
                s = lax.dot_general(qd[pl.ds(cc * sub + q0, ATT_QB), :], kpad[pl.ds(cc * kstride + q0, ATT_KB), :],
                                    (((1,), (1,)), ((), ())), preferred_element_type=F32)
                s = s * scale + bias_ref[0, variant]
                m = jnp.max(s, axis=-1, keepdims=True)
                e = jnp.exp(s - m)
                l = jnp.sum(e, axis=-1, keepdims=True)
                o = jnp.dot(e.astype(BF16), vpad[pl.ds(cc * kstride + q0, ATT_KB), :], preferred_element_type=F32)
                nat = class_rows(r, q0, ATT_QB)
                wide = (ATT_QB, HEAD_DIM)
                if first:
                    acc[nat, :] = o
                    mx[nat, :] = jnp.broadcast_to(m, wide)
                    den[nat, :] = jnp.broadcast_to(l, wide)
                else:
                    m_old = mx[nat, :]
                    m_new = jnp.maximum(m_old, m)
                    alpha = jnp.exp(m_old - m_new)
                    beta = jnp.exp(m - m_new)
                    total = acc[nat, :] * alpha + o * beta
                    l_new = den[nat, :] * alpha + l * beta
                    if last:
                        o_ref[0, nat, :] = (total / l_new).astype(o_ref.dtype)
                    else:
                        acc[nat, :] = total
                        mx[nat, :] = m_new
                        den[nat, :] = l_new

            if group == 1:
                def body(i, carry):
                    block(i, g, 0)
                    return carry

                lax.fori_loop(0, nblk, body, 0, unroll=min(nblk, ATT_UNROLL))
            else:
                for cc in range(group):
                    for i in range(nblk):
                        block(i, g * group + cc, cc)
            return carry

        if dil == 1:
            residue_classes(0, 0)
        else:
            lax.fori_loop(0, dil // group, residue_classes, 0)


def _alibi_bias(dilation):
    row = jnp.arange(ATT_QB)[:, None]
    colm = jnp.arange(ATT_KB)[None, :]
    rel = jnp.abs(colm - ATT_RADIUS - row)
    slopes = 2.0 ** (-ALIBI_MAX * jnp.arange(1, ATT_HEADS + 1, dtype=F32) / ATT_HEADS)
    bias = -slopes[:, None, None] * (rel * dilation).astype(F32)[None]
    band = rel <= ATT_RADIUS
    masks = jnp.stack([band & (colm >= ATT_RADIUS), band, band & (colm < ATT_KB - ATT_RADIUS)])
    return jnp.where(masks[None], bias[:, None], NEG_INF)


def dilated_attention(qkv):
    for window, dilation in ATT_BRANCHES:
        assert window // (2 * dilation) == ATT_RADIUS
        assert SEQ // dilation >= 2 * ATT_QB
    qspec = lambda which: pl.BlockSpec((None, 1, 1, SEQ, HEAD_DIM), lambda b, h, which=which: (which, b, h, 0, 0))
    bias_spec = pl.BlockSpec((1, 3, ATT_QB, ATT_KB), lambda b, h: (h, 0, 0, 0))
    rows = lambda dtype, n=SEQ: pltpu.VMEM((n, HEAD_DIM), dtype)
    return pl.pallas_call(
        _attn_kernel,
        grid=(BATCH, ATT_HEADS),
        in_specs=[qspec(0), qspec(1), qspec(2)] + [bias_spec] * len(ATT_BRANCHES),
        out_specs=pl.BlockSpec((1, SEQ, HEAD_DIM), lambda b, h: (b, 0, h)),
        out_shape=jax.ShapeDtypeStruct((BATCH, SEQ, ATT_W), BF16),
        scratch_shapes=[rows(F32), rows(F32), rows(F32), rows(BF16),
                        rows(BF16, SEQ + 2 * ATT_RADIUS), rows(BF16, SEQ + 2 * ATT_RADIUS),
                        rows(F32), rows(F32), rows(F32)],
        compiler_params=_cparams(("parallel", "parallel")),
        name="dilated_attention",
    )(qkv, qkv, qkv, *[_alibi_bias(dilation) for _, dilation in ATT_BRANCHES])


def conv_attention_layer(x, c, norm_g, ada_w, ada_b, w_in, conv_w, conv_b, conv_norm_g, conv_norm_b, w_out):
    shift, scale, gate = ada_modulation(c, ada_w, ada_b)
    w_in = w_in.astype(BF16)
    h = modulated_norm(x, norm_g, scale, shift)
    p_conv = project(h, w_in[:, :2 * CONV_CH], F32, tn=MM_WIDE_COLS)
    qkv = project(h, w_in[:, 2 * CONV_CH:], BF16, head_major=True)
    a = conv_module(p_conv, conv_w, conv_b, conv_norm_g, conv_norm_b)
    o = dilated_attention(qkv)
    w_out = w_out.astype(BF16)
    return project_residual([a, o], [w_out[:CONV_CH], w_out[CONV_CH:]], x, gate)


def _softplus(x):
    return jnp.maximum(x, 0.0) + jnp.log1p(jnp.exp(-jnp.abs(x)))


def _gelu_tanh(x):
    return 0.5 * x * (1.0 + jnp.tanh(0.7978845608028654 * (x + 0.044715 * (x * x * x))))


def _lru_kernel(x_ref, cw_ref, cb_ref, w_ref, b_ref, lam_ref, o_ref, xc, a_scr, u_scr, yacc):
    gw = LRU_GW
    ntile = gw // 128
    lead = LRU_CONV // 2

    def shifted(r0, off):
        lo, hi = r0 + off, r0 + off + LRU_CL
        parts = []
        if lo < 0:
            parts.append(jnp.zeros((-lo, gw), F32))
        parts.append(x_ref[0, max(lo, 0):min(hi, SEQ), :])
        if hi > SEQ:
            parts.append(jnp.zeros((hi - SEQ, gw), F32))
        return parts[0] if len(parts) == 1 else jnp.concatenate(parts, axis=0)

    for j in range(LRU_CHUNKS):
        r0 = j * LRU_CL
        acc = cw_ref[0:1, :] * shifted(r0, -lead) + cb_ref[...]
        for k in range(1, LRU_CONV):
            acc = acc + cw_ref[k:k + 1, :] * shifted(r0, k - lead)
        xc[r0:r0 + LRU_CL, :] = acc

    for d in range(2):
        reverse = d == 1
        half_decay = (0.5 * LRU_C) * _softplus(-lam_ref[d:d + 1, :])
        for j in range(LRU_CHUNKS):
            xj = xc[j * LRU_CL:(j + 1) * LRU_CL, :]
            pre = jnp.dot(xj.astype(BF16), w_ref[d, 0], preferred_element_type=F32) + b_ref[d, 0]
            t_r = jnp.tanh(pre[:, :gw])
            t_i = jnp.tanh(pre[:, gw:])
            neg_log_a = t_r * half_decay + half_decay
            a = jnp.exp2(neg_log_a * (-LOG2E))
            one_minus_a2 = jnp.tanh(neg_log_a) * (a * a + 1.0)
            root = one_minus_a2 * lax.rsqrt(jnp.maximum(one_minus_a2, 1e-30))
            u = root * ((0.5 * t_i + 0.5) * xj)
            for lt in range(ntile):
                rows = slice(j * LRU_PITCH, j * LRU_PITCH + LRU_CL)
                a_scr[lt, rows, :] = a[:, lt * 128:(lt + 1) * 128]
                u_scr[lt, rows, :] = u[:, lt * 128:(lt + 1) * 128]

        def chunk_rows(row, grp):
            return pl.ds(row + grp * 8 * LRU_PITCH, 8, stride=LRU_PITCH)

        chains = [(lt, grp) for lt in range(ntile) for grp in range(LRU_CHUNKS // 8)]

        def step(ii, carry):
            row = (LRU_CL - 1 - ii) if reverse else ii
            out = []
            for (lt, grp), (h, prod) in zip(chains, carry):
                a = a_scr[lt, chunk_rows(row, grp), :]
                h = a * h + u_scr[lt, chunk_rows(row, grp), :]
                prod = a * prod
                u_scr[lt, chunk_rows(row, grp), :] = h
                a_scr[lt, chunk_rows(row, grp), :] = prod
                out.append((h, prod))
            return tuple(out)

        init = tuple((jnp.zeros((8, 128), F32), jnp.ones((8, 128), F32)) for _ in chains)
        lax.fori_loop(0, LRU_CL, step, init, unroll=4)

        last = 0 if reverse else LRU_CL - 1
        for lt in range(ntile):
            lanes = slice(lt * 128, (lt + 1) * 128)
            h_end = [u_scr[lt, chunk_rows(last, grp), :] for grp in range(LRU_CHUNKS // 8)]
            p_end = [a_scr[lt, chunk_rows(last, grp), :] for grp in range(LRU_CHUNKS // 8)]
            carry = jnp.zeros((1, 128), F32)
            for j in (range(LRU_CHUNKS - 1, -1, -1) if reverse else range(LRU_CHUNKS)):
                src = slice(j * LRU_PITCH, j * LRU_PITCH + LRU_CL)
                dst = slice(j * LRU_CL, (j + 1) * LRU_CL)
                y = u_scr[lt, src, :] + a_scr[lt, src, :] * carry
                if reverse:
                    o_ref[0, dst, lanes] = (yacc[dst, lanes] + y).astype(o_ref.dtype)
                else:
                    yacc[dst, lanes] = y
                carry = h_end[j // 8][j % 8:j % 8 + 1, :] + p_end[j // 8][j % 8:j % 8 + 1, :] * carry


def rglru_block(p, conv_w, conv_b, w_gates, b_gates, lam):
    npair = LRU_WP // LRU_GW
    gw = LRU_GW
    big = lambda: pltpu.VMEM((SEQ, gw), F32)
    slabs = lambda: pltpu.VMEM((gw // 128, LRU_CHUNKS * LRU_PITCH, 128), F32)
    return pl.pallas_call(
        _lru_kernel,
        grid=(BATCH, npair),
        in_specs=[
            pl.BlockSpec((1, SEQ, gw), lambda b, n: (b, 0, n + npair)),
            pl.BlockSpec((LRU_CONV, gw), lambda b, n: (0, n)),
            pl.BlockSpec((1, gw), lambda b, n: (0, n)),
            pl.BlockSpec((2, 1, gw, 2 * gw), lambda b, n: (0, n, 0, 0)),
            pl.BlockSpec((2, 1, 1, 2 * gw), lambda b, n: (0, n, 0, 0)),
            pl.BlockSpec((2, gw), lambda b, n: (0, n)),
        ],
        out_specs=pl.BlockSpec((1, SEQ, gw), lambda b, n: (b, 0, n)),
        out_shape=jax.ShapeDtypeStruct((BATCH, SEQ, LRU_WP), BF16),
        scratch_shapes=[big(), slabs(), slabs(), big()],
        compiler_params=pltpu.CompilerParams(dimension_semantics=("parallel", "parallel"),
                                             vmem_limit_bytes=BIG_VMEM_LIMIT),
        name="rglru_block",
    )(p, conv_w, conv_b, w_gates, b_gates, lam)


def _gelu_gate_kernel(y_ref, gate_ref, o_ref):
    o_ref[0] = (_gelu_tanh(gate_ref[0]) * y_ref[0].astype(F32)).astype(o_ref.dtype)


def gelu_gate(y, p, tm=512):
    spec = pl.BlockSpec((1, tm, LRU_WP), lambda b, i: (b, i, 0))
    return pl.pallas_call(
        _gelu_gate_kernel,
        grid=(BATCH, SEQ // tm),
        in_specs=[spec, spec],
        out_specs=spec,
        out_shape=jax.ShapeDtypeStruct((BATCH, SEQ, LRU_WP), BF16),
        compiler_params=_cparams(("parallel", "parallel")),
        name="gelu_gate",
    )(y, p)


def _pad_blocks(a):
    lead = a.shape[:-1]
    a = a.reshape(*lead, LRU_BLOCKS, LRU_BW)
    a = jnp.pad(a, [(0, 0)] * len(lead) + [(0, 0), (0, LRU_PW - LRU_BW)])
    return a.reshape(*lead, LRU_WP)


def _pair_block_diagonal(w):
    pad = LRU_PW - LRU_BW
    w = jnp.pad(w, ((0, 0), (0, 0), (0, pad), (0, pad))).reshape(2, LRU_BLOCKS // 2, 2, LRU_PW, LRU_PW)
    zero = jnp.zeros_like(w[:, :, 0])
    top = jnp.concatenate([w[:, :, 0], zero], axis=-1)
    bottom = jnp.concatenate([zero, w[:, :, 1]], axis=-1)
    return jnp.concatenate([top, bottom], axis=-2)


def lru_mixer_layer(x, c, norm_g, ada_w, ada_b, w_in, conv_w, conv_b, w_a, b_a, w_x, b_x, lam, w_out):
    shift, scale, gate = ada_modulation(c, ada_w, ada_b)
    w_in_p = jnp.concatenate([_pad_blocks(w_in[:, :LRU_WIDTH]), _pad_blocks(w_in[:, LRU_WIDTH:])], axis=1)
    p = project(modulated_norm(x, norm_g, scale, shift), w_in_p.astype(BF16), F32, tn=MM_WIDE_COLS)
    npair = LRU_WP // LRU_GW
    w_gates = jnp.concatenate([_pair_block_diagonal(w_a), _pair_block_diagonal(w_x)], axis=-1)
    b_gates = jnp.concatenate([_pad_blocks(b_a).reshape(2, npair, 1, LRU_GW),
                               _pad_blocks(b_x).reshape(2, npair, 1, LRU_GW)], axis=-1)
    y = rglru_block(p, _pad_blocks(conv_w), _pad_blocks(conv_b).reshape(1, LRU_WP),
                    (0.5 * w_gates).astype(BF16), 0.5 * b_gates, _pad_blocks(lam))
    w_out_p = jnp.pad(w_out.reshape(LRU_BLOCKS, LRU_BW, D_MODEL), ((0, 0), (0, LRU_PW - LRU_BW), (0, 0)))
    return project_residual([gelu_gate(y, p)], [w_out_p.reshape(LRU_WP, D_MODEL).astype(BF16)], x, gate)


def _pack_halves(v):
    half = v.shape[1] // 2
    as_bits = lambda t: pltpu.bitcast(t.astype(BF16).astype(F32), jnp.uint32)
    word = as_bits(v[:, :half]) | (as_bits(v[:, half:]) >> 16)
    return pltpu.bitcast(word, jnp.int32)


def _unpack_halves(word):
    bits = pltpu.bitcast(word, jnp.uint32)
    first = pltpu.bitcast(bits & jnp.uint32(0xFFFF0000), F32)
    second = pltpu.bitcast(bits << 16, F32)
    return first, second


def _router_kernel(x_ref, g_ref, sc_ref, sh_ref, wr_ref, h_ref, eid_ref, gt_ref):
    h = _modulated_norm(x_ref[0], g_ref[...], sc_ref[0], sh_ref[0])
    hb = h.astype(BF16)
    h_ref[0] = _pack_halves(h)
    h_lo = (h - hb.astype(F32)).astype(BF16)
    w = wr_ref[...]
    w_hi = w.astype(BF16)
    w_lo = (w - w_hi.astype(F32)).astype(BF16)
    lg = (jnp.dot(hb, w_hi, preferred_element_type=F32) + jnp.dot(h_lo, w_hi, preferred_element_type=F32)
          + jnp.dot(hb, w_lo, preferred_element_type=F32))
    lt = lg.T
    tm = lt.shape[1]
    row = lax.broadcasted_iota(jnp.int32, (8, tm), 0)
    big = jnp.int32(99)
    gl = jnp.where(row < N_GROUPS, lt[0:8], -jnp.inf)
    g_max = jnp.max(gl, axis=0, keepdims=True)
    g_sel = jnp.min(jnp.where(gl == g_max, row, big), axis=0, keepdims=True)
    g_prob = 1.0 / jnp.sum(jnp.exp(gl - g_max), axis=0, keepdims=True)
    el = jnp.zeros((8, tm), F32)
    for g in range(N_GROUPS):
        el = jnp.where(g_sel == g, lt[8 + 8 * g:16 + 8 * g], el)
    v1 = jnp.max(el, axis=0, keepdims=True)
    i1 = jnp.min(jnp.where(el == v1, row, big), axis=0, keepdims=True)
    el2 = jnp.where(row == i1, -jnp.inf, el)
    v2 = jnp.max(el2, axis=0, keepdims=True)
    i2 = jnp.min(jnp.where(el2 == v2, row, big), axis=0, keepdims=True)
    e2 = jnp.exp(v2 - v1)
    p1 = 1.0 / (1.0 + e2)
    p2 = e2 * p1
    eid_ref[0] = jnp.where(row == 0, g_sel * EXPERTS_PER_GROUP + i1,
                           jnp.where(row == 1, g_sel * EXPERTS_PER_GROUP + i2, 0))
    rows = lax.broadcasted_iota(jnp.int32, (ROUTER_LANES, tm), 0)
    gates = jnp.where(rows == 0, g_prob * p1, jnp.where(rows == 1, g_prob * p2, 0.0))
    gt_ref[0] = gates.T


def route(x, g, scale, shift, w_router, tm=512):
    return pl.pallas_call(
        _router_kernel,
        grid=(BATCH, SEQ // tm),
        in_specs=[
            pl.BlockSpec((1, tm, D_MODEL), lambda b, i: (b, i, 0)),
            pl.BlockSpec((1, D_MODEL), lambda b, i: (0, 0)),
            pl.BlockSpec((1, 1, D_MODEL), lambda b, i: (b, 0, 0)),
            pl.BlockSpec((1, 1, D_MODEL), lambda b, i: (b, 0, 0)),
            pl.BlockSpec((D_MODEL, ROUTER_LANES), lambda b, i: (0, 0)),
        ],
        out_specs=[
            pl.BlockSpec((1, tm, D_MODEL // 2), lambda b, i: (b, i, 0)),
            pl.BlockSpec((1, 8, tm), lambda b, i: (b, 0, i)),
            pl.BlockSpec((1, tm, ROUTER_LANES), lambda b, i: (b, i, 0)),
        ],
        out_shape=[
            jax.ShapeDtypeStruct((BATCH, SEQ, D_MODEL // 2), jnp.int32),
            jax.ShapeDtypeStruct((BATCH, 8, SEQ), jnp.int32),
            jax.ShapeDtypeStruct((BATCH, SEQ, ROUTER_LANES), F32),
        ],
        compiler_params=_cparams(("parallel", "parallel")),
        name="route",
    )(x, g.reshape(1, D_MODEL), scale, shift, w_router)


def _expert_kernel(be_ref, nu_ref, first_ref, nxt_ref, x_ref, wgu_hbm, wd_hbm, o_ref,
                   gu_stage, d_stage, wgu, wd, sem):
    i = pl.program_id(0)
    active = i < nu_ref[0]
    is_first = first_ref[i] == 1
    half = D_MODEL // 2

    def weight_copies(e):
        return (pltpu.make_async_copy(wgu_hbm.at[e], gu_stage, sem.at[0]),
                pltpu.make_async_copy(wd_hbm.at[e], d_stage, sem.at[1]))

    def swiglu(gu):
        g = gu[:, :EXPERT_FF]
        return (g * _sigmoid(g) * gu[:, EXPERT_FF:]).astype(BF16)

    def rounded_matmul(lhs, stage, dst):
        acc = None
        for r0 in range(0, stage.shape[0], MOE_CAST_ROWS):
            rows = slice(r0, r0 + MOE_CAST_ROWS)
            w = stage[rows, :].astype(BF16)
            dst[rows, :] = w
            part = jnp.dot(lhs[:, rows], w, preferred_element_type=F32)
            acc = part if acc is None else acc + part
        return acc

    @pl.when(i == 0)
    def _():
        for cp in weight_copies(be_ref[0]):
            cp.start()

    @pl.when(active & is_first)
    def _():
        for cp in weight_copies(be_ref[i]):
            cp.wait()
        x_first, x_second = _unpack_halves(x_ref[...])
        x = jnp.concatenate([x_first.astype(BF16), x_second.astype(BF16)], axis=1)
        act = swiglu(rounded_matmul(x, gu_stage, wgu))
        o_ref[...] = _pack_halves(rounded_matmul(act, d_stage, wd))

        @pl.when(nxt_ref[i] >= 0)
        def _():
            for cp in weight_copies(nxt_ref[i]):
                cp.start(priority=1)

    @pl.when(active & jnp.logical_not(is_first))
    def _():
        x_first, x_second = _unpack_halves(x_ref[...])
        gu = (jnp.dot(x_first.astype(BF16), wgu[0:half, :], preferred_element_type=F32)
              + jnp.dot(x_second.astype(BF16), wgu[half:, :], preferred_element_type=F32))
        o_ref[...] = _pack_halves(jnp.dot(swiglu(gu), wd[...], preferred_element_type=F32))

    @pl.when(jnp.logical_not(active))
    def _():
        o_ref[...] = jnp.zeros(o_ref.shape, o_ref.dtype)


def expert_blocks(xs, blk_expert, n_used, first, nxt, w_gate_up, w_down):
    row_map = lambda i, be, nu, first, nxt: (jnp.minimum(i, nu[0] - 1), 0)
    return pl.pallas_call(
        _expert_kernel,
        grid_spec=pltpu.PrefetchScalarGridSpec(
            num_scalar_prefetch=4,
            grid=(MOE_NBLK,),
            in_specs=[
                pl.BlockSpec((MOE_TB, D_MODEL // 2), row_map),
                pl.BlockSpec(memory_space=pl.ANY),
                pl.BlockSpec(memory_space=pl.ANY),
            ],
            out_specs=pl.BlockSpec((MOE_TB, D_MODEL // 2), lambda i, be, nu, first, nxt: (i, 0)),
            scratch_shapes=[
                pltpu.VMEM((D_MODEL, 2 * EXPERT_FF), F32),
                pltpu.VMEM((EXPERT_FF, D_MODEL), F32),
                pltpu.VMEM((D_MODEL, 2 * EXPERT_FF), BF16),
                pltpu.VMEM((EXPERT_FF, D_MODEL), BF16),
                pltpu.SemaphoreType.DMA((2,)),
            ],
        ),
        out_shape=jax.ShapeDtypeStruct((MOE_ROWS, D_MODEL // 2), jnp.int32),
        compiler_params=pltpu.CompilerParams(dimension_semantics=("arbitrary",), vmem_limit_bytes=BIG_VMEM_LIMIT),
        name="expert_blocks",
    )(blk_expert, n_used, first, nxt, xs, w_gate_up, w_down)


def _router_weights(w_group, w_expert):
    w = jnp.zeros((D_MODEL, ROUTER_LANES), F32)
    w = w.at[:, 0:N_GROUPS].set(w_group)
    return w.at[:, 8:8 + N_EXPERTS].set(w_expert)


def _route_meta_kernel(e_ref, dest_ref, be_ref, first_ref, nxt_ref, nu_ref):
    nrow = e_ref.shape[0]
    expert = lax.broadcasted_iota(jnp.int32, (N_EXPERTS, META_LANES), 0)
    upto = (lax.broadcasted_iota(jnp.int32, (META_LANES, META_LANES), 0)
            <= lax.broadcasted_iota(jnp.int32, (META_LANES, META_LANES), 1)).astype(BF16)

    def count_row(c, acc):
        return acc + jnp.where(expert == e_ref[c], 1.0, 0.0)

    acc = lax.fori_loop(0, nrow, count_row, jnp.zeros((N_EXPERTS, META_LANES), F32))
    counts = jnp.sum(acc, axis=1, keepdims=True)
    pcounts = jnp.floor((counts + (MOE_TB - 1.0)) * (1.0 / MOE_TB)) * MOE_TB
    ends = []
    run = jnp.zeros((1, 1), F32)
    for e in range(N_EXPERTS):
        run = run + pcounts[e:e + 1, :]
        ends.append(run)
    pend = jnp.concatenate(ends, axis=0)
    pstart = pend - pcounts

    def dest_row(c, running):
        hit = expert == e_ref[c]
        seen = jnp.dot(jnp.where(hit, 1.0, 0.0).astype(BF16), upto, preferred_element_type=F32)
        slot = seen - 1.0 + (running + pstart)
        dest_ref[c] = jnp.sum(jnp.where(hit, slot, 0.0), axis=0, keepdims=True).astype(jnp.int32)
        return running + seen[:, META_LANES - 1:META_LANES]

    lax.fori_loop(0, nrow, dest_row, jnp.zeros((N_EXPERTS, 1), F32))

    blk = lax.broadcasted_iota(jnp.int32, (1, META_LANES), 1).astype(F32)
    n_used = pend[N_EXPERTS - 1:N_EXPERTS, :] * (1.0 / MOE_TB)

    def expert_at(b):
        start = jnp.minimum(b, n_used - 1.0) * MOE_TB
        return jnp.minimum(jnp.sum(jnp.where(pend <= start, 1.0, 0.0), axis=0, keepdims=True), N_EXPERTS - 1.0)

    be = expert_at(blk)
    is_first = jnp.logical_and(jnp.logical_or(blk == 0.0, be != expert_at(blk - 1.0)), blk < n_used)
    later = jnp.logical_and(expert.astype(F32) > be, counts > 0.0)
    nxt = jnp.min(jnp.where(later, expert.astype(F32), 2.0 * N_EXPERTS), axis=0, keepdims=True)
    be_ref[...] = be.astype(jnp.int32)
    first_ref[...] = jnp.where(is_first, 1, 0)
    nxt_ref[...] = jnp.where(nxt < N_EXPERTS, nxt, -1.0).astype(jnp.int32)
    nu_ref[...] = jnp.broadcast_to(n_used, (1, META_LANES)).astype(jnp.int32)


def route_metadata(e_flat):
    nrow = e_flat.shape[0] // META_LANES
    lane_row = jax.ShapeDtypeStruct((1, META_LANES), jnp.int32)
    dest, be, first, nxt, nu = pl.pallas_call(
        _route_meta_kernel,
        out_shape=[jax.ShapeDtypeStruct((nrow, 1, META_LANES), jnp.int32), lane_row, lane_row, lane_row, lane_row],
        compiler_params=pltpu.CompilerParams(vmem_limit_bytes=VMEM_LIMIT),
        name="route_metadata",
    )(e_flat.reshape(nrow, 1, META_LANES))
    return dest.reshape(-1), be[0, :MOE_NBLK], first[0, :MOE_NBLK], nxt[0, :MOE_NBLK], nu[0, :1]


def sc_move_rows(src, idx, n_out, scatter):
    n_idx = idx.shape[0]
    n_src, width = src.shape
    n_workers = SC_CORES * SC_SUBCORES
    per_w = n_idx // n_workers
    nchunk = per_w // SC_CHUNK
    assert per_w * n_workers == n_idx and nchunk * SC_CHUNK == per_w and nchunk % 2 == 0
    assert per_w % n_src == 0 or n_src % per_w == 0
    mesh = plsc.VectorSubcoreMesh(core_axis_name="c", subcore_axis_name="s")

    @functools.partial(
        pl.kernel, mesh=mesh,
        out_type=jax.ShapeDtypeStruct((n_out, width), src.dtype),
        scratch_types=[
            pltpu.VMEM((nchunk, SC_CHUNK), jnp.int32),
            pltpu.VMEM((2, SC_CHUNK, width), src.dtype),
            pltpu.SemaphoreType.DMA((2,)),
            pltpu.SemaphoreType.DMA((2,)),
        ],
    )
    def move(src_hbm, idx_hbm, out_hbm, idx_v, rows_v, in_sem, out_sem):
        wid = lax.axis_index("s") * SC_CORES + lax.axis_index("c")
        base = wid * per_w
        pltpu.sync_copy(idx_hbm.at[wid], idx_v)

        def load(c, b):
            if scatter:
                rows = src_hbm.at[pl.ds(lax.rem(base, n_src) + c * SC_CHUNK, SC_CHUNK)]
            else:
                rows = src_hbm.at[idx_v.at[c]]
            return pltpu.make_async_copy(rows, rows_v.at[b], in_sem.at[b])

        def store(c, b):
            if scatter:
                rows = out_hbm.at[idx_v.at[c]]
            else:
                rows = out_hbm.at[pl.ds(base + c * SC_CHUNK, SC_CHUNK)]
            return pltpu.make_async_copy(rows_v.at[b], rows, out_sem.at[b])

        load(0, 0).start()

        @pl.loop(0, nchunk, step=2)
        def _(c0):
            for b in (0, 1):
                c = c0 + b
                load(c, b).wait()

                @pl.when(c + 1 < nchunk)
                def _():
                    @pl.when(c >= 1)
                    def _():
                        store(c - 1, 1 - b).wait()

                    load(c + 1, 1 - b).start()

                store(c, b).start()

        store(nchunk - 2, 0).wait()
        store(nchunk - 1, 1).wait()

    return move(src, idx.reshape(n_workers, nchunk, SC_CHUNK))


def _combine_kernel(x_ref, z0_ref, z1_ref, gates_ref, ada_ref, ng_ref, o_ref, *, final):
    gates = gates_ref[0]
    g0, g1 = gates[:, 0:1], gates[:, 1:2]
    z0_first, z0_second = _unpack_halves(z0_ref[...])
    z1_first, z1_second = _unpack_halves(z1_ref[...])
    y = jnp.concatenate([g0 * z0_first + g1 * z1_first, g0 * z0_second + g1 * z1_second], axis=1)
    x = x_ref[0] + ada_ref[0] * y
    if final:
        x = x * lax.rsqrt(jnp.mean(x * x, axis=-1, keepdims=True) + EPS) * ng_ref[...]
    o_ref[0] = x


def moe_combine(x, z, gates, gate_ada, norm_g, tm=512):
    final = norm_g is not None
    if not final:
        norm_g = jnp.ones((D_MODEL,), F32)
    nt = SEQ // tm
    return pl.pallas_call(
        functools.partial(_combine_kernel, final=final),
        grid=(BATCH, nt),
        in_specs=[
            pl.BlockSpec((1, tm, D_MODEL), lambda b, i: (b, i, 0)),
            pl.BlockSpec((tm, D_MODEL // 2), lambda b, i: (b * nt + i, 0)),
            pl.BlockSpec((tm, D_MODEL // 2), lambda b, i: (BATCH * nt + b * nt + i, 0)),
            pl.BlockSpec((1, tm, ROUTER_LANES), lambda b, i: (b, i, 0)),
            pl.BlockSpec((1, 1, D_MODEL), lambda b, i: (b, 0, 0)),
            pl.BlockSpec((1, D_MODEL), lambda b, i: (0, 0)),
        ],
        out_specs=pl.BlockSpec((1, tm, D_MODEL), lambda b, i: (b, i, 0)),
        out_shape=jax.ShapeDtypeStruct((BATCH, SEQ, D_MODEL), F32),
        compiler_params=_cparams(("parallel", "parallel")),
        name="moe_combine",
    )(x, z, z, gates, gate_ada, norm_g.reshape(1, D_MODEL))


def hierarchical_moe(x, c, norm_g, ada_w, ada_b, w_group, w_expert, w_gate_up, w_down, final_norm_g=None):
    shift, scale, gate_ada = ada_modulation(c, ada_w, ada_b)
    h, eid, gates = route(x, norm_g, scale, shift, _router_weights(w_group, w_expert))
    e_flat = jnp.concatenate([eid[:, 0, :].reshape(N_TOK), eid[:, 1, :].reshape(N_TOK)])
    dest, blk_expert, first, nxt, n_used = route_metadata(e_flat)
    xs = sc_move_rows(h.reshape(N_TOK, D_MODEL // 2), dest, MOE_ROWS, scatter=True)
    yb = expert_blocks(xs, blk_expert, n_used, first, nxt, w_gate_up, w_down)
    z = sc_move_rows(yb, dest, TOP_K * N_TOK, scatter=False)
    return moe_combine(x, z, gates, gate_ada, final_norm_g)


def kernel(x, c, norm0_mix, ada0_mix_w, ada0_mix_b, w_in0, conv_w, conv_b, conv_norm_g, conv_norm_b, w_out0, norm0_ffn, ada0_ffn_w, ada0_ffn_b, moe0_w_group, moe0_w_expert, moe0_w_gate_up, moe0_w_down, norm1_mix, ada1_mix_w, ada1_mix_b, w_in1, lru_conv_w, lru_conv_b, lru_w_a, lru_b_a, lru_w_x, lru_b_x, lru_lambda, w_out1, norm1_ffn, ada1_ffn_w, ada1_ffn_b, moe1_w_group, moe1_w_expert, moe1_w_gate_up, moe1_w_down, norm_final):
    x = conv_attention_layer(x, c, norm0_mix, ada0_mix_w, ada0_mix_b, w_in0, conv_w, conv_b, conv_norm_g, conv_norm_b,
                             w_out0)
    x = hierarchical_moe(x, c, norm0_ffn, ada0_ffn_w, ada0_ffn_b, moe0_w_group, moe0_w_expert,
                         moe0_w_gate_up, moe0_w_down)

    x = lru_mixer_layer(x, c, norm1_mix, ada1_mix_w, ada1_mix_b, w_in1, lru_conv_w, lru_conv_b, lru_w_a, lru_b_a,
                        lru_w_x, lru_b_x, lru_lambda, w_out1)
    x = hierarchical_moe(x, c, norm1_ffn, ada1_ffn_w, ada1_ffn_b, moe1_w_group, moe1_w_expert,
                         moe1_w_gate_up, moe1_w_down, final_norm_g=norm_final)
    return x
```

```python
import functools

import jax
import jax.numpy as jnp
from jax import lax
from jax.experimental import pallas as pl
from jax.experimental.pallas import tpu as pltpu
from jax.experimental.pallas import tpu_sc as plsc

F32 = jnp.float32
BF16 = jnp.bfloat16

D_MODEL = 2048
BATCH = 4
SEQ = 4096
N_TOK = BATCH * SEQ
EPS = 1e-6
NEG_INF = -1e30

CONV_CH = 1024
CONV_GROUPS = 8
CONV_WIDTH = 31
CONV_HALO = 16
ATT_HEADS = 8
HEAD_DIM = 128
ATT_W = ATT_HEADS * HEAD_DIM
ATT_BRANCHES = ((128, 1), (512, 4), (2048, 16))
ATT_RADIUS = 64
ATT_QB = 128
ATT_KB = ATT_QB + 2 * ATT_RADIUS
ATT_UNROLL = 8
ATT_COPY_ROWS = 512
ALIBI_MAX = 8.0

LRU_WIDTH = 2688
LRU_BLOCKS = 16
LRU_BW = LRU_WIDTH // LRU_BLOCKS
LRU_PW = 192
LRU_WP = LRU_BLOCKS * LRU_PW
LRU_GW = 2 * LRU_PW
LRU_CONV = 4
LRU_C = 8.0
LRU_CHUNKS = 8
LRU_CL = SEQ // LRU_CHUNKS
LRU_PITCH = LRU_CL + 8
LOG2E = 1.4426950408889634

N_GROUPS = 4
EXPERTS_PER_GROUP = 8
N_EXPERTS = 32
TOP_K = 2
EXPERT_FF = 1024
MOE_TB = 256
MOE_ROWS = N_TOK * TOP_K + N_EXPERTS * MOE_TB
MOE_NBLK = MOE_ROWS // MOE_TB
MOE_CAST_ROWS = 256
META_LANES = 256
assert MOE_NBLK <= META_LANES
SC_CORES = 2
SC_SUBCORES = 16
SC_CHUNK = 32
ROUTER_LANES = 128

MM_ROWS = 1024
MM_WIDE_COLS = 2048
VMEM_LIMIT = 48 * 1024 * 1024
BIG_VMEM_LIMIT = 56 * 1024 * 1024


def _cparams(sem):
    return pltpu.CompilerParams(dimension_semantics=sem, vmem_limit_bytes=VMEM_LIMIT)


def _sigmoid(x):
    return 0.5 * jnp.tanh(0.5 * x) + 0.5


def _ada_kernel(ct_ref, w_ref, b_ref, o_ref):
    ct = ct_ref[...]
    st = ct * _sigmoid(ct)
    w = w_ref[...]
    rows = [jnp.sum(w * st[:, b:b + 1], axis=0, keepdims=True) for b in range(BATCH)]
    o_ref[...] = jnp.concatenate(rows, axis=0) + b_ref[...]


def ada_modulation(c, w, b):
    tn = 512
    n = w.shape[1]
    mod = pl.pallas_call(
        _ada_kernel,
        grid=(n // tn,),
        in_specs=[
            pl.BlockSpec((D_MODEL, BATCH), lambda j: (0, 0)),
            pl.BlockSpec((D_MODEL, tn), lambda j: (0, j)),
            pl.BlockSpec((1, tn), lambda j: (0, j)),
        ],
        out_specs=pl.BlockSpec((BATCH, tn), lambda j: (0, j)),
        out_shape=jax.ShapeDtypeStruct((BATCH, n), F32),
        compiler_params=_cparams(("parallel",)),
        name="ada_modulation",
    )(c.T, w, b.reshape(1, n))
    shift, scale, gate = jnp.split(mod, 3, axis=-1)
    return shift[:, None, :], scale[:, None, :], gate[:, None, :]


def _modulated_norm(x, g, scale, shift):
    y = x * lax.rsqrt(jnp.mean(x * x, axis=-1, keepdims=True) + EPS)
    return (y * g) * (1.0 + scale) + shift


def _norm_kernel(x_ref, g_ref, sc_ref, sh_ref, o_ref):
    o_ref[0] = _modulated_norm(x_ref[0], g_ref[...], sc_ref[0], sh_ref[0]).astype(o_ref.dtype)


def modulated_norm(x, g, scale, shift, tm=512):
    return pl.pallas_call(
        _norm_kernel,
        grid=(BATCH, SEQ // tm),
        in_specs=[
            pl.BlockSpec((1, tm, D_MODEL), lambda b, i: (b, i, 0)),
            pl.BlockSpec((1, D_MODEL), lambda b, i: (0, 0)),
            pl.BlockSpec((1, 1, D_MODEL), lambda b, i: (b, 0, 0)),
            pl.BlockSpec((1, 1, D_MODEL), lambda b, i: (b, 0, 0)),
        ],
        out_specs=pl.BlockSpec((1, tm, D_MODEL), lambda b, i: (b, i, 0)),
        out_shape=jax.ShapeDtypeStruct((BATCH, SEQ, D_MODEL), BF16),
        compiler_params=_cparams(("parallel", "parallel")),
        name="modulated_norm",
    )(x, g.reshape(1, D_MODEL), scale, shift)


def _mm_kernel(a_ref, w_ref, o_ref, *, head_major):
    res = jnp.dot(a_ref[0], w_ref[...], preferred_element_type=F32)
    if head_major:
        for hh in range(ATT_HEADS):
            o_ref[0, 0, hh] = res[:, hh * HEAD_DIM:(hh + 1) * HEAD_DIM].astype(o_ref.dtype)
    else:
        o_ref[0] = res.astype(o_ref.dtype)


def project(a, w, out_dtype, head_major=False, tm=MM_ROWS, tn=1024):
    k, n = w.shape
    if head_major:
        assert tn == ATT_W
        out_shape = jax.ShapeDtypeStruct((n // tn, BATCH, ATT_HEADS, SEQ, HEAD_DIM), out_dtype)
        out_spec = pl.BlockSpec((1, 1, ATT_HEADS, tm, HEAD_DIM), lambda b, i, j: (j, b, 0, i, 0))
    else:
        out_shape = jax.ShapeDtypeStruct((BATCH, SEQ, n), out_dtype)
        out_spec = pl.BlockSpec((1, tm, tn), lambda b, i, j: (b, i, j))
    return pl.pallas_call(
        functools.partial(_mm_kernel, head_major=head_major),
        grid=(BATCH, SEQ // tm, n // tn),
        in_specs=[
            pl.BlockSpec((1, tm, k), lambda b, i, j: (b, i, 0)),
            pl.BlockSpec((k, tn), lambda b, i, j: (0, j)),
        ],
        out_specs=out_spec,
        out_shape=out_shape,
        compiler_params=_cparams(("parallel", "parallel", "parallel")),
        name="project",
    )(a, w)


def _mm_residual_kernel(*refs, n_in):
    a_refs, w_refs = refs[:n_in], refs[n_in:2 * n_in]
    x_ref, gate_ref, o_ref = refs[2 * n_in:]
    acc = jnp.dot(a_refs[0][0], w_refs[0][...], preferred_element_type=F32)
    for a_ref, w_ref in zip(a_refs[1:], w_refs[1:]):
        acc = acc + jnp.dot(a_ref[0], w_ref[...], preferred_element_type=F32)
    o_ref[0] = x_ref[0] + gate_ref[0] * acc


def project_residual(a_list, w_list, x, gate, tm=MM_ROWS, tn=1024):
    n_in = len(a_list)
    grid = (BATCH, SEQ // tm, D_MODEL // tn)
    in_specs = [pl.BlockSpec((1, tm, a.shape[-1]), lambda b, i, j: (b, i, 0)) for a in a_list]
    in_specs += [pl.BlockSpec((w.shape[0], tn), lambda b, i, j: (0, j)) for w in w_list]
    in_specs += [
        pl.BlockSpec((1, tm, tn), lambda b, i, j: (b, i, j)),
        pl.BlockSpec((1, 1, tn), lambda b, i, j: (b, 0, j)),
    ]
    return pl.pallas_call(
        functools.partial(_mm_residual_kernel, n_in=n_in),
        grid=grid,
        in_specs=in_specs,
        out_specs=pl.BlockSpec((1, tm, tn), lambda b, i, j: (b, i, j)),
        out_shape=jax.ShapeDtypeStruct((BATCH, SEQ, D_MODEL), F32),
        compiler_params=_cparams(("parallel", "parallel", "parallel")),
        name="project_residual",
    )(*a_list, *w_list, x, gate)


def _conv_kernel(v_ref, g_ref, cw_ref, cb_ref, ng_ref, nb_ref, o_ref, pad_scr, *, ts):
    zeros = jnp.zeros((CONV_HALO, 128), F32)
    pad_scr[0:CONV_HALO, :] = zeros
    pad_scr[SEQ + CONV_HALO:SEQ + 2 * CONV_HALO, :] = zeros
    for t0 in range(0, SEQ, ts):
        pad_scr[CONV_HALO + t0:CONV_HALO + t0 + ts, :] = v_ref[0, t0:t0 + ts, :] * _sigmoid(g_ref[0, t0:t0 + ts, :])
    first = CONV_HALO - CONV_WIDTH // 2
    for t0 in range(0, SEQ, ts):
        acc = cw_ref[0:1, :] * pad_scr[first + t0:first + t0 + ts, :] + cb_ref[...]
        for k in range(1, CONV_WIDTH):
            acc = acc + cw_ref[k:k + 1, :] * pad_scr[first + t0 + k:first + t0 + k + ts, :]
        mu = jnp.mean(acc, axis=-1, keepdims=True)
        cen = acc - mu
        var = jnp.mean(cen * cen, axis=-1, keepdims=True)
        y = cen * lax.rsqrt(var + EPS) * ng_ref[...] + nb_ref[...]
        o_ref[0, t0:t0 + ts, :] = (y * _sigmoid(y)).astype(o_ref.dtype)


def conv_module(p, conv_w, conv_b, norm_g, norm_b, ts=128):
    ng = CONV_GROUPS
    vec = lambda a: a.reshape(1, CONV_CH)
    vspec = pl.BlockSpec((1, 128), lambda b, c: (0, c))
    return pl.pallas_call(
        functools.partial(_conv_kernel, ts=ts),
        grid=(BATCH, ng),
        in_specs=[
            pl.BlockSpec((1, SEQ, 128), lambda b, c: (b, 0, c)),
            pl.BlockSpec((1, SEQ, 128), lambda b, c: (b, 0, c + ng)),
            pl.BlockSpec((CONV_WIDTH, 128), lambda b, c: (0, c)),
            vspec, vspec, vspec,
        ],
        out_specs=pl.BlockSpec((1, SEQ, 128), lambda b, c: (b, 0, c)),
        out_shape=jax.ShapeDtypeStruct((BATCH, SEQ, CONV_CH), BF16),
        scratch_shapes=[pltpu.VMEM((SEQ + 2 * CONV_HALO, 128), F32)],
        compiler_params=_cparams(("parallel", "parallel")),
        name="conv_module",
    )(p, p, conv_w, vec(conv_b), vec(norm_g), vec(norm_b))


def _attn_kernel(q_ref, k_ref, v_ref, *rest):
    nbr = len(ATT_BRANCHES)
    bias_refs, o_ref = rest[:nbr], rest[nbr]
    qf, kf, vf, qd, kpad, vpad, acc, mx, den = rest[nbr + 1:]
    scale = HEAD_DIM ** -0.5
    zeros = jnp.zeros((ATT_RADIUS, HEAD_DIM), BF16)
    for t0 in range(0, SEQ, ATT_COPY_ROWS):
        rows = slice(t0, t0 + ATT_COPY_ROWS)
        qf[rows, :] = q_ref[0, 0, rows, :].astype(F32)
        kf[rows, :] = k_ref[0, 0, rows, :].astype(F32)
        vf[rows, :] = v_ref[0, 0, rows, :].astype(F32)

    order = sorted(range(nbr), key=lambda n: -ATT_BRANCHES[n][1])
    for pos, n in enumerate(order):
        dil = ATT_BRANCHES[n][1]
        bias_ref = bias_refs[n]
        sub = SEQ // dil
        nblk = sub // ATT_QB
        first, last = pos == 0, pos == nbr - 1
        assert dil == 1 or not last

        def class_rows(r, start, count, dil=dil):
            if dil == 1:
                return pl.ds(start, count)
            return pl.ds(r + start * dil, count, stride=dil)

        group = max(1, min(dil, ATT_UNROLL // nblk))
        kstride = sub + 2 * ATT_RADIUS

        def residue_classes(g, carry, dil=dil, sub=sub, nblk=nblk, bias_ref=bias_ref, first=first, last=last,
                            class_rows=class_rows, group=group, kstride=kstride):
            for cc in range(group):
                r = g * group + cc
                k0 = cc * kstride
                for pad in (kpad, vpad):
                    pad[k0:k0 + ATT_RADIUS, :] = zeros
                    pad[k0 + sub + ATT_RADIUS:k0 + kstride, :] = zeros
                for c0 in range(0, sub, ATT_COPY_ROWS):
                    n_rows = min(ATT_COPY_ROWS, sub - c0)
                    src = class_rows(r, c0, n_rows)
                    qd[cc * sub + c0:cc * sub + c0 + n_rows, :] = qf[src, :].astype(BF16)
                    kpad[k0 + ATT_RADIUS + c0:k0 + ATT_RADIUS + c0 + n_rows, :] = kf[src, :].astype(BF16)
                    vpad[k0 + ATT_RADIUS + c0:k0 + ATT_RADIUS + c0 + n_rows, :] = vf[src, :].astype(BF16)

            def block(i, r, cc):
                if isinstance(i, int):
                    q0 = i * ATT_QB
                    variant = 0 if i == 0 else (2 if i == nblk - 1 else 1)
                else:
                    q0 = pl.multiple_of(i * ATT_QB, ATT_QB)
                    variant = jnp.where(i == 0, 0, jnp.where(i == nblk - 1, 2, 1))
                s = lax.dot_general(qd[pl.ds(cc * sub + q0, ATT_QB), :], kpad[pl.ds(cc * kstride + q0, ATT_KB), :],
                                    (((1,), (1,)), ((), ())), preferred_element_type=F32)
                s = s * scale + bias_ref[0, variant]
                m = jnp.max(s, axis=-1, keepdims=True)
                e = jnp.exp(s - m)
                l = jnp.sum(e, axis=-1, keepdims=True)
                o = jnp.dot(e.astype(BF16), vpad[pl.ds(cc * kstride + q0, ATT_KB), :], preferred_element_type=F32)
                nat = class_rows(r, q0, ATT_QB)
                wide = (ATT_QB, HEAD_DIM)
                if first:
                    acc[nat, :] = o
                    mx[nat, :] = jnp.broadcast_to(m, wide)
                    den[nat, :] = jnp.broadcast_to(l, wide)
                else:
                    m_old = mx[nat, :]
                    m_new = jnp.maximum(m_old, m)
                    alpha = jnp.exp(m_old - m_new)
                    beta = jnp.exp(m - m_new)
                    total = acc[nat, :] * alpha + o * beta
                    l_new = den[nat, :] * alpha + l * beta
                    if last:
                        o_ref[0, nat, :] = (total / l_new).astype(o_ref.dtype)
                    else:
                        acc[nat, :] = total
                        mx[nat, :] = m_new
                        den[nat, :] = l_new

            if group == 1:
                def body(i, carry):
                    block(i, g, 0)
                    return carry

                lax.fori_loop(0, nblk, body, 0, unroll=min(nblk, ATT_UNROLL))
            else:
                for cc in range(group):
                    for i in range(nblk):
                        block(i, g * group + cc, cc)
            return carry

        if dil == 1:
            residue_classes(0, 0)
        else:
            lax.fori_loop(0, dil // group, residue_classes, 0)


def _alibi_bias(dilation):
    row = jnp.arange(ATT_QB)[:, None]
    colm = jnp.arange(ATT_KB)[None, :]
    rel = jnp.abs(colm - ATT_RADIUS - row)
    slopes = 2.0 ** (-ALIBI_MAX * jnp.arange(1, ATT_HEADS + 1, dtype=F32) / ATT_HEADS)
    bias = -slopes[:, None, None] * (rel * dilation).astype(F32)[None]
    band = rel <= ATT_RADIUS
    masks = jnp.stack([band & (colm >= ATT_RADIUS), band, band & (colm < ATT_KB - ATT_RADIUS)])
    return jnp.where(masks[None], bias[:, None], NEG_INF)


def dilated_attention(qkv):
    for window, dilation in ATT_BRANCHES:
        assert window // (2 * dilation) == ATT_RADIUS
        assert SEQ // dilation >= 2 * ATT_QB
    qspec = lambda which: pl.BlockSpec((None, 1, 1, SEQ, HEAD_DIM), lambda b, h, which=which: (which, b, h, 0, 0))
    bias_spec = pl.BlockSpec((1, 3, ATT_QB, ATT_KB), lambda b, h: (h, 0, 0, 0))
    rows = lambda dtype, n=SEQ: pltpu.VMEM((n, HEAD_DIM), dtype)
    return pl.pallas_call(
        _attn_kernel,
        grid=(BATCH, ATT_HEADS),
        in_specs=[qspec(0), qspec(1), qspec(2)] + [bias_spec] * len(ATT_BRANCHES),
        out_specs=pl.BlockSpec((1, SEQ, HEAD_DIM), lambda b, h: (b, 0, h)),
        out_shape=jax.ShapeDtypeStruct((BATCH, SEQ, ATT_W), BF16),
        scratch_shapes=[rows(F32), rows(F32), rows(F32), rows(BF16),
                        rows(BF16, SEQ + 2 * ATT_RADIUS), rows(BF16, SEQ + 2 * ATT_RADIUS),
                        rows(F32), rows(F32), rows(F32)],
        compiler_params=_cparams(("parallel", "parallel")),
        name="dilated_attention",
    )(qkv, qkv, qkv, *[_alibi_bias(dilation) for _, dilation in ATT_BRANCHES])


def conv_attention_layer(x, c, norm_g, ada_w, ada_b, w_in, conv_w, conv_b, conv_norm_g, conv_norm_b, w_out):
    shift, scale, gate = ada_modulation(c, ada_w, ada_b)
    w_in = w_in.astype(BF16)
    h = modulated_norm(x, norm_g, scale, shift)
    p_conv = project(h, w_in[:, :2 * CONV_CH], F32, tn=MM_WIDE_COLS)
    qkv = project(h, w_in[:, 2 * CONV_CH:], BF16, head_major=True)
    a = conv_module(p_conv, conv_w, conv_b, conv_norm_g, conv_norm_b)
    o = dilated_attention(qkv)
    w_out = w_out.astype(BF16)
    return project_residual([a, o], [w_out[:CONV_CH], w_out[CONV_CH:]], x, gate)


def _softplus(x):
    return jnp.maximum(x, 0.0) + jnp.log1p(jnp.exp(-jnp.abs(x)))


def _gelu_tanh(x):
    return 0.5 * x * (1.0 + jnp.tanh(0.7978845608028654 * (x + 0.044715 * (x * x * x))))


def _lru_kernel(x_ref, cw_ref, cb_ref, w_ref, b_ref, lam_ref, o_ref, xc, a_scr, u_scr, yacc):
    gw = LRU_GW
    ntile = gw // 128
    lead = LRU_CONV // 2

    def shifted(r0, off):
        lo, hi = r0 + off, r0 + off + LRU_CL
        parts = []
        if lo < 0:
            parts.append(jnp.zeros((-lo, gw), F32))
        parts.append(x_ref[0, max(lo, 0):min(hi, SEQ), :])
        if hi > SEQ:
            parts.append(jnp.zeros((hi - SEQ, gw), F32))
        return parts[0] if len(parts) == 1 else jnp.concatenate(parts, axis=0)

    for j in range(LRU_CHUNKS):
        r0 = j * LRU_CL
        acc = cw_ref[0:1, :] * shifted(r0, -lead) + cb_ref[...]
        for k in range(1, LRU_CONV):
            acc = acc + cw_ref[k:k + 1, :] * shifted(r0, k - lead)
        xc[r0:r0 + LRU_CL, :] = acc

    for d in range(2):
        reverse = d == 1
        half_decay = (0.5 * LRU_C) * _softplus(-lam_ref[d:d + 1, :])
        for j in range(LRU_CHUNKS):
            xj = xc[j * LRU_CL:(j + 1) * LRU_CL, :]
            pre = jnp.dot(xj.astype(BF16), w_ref[d, 0], preferred_element_type=F32) + b_ref[d, 0]
            t_r = jnp.tanh(pre[:, :gw])
            t_i = jnp.tanh(pre[:, gw:])
            neg_log_a = t_r * half_decay + half_decay
            a = jnp.exp2(neg_log_a * (-LOG2E))
            one_minus_a2 = jnp.tanh(neg_log_a) * (a * a + 1.0)
            root = one_minus_a2 * lax.rsqrt(jnp.maximum(one_minus_a2, 1e-30))
            u = root * ((0.5 * t_i + 0.5) * xj)
            for lt in range(ntile):
                rows = slice(j * LRU_PITCH, j * LRU_PITCH + LRU_CL)
                a_scr[lt, rows, :] = a[:, lt * 128:(lt + 1) * 128]
                u_scr[lt, rows, :] = u[:, lt * 128:(lt + 1) * 128]

        def chunk_rows(row, grp):
            return pl.ds(row + grp * 8 * LRU_PITCH, 8, stride=LRU_PITCH)

        chains = [(lt, grp) for lt in range(ntile) for grp in range(LRU_CHUNKS // 8)]

        def step(ii, carry):
            row = (LRU_CL - 1 - ii) if reverse else ii
            out = []
            for (lt, grp), (h, prod) in zip(chains, carry):
                a = a_scr[lt, chunk_rows(row, grp), :]
                h = a * h + u_scr[lt, chunk_rows(row, grp), :]
                prod = a * prod
                u_scr[lt, chunk_rows(row, grp), :] = h
                a_scr[lt, chunk_rows(row, grp), :] = prod
                out.append((h, prod))
            return tuple(out)

        init = tuple((jnp.zeros((8, 128), F32), jnp.ones((8, 128), F32)) for _ in chains)
        lax.fori_loop(0, LRU_CL, step, init, unroll=4)

        last = 0 if reverse else LRU_CL - 1
        for lt in range(ntile):
            lanes = slice(lt * 128, (lt + 1) * 128)
            h_end = [u_scr[lt, chunk_rows(last, grp), :] for grp in range(LRU_CHUNKS // 8)]
            p_end = [a_scr[lt, chunk_rows(last, grp), :] for grp in range(LRU_CHUNKS // 8)]
            carry = jnp.zeros((1, 128), F32)
            for j in (range(LRU_CHUNKS - 1, -1, -1) if reverse else range(LRU_CHUNKS)):
                src = slice(j * LRU_PITCH, j * LRU_PITCH + LRU_CL)
                dst = slice(j * LRU_CL, (j + 1) * LRU_CL)
                y = u_scr[lt, src, :] + a_scr[lt, src, :] * carry
                if reverse:
                    o_ref[0, dst, lanes] = (yacc[dst, lanes] + y).astype(o_ref.dtype)
                else:
                    yacc[dst, lanes] = y
                carry = h_end[j // 8][j % 8:j % 8 + 1, :] + p_end[j // 8][j % 8:j % 8 + 1, :] * carry


def rglru_block(p, conv_w, conv_b, w_gates, b_gates, lam):
    npair = LRU_WP // LRU_GW
    gw = LRU_GW
    big = lambda: pltpu.VMEM((SEQ, gw), F32)
    slabs = lambda: pltpu.VMEM((gw // 128, LRU_CHUNKS * LRU_PITCH, 128), F32)
    return pl.pallas_call(
        _lru_kernel,
        grid=(BATCH, npair),
        in_specs=[
            pl.BlockSpec((1, SEQ, gw), lambda b, n: (b, 0, n + npair)),
            pl.BlockSpec((LRU_CONV, gw), lambda b, n: (0, n)),
            pl.BlockSpec((1, gw), lambda b, n: (0, n)),
            pl.BlockSpec((2, 1, gw, 2 * gw), lambda b, n: (0, n, 0, 0)),
            pl.BlockSpec((2, 1, 1, 2 * gw), lambda b, n: (0, n, 0, 0)),
            pl.BlockSpec((2, gw), lambda b, n: (0, n)),
        ],
        out_specs=pl.BlockSpec((1, SEQ, gw), lambda b, n: (b, 0, n)),
        out_shape=jax.ShapeDtypeStruct((BATCH, SEQ, LRU_WP), BF16),
        scratch_shapes=[big(), slabs(), slabs(), big()],
        compiler_params=pltpu.CompilerParams(dimension_semantics=("parallel", "parallel"),
                                             vmem_limit_bytes=BIG_VMEM_LIMIT),
        name="rglru_block",
    )(p, conv_w, conv_b, w_gates, b_gates, lam)


def _gelu_gate_kernel(y_ref, gate_ref, o_ref):
    o_ref[0] = (_gelu_tanh(gate_ref[0]) * y_ref[0].astype(F32)).astype(o_ref.dtype)


def gelu_gate(y, p, tm=512):
    spec = pl.BlockSpec((1, tm, LRU_WP), lambda b, i: (b, i, 0))
    return pl.pallas_call(
        _gelu_gate_kernel,
        grid=(BATCH, SEQ // tm),
        in_specs=[spec, spec],
        out_specs=spec,
        out_shape=jax.ShapeDtypeStruct((BATCH, SEQ, LRU_WP), BF16),
        compiler_params=_cparams(("parallel", "parallel")),
        name="gelu_gate",
    )(y, p)


def _pad_blocks(a):
    lead = a.shape[:-1]
    a = a.reshape(*lead, LRU_BLOCKS, LRU_BW)
    a = jnp.pad(a, [(0, 0)] * len(lead) + [(0, 0), (0, LRU_PW - LRU_BW)])
    return a.reshape(*lead, LRU_WP)


def _pair_block_diagonal(w):
    pad = LRU_PW - LRU_BW
    w = jnp.pad(w, ((0, 0), (0, 0), (0, pad), (0, pad))).reshape(2, LRU_BLOCKS // 2, 2, LRU_PW, LRU_PW)
    zero = jnp.zeros_like(w[:, :, 0])
    top = jnp.concatenate([w[:, :, 0], zero], axis=-1)
    bottom = jnp.concatenate([zero, w[:, :, 1]], axis=-1)
    return jnp.concatenate([top, bottom], axis=-2)


def lru_mixer_layer(x, h, gate, w_in, conv_w, conv_b, w_a, b_a, w_x, b_x, lam, w_out):
    w_in_p = jnp.concatenate([_pad_blocks(w_in[:, :LRU_WIDTH]), _pad_blocks(w_in[:, LRU_WIDTH:])], axis=1)
    p = project(h, w_in_p.astype(BF16), F32, tn=MM_WIDE_COLS)
    npair = LRU_WP // LRU_GW
    w_gates = jnp.concatenate([_pair_block_diagonal(w_a), _pair_block_diagonal(w_x)], axis=-1)
    b_gates = jnp.concatenate([_pad_blocks(b_a).reshape(2, npair, 1, LRU_GW),
                               _pad_blocks(b_x).reshape(2, npair, 1, LRU_GW)], axis=-1)
    y = rglru_block(p, _pad_blocks(conv_w), _pad_blocks(conv_b).reshape(1, LRU_WP),
                    (0.5 * w_gates).astype(BF16), 0.5 * b_gates, _pad_blocks(lam))
    w_out_p = jnp.pad(w_out.reshape(LRU_BLOCKS, LRU_BW, D_MODEL), ((0, 0), (0, LRU_PW - LRU_BW), (0, 0)))
    return project_residual([gelu_gate(y, p)], [w_out_p.reshape(LRU_WP, D_MODEL).astype(BF16)], x, gate)


def _pack_halves(v):
    half = v.shape[1] // 2
    as_bits = lambda t: pltpu.bitcast(t.astype(BF16).astype(F32), jnp.uint32)
    word = as_bits(v[:, :half]) | (as_bits(v[:, half:]) >> 16)
    return pltpu.bitcast(word, jnp.int32)


def _unpack_halves(word):
    bits = pltpu.bitcast(word, jnp.uint32)
    first = pltpu.bitcast(bits & jnp.uint32(0xFFFF0000), F32)
    second = pltpu.bitcast(bits << 16, F32)
    return first, second


def _router_kernel(x_ref, g_ref, sc_ref, sh_ref, wr_ref, h_ref, eid_ref, gt_ref):
    h = _modulated_norm(x_ref[0], g_ref[...], sc_ref[0], sh_ref[0])
    hb = h.astype(BF16)
    h_ref[0] = _pack_halves(h)
    h_lo = (h - hb.astype(F32)).astype(BF16)
    w = wr_ref[...]
    w_hi = w.astype(BF16)
    w_lo = (w - w_hi.astype(F32)).astype(BF16)
    lg = (jnp.dot(hb, w_hi, preferred_element_type=F32) + jnp.dot(h_lo, w_hi, preferred_element_type=F32)
          + jnp.dot(hb, w_lo, preferred_element_type=F32))
    lt = lg.T
    tm = lt.shape[1]
    row = lax.broadcasted_iota(jnp.int32, (8, tm), 0)
    big = jnp.int32(99)
    gl = jnp.where(row < N_GROUPS, lt[0:8], -jnp.inf)
    g_max = jnp.max(gl, axis=0, keepdims=True)
    g_sel = jnp.min(jnp.where(gl == g_max, row, big), axis=0, keepdims=True)
    g_prob = 1.0 / jnp.sum(jnp.exp(gl - g_max), axis=0, keepdims=True)
    el = jnp.zeros((8, tm), F32)
    for g in range(N_GROUPS):
        el = jnp.where(g_sel == g, lt[8 + 8 * g:16 + 8 * g], el)
    v1 = jnp.max(el, axis=0, keepdims=True)
    i1 = jnp.min(jnp.where(el == v1, row, big), axis=0, keepdims=True)
    el2 = jnp.where(row == i1, -jnp.inf, el)
    v2 = jnp.max(el2, axis=0, keepdims=True)
    i2 = jnp.min(jnp.where(el2 == v2, row, big), axis=0, keepdims=True)
    e2 = jnp.exp(v2 - v1)
    p1 = 1.0 / (1.0 + e2)
    p2 = e2 * p1
    eid_ref[0] = jnp.where(row == 0, g_sel * EXPERTS_PER_GROUP + i1,
                           jnp.where(row == 1, g_sel * EXPERTS_PER_GROUP + i2, 0))
    rows = lax.broadcasted_iota(jnp.int32, (ROUTER_LANES, tm), 0)
    gates = jnp.where(rows == 0, g_prob * p1, jnp.where(rows == 1, g_prob * p2, 0.0))
    gt_ref[0] = gates.T


def route(x, g, scale, shift, w_router, tm=512):
    return pl.pallas_call(
        _router_kernel,
        grid=(BATCH, SEQ // tm),
        in_specs=[
            pl.BlockSpec((1, tm, D_MODEL), lambda b, i: (b, i, 0)),
            pl.BlockSpec((1, D_MODEL), lambda b, i: (0, 0)),
            pl.BlockSpec((1, 1, D_MODEL), lambda b, i: (b, 0, 0)),
            pl.BlockSpec((1, 1, D_MODEL), lambda b, i: (b, 0, 0)),
            pl.BlockSpec((D_MODEL, ROUTER_LANES), lambda b, i: (0, 0)),
        ],
        out_specs=[
            pl.BlockSpec((1, tm, D_MODEL // 2), lambda b, i: (b, i, 0)),
            pl.BlockSpec((1, 8, tm), lambda b, i: (b, 0, i)),
            pl.BlockSpec((1, tm, ROUTER_LANES), lambda b, i: (b, i, 0)),
        ],
        out_shape=[
            jax.ShapeDtypeStruct((BATCH, SEQ, D_MODEL // 2), jnp.int32),
            jax.ShapeDtypeStruct((BATCH, 8, SEQ), jnp.int32),
            jax.ShapeDtypeStruct((BATCH, SEQ, ROUTER_LANES), F32),
        ],
        compiler_params=_cparams(("parallel", "parallel")),
        name="route",
    )(x, g.reshape(1, D_MODEL), scale, shift, w_router)


def _expert_kernel(be_ref, nu_ref, first_ref, nxt_ref, x_ref, wgu_hbm, wd_hbm, o_ref,
                   gu_stage, d_stage, wgu, wd, sem):
    i = pl.program_id(0)
    active = i < nu_ref[0]
    is_first = first_ref[i] == 1
    half = D_MODEL // 2

    def weight_copies(e):
        return (pltpu.make_async_copy(wgu_hbm.at[e], gu_stage, sem.at[0]),
                pltpu.make_async_copy(wd_hbm.at[e], d_stage, sem.at[1]))

    def swiglu(gu):
        g = gu[:, :EXPERT_FF]
        return (g * _sigmoid(g) * gu[:, EXPERT_FF:]).astype(BF16)

    def rounded_matmul(lhs, stage, dst):
        acc = None
        for r0 in range(0, stage.shape[0], MOE_CAST_ROWS):
            rows = slice(r0, r0 + MOE_CAST_ROWS)
            w = stage[rows, :].astype(BF16)
            dst[rows, :] = w
            part = jnp.dot(lhs[:, rows], w, preferred_element_type=F32)
            acc = part if acc is None else acc + part
        return acc

    @pl.when(i == 0)
    def _():
        for cp in weight_copies(be_ref[0]):
            cp.start()

    @pl.when(active & is_first)
    def _():
        for cp in weight_copies(be_ref[i]):
            cp.wait()
        x_first, x_second = _unpack_halves(x_ref[...])
        x = jnp.concatenate([x_first.astype(BF16), x_second.astype(BF16)], axis=1)
        act = swiglu(rounded_matmul(x, gu_stage, wgu))
        o_ref[...] = _pack_halves(rounded_matmul(act, d_stage, wd))

        @pl.when(nxt_ref[i] >= 0)
        def _():
            for cp in weight_copies(nxt_ref[i]):
                cp.start(priority=1)

    @pl.when(active & jnp.logical_not(is_first))
    def _():
        x_first, x_second = _unpack_halves(x_ref[...])
        gu = (jnp.dot(x_first.astype(BF16), wgu[0:half, :], preferred_element_type=F32)
              + jnp.dot(x_second.astype(BF16), wgu[half:, :], preferred_element_type=F32))
        o_ref[...] = _pack_halves(jnp.dot(swiglu(gu), wd[...], preferred_element_type=F32))

    @pl.when(jnp.logical_not(active))
    def _():
        o_ref[...] = jnp.zeros(o_ref.shape, o_ref.dtype)


def expert_blocks(xs, blk_expert, n_used, first, nxt, w_gate_up, w_down):
    row_map = lambda i, be, nu, first, nxt: (jnp.minimum(i, nu[0] - 1), 0)
    return pl.pallas_call(
        _expert_kernel,
        grid_spec=pltpu.PrefetchScalarGridSpec(
            num_scalar_prefetch=4,
            grid=(MOE_NBLK,),
            in_specs=[
                pl.BlockSpec((MOE_TB, D_MODEL // 2), row_map),
                pl.BlockSpec(memory_space=pl.ANY),
                pl.BlockSpec(memory_space=pl.ANY),
            ],
            out_specs=pl.BlockSpec((MOE_TB, D_MODEL // 2), lambda i, be, nu, first, nxt: (i, 0)),
            scratch_shapes=[
                pltpu.VMEM((D_MODEL, 2 * EXPERT_FF), F32),
                pltpu.VMEM((EXPERT_FF, D_MODEL), F32),
                pltpu.VMEM((D_MODEL, 2 * EXPERT_FF), BF16),
                pltpu.VMEM((EXPERT_FF, D_MODEL), BF16),
                pltpu.SemaphoreType.DMA((2,)),
            ],
        ),
        out_shape=jax.ShapeDtypeStruct((MOE_ROWS, D_MODEL // 2), jnp.int32),
        compiler_params=pltpu.CompilerParams(dimension_semantics=("arbitrary",), vmem_limit_bytes=BIG_VMEM_LIMIT),
        name="expert_blocks",
    )(blk_expert, n_used, first, nxt, xs, w_gate_up, w_down)


def _router_weights(w_group, w_expert):
    w = jnp.zeros((D_MODEL, ROUTER_LANES), F32)
    w = w.at[:, 0:N_GROUPS].set(w_group)
    return w.at[:, 8:8 + N_EXPERTS].set(w_expert)


def _route_meta_kernel(e_ref, dest_ref, be_ref, first_ref, nxt_ref, nu_ref):
    nrow = e_ref.shape[0]
    expert = lax.broadcasted_iota(jnp.int32, (N_EXPERTS, META_LANES), 0)
    upto = (lax.broadcasted_iota(jnp.int32, (META_LANES, META_LANES), 0)
            <= lax.broadcasted_iota(jnp.int32, (META_LANES, META_LANES), 1)).astype(BF16)

    def count_row(c, acc):
        return acc + jnp.where(expert == e_ref[c], 1.0, 0.0)

    acc = lax.fori_loop(0, nrow, count_row, jnp.zeros((N_EXPERTS, META_LANES), F32))
    counts = jnp.sum(acc, axis=1, keepdims=True)
    pcounts = jnp.floor((counts + (MOE_TB - 1.0)) * (1.0 / MOE_TB)) * MOE_TB
    ends = []
    run = jnp.zeros((1, 1), F32)
    for e in range(N_EXPERTS):
        run = run + pcounts[e:e + 1, :]
        ends.append(run)
    pend = jnp.concatenate(ends, axis=0)
    pstart = pend - pcounts

    def dest_row(c, running):
        hit = expert == e_ref[c]
        seen = jnp.dot(jnp.where(hit, 1.0, 0.0).astype(BF16), upto, preferred_element_type=F32)
        slot = seen - 1.0 + (running + pstart)
        dest_ref[c] = jnp.sum(jnp.where(hit, slot, 0.0), axis=0, keepdims=True).astype(jnp.int32)
        return running + seen[:, META_LANES - 1:META_LANES]

    lax.fori_loop(0, nrow, dest_row, jnp.zeros((N_EXPERTS, 1), F32))

    blk = lax.broadcasted_iota(jnp.int32, (1, META_LANES), 1).astype(F32)
    n_used = pend[N_EXPERTS - 1:N_EXPERTS, :] * (1.0 / MOE_TB)

    def expert_at(b):
        start = jnp.minimum(b, n_used - 1.0) * MOE_TB
        return jnp.minimum(jnp.sum(jnp.where(pend <= start, 1.0, 0.0), axis=0, keepdims=True), N_EXPERTS - 1.0)

    be = expert_at(blk)
    is_first = jnp.logical_and(jnp.logical_or(blk == 0.0, be != expert_at(blk - 1.0)), blk < n_used)
    later = jnp.logical_and(expert.astype(F32) > be, counts > 0.0)
    nxt = jnp.min(jnp.where(later, expert.astype(F32), 2.0 * N_EXPERTS), axis=0, keepdims=True)
    be_ref[...] = be.astype(jnp.int32)
    first_ref[...] = jnp.where(is_first, 1, 0)
    nxt_ref[...] = jnp.where(nxt < N_EXPERTS, nxt, -1.0).astype(jnp.int32)
    nu_ref[...] = jnp.broadcast_to(n_used, (1, META_LANES)).astype(jnp.int32)


def route_metadata(e_flat):
    nrow = e_flat.shape[0] // META_LANES
    lane_row = jax.ShapeDtypeStruct((1, META_LANES), jnp.int32)
    dest, be, first, nxt, nu = pl.pallas_call(
        _route_meta_kernel,
        out_shape=[jax.ShapeDtypeStruct((nrow, 1, META_LANES), jnp.int32), lane_row, lane_row, lane_row, lane_row],
        compiler_params=pltpu.CompilerParams(vmem_limit_bytes=VMEM_LIMIT),
        name="route_metadata",
    )(e_flat.reshape(nrow, 1, META_LANES))
    return dest.reshape(-1), be[0, :MOE_NBLK], first[0, :MOE_NBLK], nxt[0, :MOE_NBLK], nu[0, :1]


def sc_move_rows(src, idx, n_out, scatter):
    n_idx = idx.shape[0]
    n_src, width = src.shape
    n_workers = SC_CORES * SC_SUBCORES
    per_w = n_idx // n_workers
    nchunk = per_w // SC_CHUNK
    assert per_w * n_workers == n_idx and nchunk * SC_CHUNK == per_w and nchunk % 2 == 0
    assert per_w % n_src == 0 or n_src % per_w == 0
    mesh = plsc.VectorSubcoreMesh(core_axis_name="c", subcore_axis_name="s")

    @functools.partial(
        pl.kernel, mesh=mesh,
        out_type=jax.ShapeDtypeStruct((n_out, width), src.dtype),
        scratch_types=[
            pltpu.VMEM((nchunk, SC_CHUNK), jnp.int32),
            pltpu.VMEM((2, SC_CHUNK, width), src.dtype),
            pltpu.SemaphoreType.DMA((2,)),
            pltpu.SemaphoreType.DMA((2,)),
        ],
    )
    def move(src_hbm, idx_hbm, out_hbm, idx_v, rows_v, in_sem, out_sem):
        wid = lax.axis_index("s") * SC_CORES + lax.axis_index("c")
        base = wid * per_w
        pltpu.sync_copy(idx_hbm.at[wid], idx_v)

        def load(c, b):
            if scatter:
                rows = src_hbm.at[pl.ds(lax.rem(base, n_src) + c * SC_CHUNK, SC_CHUNK)]
            else:
                rows = src_hbm.at[idx_v.at[c]]
            return pltpu.make_async_copy(rows, rows_v.at[b], in_sem.at[b])

        def store(c, b):
            if scatter:
                rows = out_hbm.at[idx_v.at[c]]
            else:
                rows = out_hbm.at[pl.ds(base + c * SC_CHUNK, SC_CHUNK)]
            return pltpu.make_async_copy(rows_v.at[b], rows, out_sem.at[b])

        load(0, 0).start()

        @pl.loop(0, nchunk, step=2)
        def _(c0):
            for b in (0, 1):
                c = c0 + b
                load(c, b).wait()

                @pl.when(c + 1 < nchunk)
                def _():
                    @pl.when(c >= 1)
                    def _():
                        store(c - 1, 1 - b).wait()

                    load(c + 1, 1 - b).start()

                store(c, b).start()

        store(nchunk - 2, 0).wait()
        store(nchunk - 1, 1).wait()

    return move(src, idx.reshape(n_workers, nchunk, SC_CHUNK))


def _combine_kernel(x_ref, z0_ref, z1_ref, gates_ref, ada_ref, ng_ref, sc_ref, sh_ref, *o_refs, final):
    gates = gates_ref[0]
    g0, g1 = gates[:, 0:1], gates[:, 1:2]
    z0_first, z0_second = _unpack_halves(z0_ref[...])
    z1_first, z1_second = _unpack_halves(z1_ref[...])
    y = jnp.concatenate([g0 * z0_first + g1 * z1_first, g0 * z0_second + g1 * z1_second], axis=1)
    x = x_ref[0] + ada_ref[0] * y
    if final:
        o_refs[0][0] = x * lax.rsqrt(jnp.mean(x * x, axis=-1, keepdims=True) + EPS) * ng_ref[...]
    else:
        o_refs[0][0] = x
        o_refs[1][0] = _modulated_norm(x, ng_ref[...], sc_ref[0], sh_ref[0]).astype(BF16)


def moe_combine(x, z, gates, gate_ada, norm_g, next_mod, tm=512):
    final = next_mod is None
    scale, shift = (gate_ada, gate_ada) if final else next_mod
    nt = SEQ // tm
    tile = pl.BlockSpec((1, tm, D_MODEL), lambda b, i: (b, i, 0))
    per_batch = pl.BlockSpec((1, 1, D_MODEL), lambda b, i: (b, 0, 0))
    f32_out = jax.ShapeDtypeStruct((BATCH, SEQ, D_MODEL), F32)
    return pl.pallas_call(
        functools.partial(_combine_kernel, final=final),
        grid=(BATCH, nt),
        in_specs=[
            tile,
            pl.BlockSpec((tm, D_MODEL // 2), lambda b, i: (b * nt + i, 0)),
            pl.BlockSpec((tm, D_MODEL // 2), lambda b, i: (BATCH * nt + b * nt + i, 0)),
            pl.BlockSpec((1, tm, ROUTER_LANES), lambda b, i: (b, i, 0)),
            per_batch,
            pl.BlockSpec((1, D_MODEL), lambda b, i: (0, 0)),
            per_batch,
            per_batch,
        ],
        out_specs=tile if final else [tile, tile],
        out_shape=f32_out if final else [f32_out, jax.ShapeDtypeStruct((BATCH, SEQ, D_MODEL), BF16)],
        compiler_params=_cparams(("parallel", "parallel")),
        name="moe_combine",
    )(x, z, z, gates, gate_ada, norm_g.reshape(1, D_MODEL), scale, shift)


def hierarchical_moe(x, c, norm_g, ada_w, ada_b, w_group, w_expert, w_gate_up, w_down, out_norm_g, next_mod=None):
    shift, scale, gate_ada = ada_modulation(c, ada_w, ada_b)
    h, eid, gates = route(x, norm_g, scale, shift, _router_weights(w_group, w_expert))
    e_flat = jnp.concatenate([eid[:, 0, :].reshape(N_TOK), eid[:, 1, :].reshape(N_TOK)])
    dest, blk_expert, first, nxt, n_used = route_metadata(e_flat)
    xs = sc_move_rows(h.reshape(N_TOK, D_MODEL // 2), dest, MOE_ROWS, scatter=True)
    yb = expert_blocks(xs, blk_expert, n_used, first, nxt, w_gate_up, w_down)
    z = sc_move_rows(yb, dest, TOP_K * N_TOK, scatter=False)
    return moe_combine(x, z, gates, gate_ada, out_norm_g, next_mod)


def kernel(x, c, norm0_mix, ada0_mix_w, ada0_mix_b, w_in0, conv_w, conv_b, conv_norm_g, conv_norm_b, w_out0, norm0_ffn, ada0_ffn_w, ada0_ffn_b, moe0_w_group, moe0_w_expert, moe0_w_gate_up, moe0_w_down, norm1_mix, ada1_mix_w, ada1_mix_b, w_in1, lru_conv_w, lru_conv_b, lru_w_a, lru_b_a, lru_w_x, lru_b_x, lru_lambda, w_out1, norm1_ffn, ada1_ffn_w, ada1_ffn_b, moe1_w_group, moe1_w_expert, moe1_w_gate_up, moe1_w_down, norm_final):
    x = conv_attention_layer(x, c, norm0_mix, ada0_mix_w, ada0_mix_b, w_in0, conv_w, conv_b, conv_norm_g, conv_norm_b,
                             w_out0)
    shift1, scale1, gate1 = ada_modulation(c, ada1_mix_w, ada1_mix_b)
    x, h = hierarchical_moe(x, c, norm0_ffn, ada0_ffn_w, ada0_ffn_b, moe0_w_group, moe0_w_expert,
                            moe0_w_gate_up, moe0_w_down, norm1_mix, next_mod=(scale1, shift1))
    x = lru_mixer_layer(x, h, gate1, w_in1, lru_conv_w, lru_conv_b, lru_w_a, lru_b_a, lru_w_x, lru_b_x, lru_lambda,
                        w_out1)
    return hierarchical_moe(x, c, norm1_ffn, ada1_ffn_w, ada1_ffn_b, moe1_w_group, moe1_w_expert,
                            moe1_w_gate_up, moe1_w_down, norm_final)
```
